```python
import math, functools
import jax, jax.numpy as jnp
from jax import lax
import numpy as np


D_MODEL = 2048
BATCH = 8
SEQ = 8192
DEPTH = 4

GRID_W = 64
CTX_LEN = 256
N_MIXERS = 2
N_MOD = 6
MLA_HEADS = 16
MLA_Q_RANK = 512
MLA_KV_RANK = 512
MLA_NOPE = 128
MLA_ROPE = 64
MLA_V = 128
GQA_HEADS = 16
GQA_KV_HEADS = 4
GQA_HEAD_DIM = 128
D_FF = 5632
CONV_W = 3

ROPE_BASE = 10000.0
EPS = 1e-6
Q_BLOCK = 128
N_MLA_LAYERS = (DEPTH + N_MIXERS - 1) // N_MIXERS
N_GQA_LAYERS = DEPTH // N_MIXERS
MLA_SCALE = 1.0 / math.sqrt(MLA_NOPE + MLA_ROPE)
GQA_SCALE = 1.0 / math.sqrt(GQA_HEAD_DIM)

kernel_name = "hybrid_mla_gqa_convffn_dit"


def rms_norm(x, g):
    xf = x.astype(jnp.float32)
    y = xf * lax.rsqrt(jnp.mean(xf * xf, axis=-1, keepdims=True) + EPS)
    return (y * g.astype(jnp.float32)).astype(x.dtype)


def modulate(x, g, shift, scale):
    return rms_norm(x, g) * (1.0 + scale) + shift


def axial_rope_tables(rows, cols, rot_dim):
    axis_dim = rot_dim // 2
    inv = jnp.power(ROPE_BASE, -jnp.arange(0, axis_dim, 2, dtype=jnp.float32) / axis_dim)
    ang_r = rows.astype(jnp.float32)[:, None] * inv
    ang_c = cols.astype(jnp.float32)[:, None] * inv
    ang = jnp.concatenate([ang_r, ang_r, ang_c, ang_c], axis=-1)
    return jnp.cos(ang), jnp.sin(ang)


def rotate_half(x):
    x1, x2 = jnp.split(x, 2, axis=-1)
    return jnp.concatenate([-x2, x1], axis=-1)


def apply_axial_rope(x, cos, sin):
    half = x.shape[-1] // 2
    rot = jnp.concatenate([rotate_half(x[..., :half]), rotate_half(x[..., half:])], axis=-1)
    return (x * cos[:, None, :] + rot * sin[:, None, :]).astype(x.dtype)


def attend(q, k, v, scale):
    B, Q, H, Dq = q.shape
    Hk = k.shape[2]
    qg = q.reshape(B, Q, Hk, H // Hk, Dq)
    s = jnp.einsum("bqkgd,btkd->bkgqt", qg, k, preferred_element_type=jnp.float32) * scale
    p = jax.nn.softmax(s, axis=-1)
    o = jnp.einsum("bkgqt,btkd->bqkgd", p.astype(v.dtype), v, preferred_element_type=jnp.float32)
    return o.reshape(B, Q, H, v.shape[-1]).astype(q.dtype)


def blocked_attention(q, k, v, scale):
    B, S, H, Dq = q.shape
    nb = S // Q_BLOCK
    qs = q.reshape(B, nb, Q_BLOCK, H, Dq).swapaxes(0, 1)
    o = lax.map(lambda qb: attend(qb, k, v, scale), qs)
    return o.swapaxes(0, 1).reshape(B, S, H, v.shape[-1])


def mla_queries(h, rope, w_dq, g_dq, w_uq, g_q_nope, g_q_pe):
    B, S, _ = h.shape
    cq = rms_norm(h @ w_dq, g_dq)
    q = (cq @ w_uq).reshape(B, S, MLA_HEADS, MLA_NOPE + MLA_ROPE)
    q_nope = rms_norm(q[..., :MLA_NOPE], g_q_nope)
    q_pe = rms_norm(q[..., MLA_NOPE:], g_q_pe)
    if rope is not None:
        q_pe = apply_axial_rope(q_pe, *rope)
    return jnp.concatenate([q_nope, q_pe], axis=-1)


def mla_keys_values(h, rope, w_dkv, g_dkv, g_k_pe, w_ukv, g_k_nope):
    B, S, _ = h.shape
    kv_a = h @ w_dkv
    c_kv = rms_norm(kv_a[..., :MLA_KV_RANK], g_dkv)
    k_pe = rms_norm(kv_a[..., MLA_KV_RANK:], g_k_pe)[:, :, None, :]
    if rope is not None:
        k_pe = apply_axial_rope(k_pe, *rope)
    kv = (c_kv @ w_ukv).reshape(B, S, MLA_HEADS, MLA_NOPE + MLA_V)
    k_nope = rms_norm(kv[..., :MLA_NOPE], g_k_nope)
    v = kv[..., MLA_NOPE:]
    k = jnp.concatenate([k_nope, jnp.broadcast_to(k_pe, (B, S, MLA_HEADS, MLA_ROPE))], axis=-1)
    return k, v


def gqa_queries(h, rope, w_q, g_q):
    B, S, _ = h.shape
    q = rms_norm((h @ w_q).reshape(B, S, GQA_HEADS, GQA_HEAD_DIM), g_q)
    if rope is not None:
        q = apply_axial_rope(q, *rope)
    return q


def gqa_keys_values(h, rope, w_kv, g_k):
    B, S, _ = h.shape
    kv = (h @ w_kv).reshape(B, S, 2, GQA_KV_HEADS, GQA_HEAD_DIM)
    k = rms_norm(kv[:, :, 0], g_k)
    v = kv[:, :, 1]
    if rope is not None:
        k = apply_axial_rope(k, *rope)
    return k, v


def depthwise_conv_centred(u, w, b):
    S = u.shape[1]
    pad = CONV_W // 2
    up = jnp.pad(u, ((0, 0), (pad, pad), (0, 0)))
    return sum(up[:, k:k + S] * w[k] for k in range(CONV_W)) + b


def conv_ffn(h, w_up, conv_w, conv_b, w_down):
    u = h @ w_up
    gate, val = u[..., :D_FF], u[..., D_FF:]
    gate = depthwise_conv_centred(gate, conv_w, conv_b)
    return (jax.nn.silu(gate) * val) @ w_down


def _fwd_setup_inputs(seed: int = 0) -> dict:
    key = jax.random.key(seed)
    ks = iter(jax.random.split(key, 40))
    D, L, LA, LB = D_MODEL, DEPTH, N_MLA_LAYERS, N_GQA_LAYERS

    def nrm(shape, scale):
        return jax.random.normal(next(ks), shape, jnp.float32) * scale

    def gain(shape):
        return 1.0 + nrm(shape, 0.02)

    return {
        "x": nrm((BATCH, SEQ, D), 1.0),
        "c": nrm((BATCH, D), 1.0),
        "ctx": nrm((BATCH, CTX_LEN, D), 1.0),
        "c_ctx": nrm((D,), 1.0),
        "w_mod": nrm((L, D, N_MOD * D), 0.5 * D ** -0.5),
        "b_mod": nrm((L, N_MOD * D), 0.01),
        "norm_mix": gain((L, D)),
        "norm_ffn": gain((L, D)),
        "mla_w_dq": nrm((LA, D, MLA_Q_RANK), D ** -0.5),
        "mla_g_dq": gain((LA, MLA_Q_RANK)),
        "mla_w_uq": nrm((LA, MLA_Q_RANK, MLA_HEADS * (MLA_NOPE + MLA_ROPE)), MLA_Q_RANK ** -0.5),
        "mla_g_q_nope": gain((LA, MLA_NOPE)),
        "mla_g_q_pe": gain((LA, MLA_ROPE)),
        "mla_w_dkv": nrm((LA, D, MLA_KV_RANK + MLA_ROPE), D ** -0.5),
        "mla_g_dkv": gain((LA, MLA_KV_RANK)),
        "mla_g_k_pe": gain((LA, MLA_ROPE)),
        "mla_w_ukv": nrm((LA, MLA_KV_RANK, MLA_HEADS * (MLA_NOPE + MLA_V)), MLA_KV_RANK ** -0.5),
        "mla_g_k_nope": gain((LA, MLA_NOPE)),
        "mla_w_o": nrm((LA, MLA_HEADS * MLA_V, D), (MLA_HEADS * MLA_V) ** -0.5),
        "gqa_w_q": nrm((LB, D, GQA_HEADS * GQA_HEAD_DIM), D ** -0.5),
        "gqa_g_q": gain((LB, GQA_HEAD_DIM)),
        "gqa_w_kv": nrm((LB, D, 2 * GQA_KV_HEADS * GQA_HEAD_DIM), D ** -0.5),
        "gqa_g_k": gain((LB, GQA_HEAD_DIM)),
        "gqa_w_o": nrm((LB, GQA_HEADS * GQA_HEAD_DIM, D), (GQA_HEADS * GQA_HEAD_DIM) ** -0.5),
        "ffn_w_up": nrm((L, D, 2 * D_FF), D ** -0.5),
        "ffn_conv_w": nrm((L, CONV_W, D_FF), CONV_W ** -0.5),
        "ffn_conv_b": nrm((L, D_FF), 0.01),
        "ffn_w_down": nrm((L, D_FF, D), D_FF ** -0.5),
    }


def _fwd_reference(x, c, ctx, c_ctx, w_mod, b_mod, norm_mix, norm_ffn,
              mla_w_dq, mla_g_dq, mla_w_uq, mla_g_q_nope, mla_g_q_pe,
              mla_w_dkv, mla_g_dkv, mla_g_k_pe, mla_w_ukv, mla_g_k_nope, mla_w_o,
              gqa_w_q, gqa_g_q, gqa_w_kv, gqa_g_k, gqa_w_o,
              ffn_w_up, ffn_conv_w, ffn_conv_b, ffn_w_down):
    B, S, _ = x.shape
    C = ctx.shape[1]
    ROWS = S // GRID_W
    rows = jnp.repeat(jnp.arange(ROWS, dtype=jnp.int32), GRID_W)
    cols = jnp.tile(jnp.arange(GRID_W, dtype=jnp.int32), ROWS)
    rope_mla = axial_rope_tables(rows, cols, MLA_ROPE)
    rope_gqa = axial_rope_tables(rows, cols, GQA_HEAD_DIM)
    silu_c = jax.nn.silu(c)
    silu_cc = jax.nn.silu(c_ctx)

    for i in range(DEPTH):
        last = i == DEPTH - 1
        j = i // N_MIXERS
        mod = (silu_c @ w_mod[i] + b_mod[i])[:, None, :]
        mod_c = silu_cc @ w_mod[i] + b_mod[i]
        sh1, sc1, g1, sh2, sc2, g2 = jnp.split(mod, N_MOD, axis=-1)
        csh1, csc1, cg1, csh2, csc2, cg2 = jnp.split(mod_c, N_MOD, axis=-1)

        h = modulate(x, norm_mix[i], sh1, sc1)
        hc = modulate(ctx, norm_mix[i], csh1, csc1)
        if i % N_MIXERS == 0:
            q_fn = functools.partial(mla_queries, w_dq=mla_w_dq[j], g_dq=mla_g_dq[j], w_uq=mla_w_uq[j],
                                     g_q_nope=mla_g_q_nope[j], g_q_pe=mla_g_q_pe[j])
            kv_fn = functools.partial(mla_keys_values, w_dkv=mla_w_dkv[j], g_dkv=mla_g_dkv[j],
                                      g_k_pe=mla_g_k_pe[j], w_ukv=mla_w_ukv[j], g_k_nope=mla_g_k_nope[j])
            w_o, rope, scale = mla_w_o[j], rope_mla, MLA_SCALE
        else:
            q_fn = functools.partial(gqa_queries, w_q=gqa_w_q[j], g_q=gqa_g_q[j])
            kv_fn = functools.partial(gqa_keys_values, w_kv=gqa_w_kv[j], g_k=gqa_g_k[j])
            w_o, rope, scale = gqa_w_o[j], rope_gqa, GQA_SCALE

        k_lat, v_lat = kv_fn(h, rope)
        k_ctx, v_ctx = kv_fn(hc, None)
        o = blocked_attention(q_fn(h, rope),
                              jnp.concatenate([k_lat, k_ctx], axis=1),
                              jnp.concatenate([v_lat, v_ctx], axis=1), scale)
        x = x + g1 * (o.reshape(B, S, -1) @ w_o)
        if not last:
            oc = attend(q_fn(hc, None), k_ctx, v_ctx, scale)
            ctx = ctx + cg1 * (oc.reshape(B, C, -1) @ w_o)

        x = x + g2 * conv_ffn(modulate(x, norm_ffn[i], sh2, sc2),
                              ffn_w_up[i], ffn_conv_w[i], ffn_conv_b[i], ffn_w_down[i])
        if not last:
            ctx = ctx + cg2 * conv_ffn(modulate(ctx, norm_ffn[i], csh2, csc2),
                                       ffn_w_up[i], ffn_conv_w[i], ffn_conv_b[i], ffn_w_down[i])
    return x


import jax as _jax
import jax.numpy as _jnp

TWIN_FORMAT = 'train_step'
FWD_PARAMS = ['x', 'c', 'ctx', 'c_ctx', 'w_mod', 'b_mod', 'norm_mix', 'norm_ffn', 'mla_w_dq', 'mla_g_dq', 'mla_w_uq', 'mla_g_q_nope', 'mla_g_q_pe', 'mla_w_dkv', 'mla_g_dkv', 'mla_g_k_pe', 'mla_w_ukv', 'mla_g_k_nope', 'mla_w_o', 'gqa_w_q', 'gqa_g_q', 'gqa_w_kv', 'gqa_g_k', 'gqa_w_o', 'ffn_w_up', 'ffn_conv_w', 'ffn_conv_b', 'ffn_w_down']
TWIN_WEIGHTS = ['c_ctx', 'w_mod', 'b_mod', 'norm_mix', 'norm_ffn', 'mla_w_dq', 'mla_g_dq', 'mla_w_uq', 'mla_g_q_nope', 'mla_g_q_pe', 'mla_w_dkv', 'mla_g_dkv', 'mla_g_k_pe', 'mla_w_ukv', 'mla_g_k_nope', 'mla_w_o', 'gqa_w_q', 'gqa_g_q', 'gqa_w_kv', 'gqa_g_k', 'gqa_w_o', 'ffn_w_up', 'ffn_conv_w', 'ffn_conv_b', 'ffn_w_down']
TWIN_DIFF_INPUT = 'x'
TWIN_INPUTS = ['x', 'c', 'ctx', 'c_ctx', 'w_mod', 'b_mod', 'norm_mix', 'norm_ffn', 'mla_w_dq', 'mla_g_dq', 'mla_w_uq', 'mla_g_q_nope', 'mla_g_q_pe', 'mla_w_dkv', 'mla_g_dkv', 'mla_g_k_pe', 'mla_w_ukv', 'mla_g_k_nope', 'mla_w_o', 'gqa_w_q', 'gqa_g_q', 'gqa_w_kv', 'gqa_g_k', 'gqa_w_o', 'ffn_w_up', 'ffn_conv_w', 'ffn_conv_b', 'ffn_w_down', 'loss_target', 'm_c_ctx', 'm_w_mod', 'm_b_mod', 'm_norm_mix', 'm_norm_ffn', 'm_mla_w_dq', 'm_mla_g_dq', 'm_mla_w_uq', 'm_mla_g_q_nope', 'm_mla_g_q_pe', 'm_mla_w_dkv', 'm_mla_g_dkv', 'm_mla_g_k_pe', 'm_mla_w_ukv', 'm_mla_g_k_nope', 'm_mla_w_o', 'm_gqa_w_q', 'm_gqa_g_q', 'm_gqa_w_kv', 'm_gqa_g_k', 'm_gqa_w_o', 'm_ffn_w_up', 'm_ffn_conv_w', 'm_ffn_conv_b', 'm_ffn_w_down', 'v_c_ctx', 'v_w_mod', 'v_b_mod', 'v_norm_mix', 'v_norm_ffn', 'v_mla_w_dq', 'v_mla_g_dq', 'v_mla_w_uq', 'v_mla_g_q_nope', 'v_mla_g_q_pe', 'v_mla_w_dkv', 'v_mla_g_dkv', 'v_mla_g_k_pe', 'v_mla_w_ukv', 'v_mla_g_k_nope', 'v_mla_w_o', 'v_gqa_w_q', 'v_gqa_g_q', 'v_gqa_w_kv', 'v_gqa_g_k', 'v_gqa_w_o', 'v_ffn_w_up', 'v_ffn_conv_w', 'v_ffn_conv_b', 'v_ffn_w_down']
TWIN_OUTPUTS = ['loss', 'grad_x', 'grad_c_ctx', 'grad_w_mod', 'grad_b_mod', 'grad_norm_mix', 'grad_norm_ffn', 'grad_mla_w_dq', 'grad_mla_g_dq', 'grad_mla_w_uq', 'grad_mla_g_q_nope', 'grad_mla_g_q_pe', 'grad_mla_w_dkv', 'grad_mla_g_dkv', 'grad_mla_g_k_pe', 'grad_mla_w_ukv', 'grad_mla_g_k_nope', 'grad_mla_w_o', 'grad_gqa_w_q', 'grad_gqa_g_q', 'grad_gqa_w_kv', 'grad_gqa_g_k', 'grad_gqa_w_o', 'grad_ffn_w_up', 'grad_ffn_conv_w', 'grad_ffn_conv_b', 'grad_ffn_w_down', 'delta_c_ctx', 'delta_w_mod', 'delta_b_mod', 'delta_norm_mix', 'delta_norm_ffn', 'delta_mla_w_dq', 'delta_mla_g_dq', 'delta_mla_w_uq', 'delta_mla_g_q_nope', 'delta_mla_g_q_pe', 'delta_mla_w_dkv', 'delta_mla_g_dkv', 'delta_mla_g_k_pe', 'delta_mla_w_ukv', 'delta_mla_g_k_nope', 'delta_mla_w_o', 'delta_gqa_w_q', 'delta_gqa_g_q', 'delta_gqa_w_kv', 'delta_gqa_g_k', 'delta_gqa_w_o', 'delta_ffn_w_up', 'delta_ffn_conv_w', 'delta_ffn_conv_b', 'delta_ffn_w_down', 'new_m_c_ctx', 'new_m_w_mod', 'new_m_b_mod', 'new_m_norm_mix', 'new_m_norm_ffn', 'new_m_mla_w_dq', 'new_m_mla_g_dq', 'new_m_mla_w_uq', 'new_m_mla_g_q_nope', 'new_m_mla_g_q_pe', 'new_m_mla_w_dkv', 'new_m_mla_g_dkv', 'new_m_mla_g_k_pe', 'new_m_mla_w_ukv', 'new_m_mla_g_k_nope', 'new_m_mla_w_o', 'new_m_gqa_w_q', 'new_m_gqa_g_q', 'new_m_gqa_w_kv', 'new_m_gqa_g_k', 'new_m_gqa_w_o', 'new_m_ffn_w_up', 'new_m_ffn_conv_w', 'new_m_ffn_conv_b', 'new_m_ffn_w_down', 'new_v_c_ctx', 'new_v_w_mod', 'new_v_b_mod', 'new_v_norm_mix', 'new_v_norm_ffn', 'new_v_mla_w_dq', 'new_v_mla_g_dq', 'new_v_mla_w_uq', 'new_v_mla_g_q_nope', 'new_v_mla_g_q_pe', 'new_v_mla_w_dkv', 'new_v_mla_g_dkv', 'new_v_mla_g_k_pe', 'new_v_mla_w_ukv', 'new_v_mla_g_k_nope', 'new_v_mla_w_o', 'new_v_gqa_w_q', 'new_v_gqa_g_q', 'new_v_gqa_w_kv', 'new_v_gqa_g_k', 'new_v_gqa_w_o', 'new_v_ffn_w_up', 'new_v_ffn_conv_w', 'new_v_ffn_conv_b', 'new_v_ffn_w_down']
TWIN_LEAF_KINDS = {'loss': 'loss', 'grad_x': 'grad_x', 'grad_c_ctx': 'grad_w', 'grad_w_mod': 'grad_w', 'grad_b_mod': 'grad_w', 'grad_norm_mix': 'grad_w', 'grad_norm_ffn': 'grad_w', 'grad_mla_w_dq': 'grad_w', 'grad_mla_g_dq': 'grad_w', 'grad_mla_w_uq': 'grad_w', 'grad_mla_g_q_nope': 'grad_w', 'grad_mla_g_q_pe': 'grad_w', 'grad_mla_w_dkv': 'grad_w', 'grad_mla_g_dkv': 'grad_w', 'grad_mla_g_k_pe': 'grad_w', 'grad_mla_w_ukv': 'grad_w', 'grad_mla_g_k_nope': 'grad_w', 'grad_mla_w_o': 'grad_w', 'grad_gqa_w_q': 'grad_w', 'grad_gqa_g_q': 'grad_w', 'grad_gqa_w_kv': 'grad_w', 'grad_gqa_g_k': 'grad_w', 'grad_gqa_w_o': 'grad_w', 'grad_ffn_w_up': 'grad_w', 'grad_ffn_conv_w': 'grad_w', 'grad_ffn_conv_b': 'grad_w', 'grad_ffn_w_down': 'grad_w', 'delta_c_ctx': 'delta_w', 'delta_w_mod': 'delta_w', 'delta_b_mod': 'delta_w', 'delta_norm_mix': 'delta_w', 'delta_norm_ffn': 'delta_w', 'delta_mla_w_dq': 'delta_w', 'delta_mla_g_dq': 'delta_w', 'delta_mla_w_uq': 'delta_w', 'delta_mla_g_q_nope': 'delta_w', 'delta_mla_g_q_pe': 'delta_w', 'delta_mla_w_dkv': 'delta_w', 'delta_mla_g_dkv': 'delta_w', 'delta_mla_g_k_pe': 'delta_w', 'delta_mla_w_ukv': 'delta_w', 'delta_mla_g_k_nope': 'delta_w', 'delta_mla_w_o': 'delta_w', 'delta_gqa_w_q': 'delta_w', 'delta_gqa_g_q': 'delta_w', 'delta_gqa_w_kv': 'delta_w', 'delta_gqa_g_k': 'delta_w', 'delta_gqa_w_o': 'delta_w', 'delta_ffn_w_up': 'delta_w', 'delta_ffn_conv_w': 'delta_w', 'delta_ffn_conv_b': 'delta_w', 'delta_ffn_w_down': 'delta_w', 'new_m_c_ctx': 'new_m', 'new_m_w_mod': 'new_m', 'new_m_b_mod': 'new_m', 'new_m_norm_mix': 'new_m', 'new_m_norm_ffn': 'new_m', 'new_m_mla_w_dq': 'new_m', 'new_m_mla_g_dq': 'new_m', 'new_m_mla_w_uq': 'new_m', 'new_m_mla_g_q_nope': 'new_m', 'new_m_mla_g_q_pe': 'new_m', 'new_m_mla_w_dkv': 'new_m', 'new_m_mla_g_dkv': 'new_m', 'new_m_mla_g_k_pe': 'new_m', 'new_m_mla_w_ukv': 'new_m', 'new_m_mla_g_k_nope': 'new_m', 'new_m_mla_w_o': 'new_m', 'new_m_gqa_w_q': 'new_m', 'new_m_gqa_g_q': 'new_m', 'new_m_gqa_w_kv': 'new_m', 'new_m_gqa_g_k': 'new_m', 'new_m_gqa_w_o': 'new_m', 'new_m_ffn_w_up': 'new_m', 'new_m_ffn_conv_w': 'new_m', 'new_m_ffn_conv_b': 'new_m', 'new_m_ffn_w_down': 'new_m', 'new_v_c_ctx': 'new_v', 'new_v_w_mod': 'new_v', 'new_v_b_mod': 'new_v', 'new_v_norm_mix': 'new_v', 'new_v_norm_ffn': 'new_v', 'new_v_mla_w_dq': 'new_v', 'new_v_mla_g_dq': 'new_v', 'new_v_mla_w_uq': 'new_v', 'new_v_mla_g_q_nope': 'new_v', 'new_v_mla_g_q_pe': 'new_v', 'new_v_mla_w_dkv': 'new_v', 'new_v_mla_g_dkv': 'new_v', 'new_v_mla_g_k_pe': 'new_v', 'new_v_mla_w_ukv': 'new_v', 'new_v_mla_g_k_nope': 'new_v', 'new_v_mla_w_o': 'new_v', 'new_v_gqa_w_q': 'new_v', 'new_v_gqa_g_q': 'new_v', 'new_v_gqa_w_kv': 'new_v', 'new_v_gqa_g_k': 'new_v', 'new_v_gqa_w_o': 'new_v', 'new_v_ffn_w_up': 'new_v', 'new_v_ffn_conv_w': 'new_v', 'new_v_ffn_conv_b': 'new_v', 'new_v_ffn_w_down': 'new_v'}


def _forward(args):
    return _fwd_reference(*[args[k] for k in FWD_PARAMS])


def _output_shape():
    def fwd():
        inp = _fwd_setup_inputs(0)
        return _fwd_reference(*[inp[k] for k in FWD_PARAMS])
    out = _jax.eval_shape(fwd)
    return out.shape, out.dtype

N_MICROBATCH = 1
ADAM_LR = 0.001
ADAM_B1 = 0.9
ADAM_B2 = 0.999
ADAM_EPS = 1e-08
ADAM_WD = 0.01
ADAM_STEP = 10
PER_EXAMPLE_BATCH_AXIS = {'x': 0, 'c': 0, 'ctx': 0, 'loss_target': 0}
SHARED_INPUTS = []
_WEIGHT_DTYPES = {'c_ctx': _jnp.float32, 'w_mod': _jnp.float32, 'b_mod': _jnp.float32, 'norm_mix': _jnp.float32, 'norm_ffn': _jnp.float32, 'mla_w_dq': _jnp.float32, 'mla_g_dq': _jnp.float32, 'mla_w_uq': _jnp.float32, 'mla_g_q_nope': _jnp.float32, 'mla_g_q_pe': _jnp.float32, 'mla_w_dkv': _jnp.float32, 'mla_g_dkv': _jnp.float32, 'mla_g_k_pe': _jnp.float32, 'mla_w_ukv': _jnp.float32, 'mla_g_k_nope': _jnp.float32, 'mla_w_o': _jnp.float32, 'gqa_w_q': _jnp.float32, 'gqa_g_q': _jnp.float32, 'gqa_w_kv': _jnp.float32, 'gqa_g_k': _jnp.float32, 'gqa_w_o': _jnp.float32, 'ffn_w_up': _jnp.float32, 'ffn_conv_w': _jnp.float32, 'ffn_conv_b': _jnp.float32, 'ffn_w_down': _jnp.float32}
MOMENT_SCALE = {'c_ctx': 1.414923e-01, 'w_mod': 6.287053e-01, 'b_mod': 1.633349e+00, 'norm_mix': 1.474010e-01, 'norm_ffn': 3.141238e+00, 'mla_w_dq': 2.142117e-02, 'mla_g_dq': 2.118799e-02, 'mla_w_uq': 8.374064e-03, 'mla_g_q_nope': 4.770847e-02, 'mla_g_q_pe': 3.506617e-02, 'mla_w_dkv': 4.338606e-01, 'mla_g_dkv': 1.019959e+00, 'mla_g_k_pe': 3.542444e-02, 'mla_w_ukv': 1.509456e-01, 'mla_g_k_nope': 4.789769e-02, 'mla_w_o': 2.174813e-01, 'gqa_w_q': 7.306669e-03, 'gqa_g_q': 5.584800e-02, 'gqa_w_kv': 3.431068e-01, 'gqa_g_k': 5.616336e-02, 'gqa_w_o': 2.229810e-01, 'ffn_w_up': 6.156355e-02, 'ffn_conv_w': 3.727113e-01, 'ffn_conv_b': 4.184998e-01, 'ffn_w_down': 6.792114e-02}


def _to_microbatches(a, axis):
    t = _jnp.moveaxis(a, axis, 0)
    t = t.reshape((N_MICROBATCH, t.shape[0] // N_MICROBATCH) + t.shape[1:])
    return _jnp.moveaxis(t, 1, axis + 1)


def setup_inputs(seed: int = 0) -> dict:
    inp = _fwd_setup_inputs(seed)
    key = _jax.random.fold_in(_jax.random.key(seed), 7919)
    shape, _ = _output_shape()
    out = dict(inp)
    out["loss_target"] = _jax.random.normal(_jax.random.fold_in(key, 0), shape, _jnp.float32)
    for i, name in enumerate(TWIN_WEIGHTS):
        w = inp[name].astype(_jnp.float32)
        if MOMENT_SCALE is None:
            s = _jnp.sqrt(_jnp.mean(_jnp.square(w)) + 1e-30)
        else:
            s = MOMENT_SCALE[name]
        km, kv = _jax.random.split(_jax.random.fold_in(key, i + 1))
        out[name] = w
        out["m_" + name] = s * _jax.random.normal(km, w.shape, _jnp.float32)
        out["v_" + name] = (s * s) * _jax.random.uniform(kv, w.shape, _jnp.float32, 0.5, 1.5)
    if N_MICROBATCH > 1:
        for name, axis in PER_EXAMPLE_BATCH_AXIS.items():
            out[name] = _to_microbatches(out[name], axis)
    return {'x': out['x'], 'c': out['c'], 'ctx': out['ctx'], 'c_ctx': out['c_ctx'], 'w_mod': out['w_mod'], 'b_mod': out['b_mod'], 'norm_mix': out['norm_mix'], 'norm_ffn': out['norm_ffn'], 'mla_w_dq': out['mla_w_dq'], 'mla_g_dq': out['mla_g_dq'], 'mla_w_uq': out['mla_w_uq'], 'mla_g_q_nope': out['mla_g_q_nope'], 'mla_g_q_pe': out['mla_g_q_pe'], 'mla_w_dkv': out['mla_w_dkv'], 'mla_g_dkv': out['mla_g_dkv'], 'mla_g_k_pe': out['mla_g_k_pe'], 'mla_w_ukv': out['mla_w_ukv'], 'mla_g_k_nope': out['mla_g_k_nope'], 'mla_w_o': out['mla_w_o'], 'gqa_w_q': out['gqa_w_q'], 'gqa_g_q': out['gqa_g_q'], 'gqa_w_kv': out['gqa_w_kv'], 'gqa_g_k': out['gqa_g_k'], 'gqa_w_o': out['gqa_w_o'], 'ffn_w_up': out['ffn_w_up'], 'ffn_conv_w': out['ffn_conv_w'], 'ffn_conv_b': out['ffn_conv_b'], 'ffn_w_down': out['ffn_w_down'], 'loss_target': out['loss_target'], 'm_c_ctx': out['m_c_ctx'], 'm_w_mod': out['m_w_mod'], 'm_b_mod': out['m_b_mod'], 'm_norm_mix': out['m_norm_mix'], 'm_norm_ffn': out['m_norm_ffn'], 'm_mla_w_dq': out['m_mla_w_dq'], 'm_mla_g_dq': out['m_mla_g_dq'], 'm_mla_w_uq': out['m_mla_w_uq'], 'm_mla_g_q_nope': out['m_mla_g_q_nope'], 'm_mla_g_q_pe': out['m_mla_g_q_pe'], 'm_mla_w_dkv': out['m_mla_w_dkv'], 'm_mla_g_dkv': out['m_mla_g_dkv'], 'm_mla_g_k_pe': out['m_mla_g_k_pe'], 'm_mla_w_ukv': out['m_mla_w_ukv'], 'm_mla_g_k_nope': out['m_mla_g_k_nope'], 'm_mla_w_o': out['m_mla_w_o'], 'm_gqa_w_q': out['m_gqa_w_q'], 'm_gqa_g_q': out['m_gqa_g_q'], 'm_gqa_w_kv': out['m_gqa_w_kv'], 'm_gqa_g_k': out['m_gqa_g_k'], 'm_gqa_w_o': out['m_gqa_w_o'], 'm_ffn_w_up': out['m_ffn_w_up'], 'm_ffn_conv_w': out['m_ffn_conv_w'], 'm_ffn_conv_b': out['m_ffn_conv_b'], 'm_ffn_w_down': out['m_ffn_w_down'], 'v_c_ctx': out['v_c_ctx'], 'v_w_mod': out['v_w_mod'], 'v_b_mod': out['v_b_mod'], 'v_norm_mix': out['v_norm_mix'], 'v_norm_ffn': out['v_norm_ffn'], 'v_mla_w_dq': out['v_mla_w_dq'], 'v_mla_g_dq': out['v_mla_g_dq'], 'v_mla_w_uq': out['v_mla_w_uq'], 'v_mla_g_q_nope': out['v_mla_g_q_nope'], 'v_mla_g_q_pe': out['v_mla_g_q_pe'], 'v_mla_w_dkv': out['v_mla_w_dkv'], 'v_mla_g_dkv': out['v_mla_g_dkv'], 'v_mla_g_k_pe': out['v_mla_g_k_pe'], 'v_mla_w_ukv': out['v_mla_w_ukv'], 'v_mla_g_k_nope': out['v_mla_g_k_nope'], 'v_mla_w_o': out['v_mla_w_o'], 'v_gqa_w_q': out['v_gqa_w_q'], 'v_gqa_g_q': out['v_gqa_g_q'], 'v_gqa_w_kv': out['v_gqa_w_kv'], 'v_gqa_g_k': out['v_gqa_g_k'], 'v_gqa_w_o': out['v_gqa_w_o'], 'v_ffn_w_up': out['v_ffn_w_up'], 'v_ffn_conv_w': out['v_ffn_conv_w'], 'v_ffn_conv_b': out['v_ffn_conv_b'], 'v_ffn_w_down': out['v_ffn_w_down']}


def _loss(weights, diff, rest, loss_target):
    with _jax.named_scope("forward"):
        args = {**rest, TWIN_DIFF_INPUT: diff, **{k: w.astype(_WEIGHT_DTYPES[k]) for k, w in weights.items()}}
        y = _forward(args)
    with _jax.named_scope("loss_head"):
        err = _jnp.square(y.astype(_jnp.float32) - loss_target)
        return 0.5 * _jnp.sum(_jnp.mean(err, axis=-1)) if err.ndim else 0.5 * err


def _adamw(w, g, m, v):
    m = ADAM_B1 * m + (1.0 - ADAM_B1) * g
    v = ADAM_B2 * v + (1.0 - ADAM_B2) * _jnp.square(g)
    m_hat = m / (1.0 - ADAM_B1 ** ADAM_STEP)
    v_hat = v / (1.0 - ADAM_B2 ** ADAM_STEP)
    delta = -ADAM_LR * (m_hat / (_jnp.sqrt(v_hat) + ADAM_EPS) + ADAM_WD * w)
    return delta, m, v


def reference(x, c, ctx, c_ctx, w_mod, b_mod, norm_mix, norm_ffn, mla_w_dq, mla_g_dq, mla_w_uq, mla_g_q_nope, mla_g_q_pe, mla_w_dkv, mla_g_dkv, mla_g_k_pe, mla_w_ukv, mla_g_k_nope, mla_w_o, gqa_w_q, gqa_g_q, gqa_w_kv, gqa_g_k, gqa_w_o, ffn_w_up, ffn_conv_w, ffn_conv_b, ffn_w_down, loss_target, m_c_ctx, m_w_mod, m_b_mod, m_norm_mix, m_norm_ffn, m_mla_w_dq, m_mla_g_dq, m_mla_w_uq, m_mla_g_q_nope, m_mla_g_q_pe, m_mla_w_dkv, m_mla_g_dkv, m_mla_g_k_pe, m_mla_w_ukv, m_mla_g_k_nope, m_mla_w_o, m_gqa_w_q, m_gqa_g_q, m_gqa_w_kv, m_gqa_g_k, m_gqa_w_o, m_ffn_w_up, m_ffn_conv_w, m_ffn_conv_b, m_ffn_w_down, v_c_ctx, v_w_mod, v_b_mod, v_norm_mix, v_norm_ffn, v_mla_w_dq, v_mla_g_dq, v_mla_w_uq, v_mla_g_q_nope, v_mla_g_q_pe, v_mla_w_dkv, v_mla_g_dkv, v_mla_g_k_pe, v_mla_w_ukv, v_mla_g_k_nope, v_mla_w_o, v_gqa_w_q, v_gqa_g_q, v_gqa_w_kv, v_gqa_g_k, v_gqa_w_o, v_ffn_w_up, v_ffn_conv_w, v_ffn_conv_b, v_ffn_w_down):
    given = dict(x=x, c=c, ctx=ctx, c_ctx=c_ctx, w_mod=w_mod, b_mod=b_mod, norm_mix=norm_mix, norm_ffn=norm_ffn, mla_w_dq=mla_w_dq, mla_g_dq=mla_g_dq, mla_w_uq=mla_w_uq, mla_g_q_nope=mla_g_q_nope, mla_g_q_pe=mla_g_q_pe, mla_w_dkv=mla_w_dkv, mla_g_dkv=mla_g_dkv, mla_g_k_pe=mla_g_k_pe, mla_w_ukv=mla_w_ukv, mla_g_k_nope=mla_g_k_nope, mla_w_o=mla_w_o, gqa_w_q=gqa_w_q, gqa_g_q=gqa_g_q, gqa_w_kv=gqa_w_kv, gqa_g_k=gqa_g_k, gqa_w_o=gqa_w_o, ffn_w_up=ffn_w_up, ffn_conv_w=ffn_conv_w, ffn_conv_b=ffn_conv_b, ffn_w_down=ffn_w_down, loss_target=loss_target, m_c_ctx=m_c_ctx, m_w_mod=m_w_mod, m_b_mod=m_b_mod, m_norm_mix=m_norm_mix, m_norm_ffn=m_norm_ffn, m_mla_w_dq=m_mla_w_dq, m_mla_g_dq=m_mla_g_dq, m_mla_w_uq=m_mla_w_uq, m_mla_g_q_nope=m_mla_g_q_nope, m_mla_g_q_pe=m_mla_g_q_pe, m_mla_w_dkv=m_mla_w_dkv, m_mla_g_dkv=m_mla_g_dkv, m_mla_g_k_pe=m_mla_g_k_pe, m_mla_w_ukv=m_mla_w_ukv, m_mla_g_k_nope=m_mla_g_k_nope, m_mla_w_o=m_mla_w_o, m_gqa_w_q=m_gqa_w_q, m_gqa_g_q=m_gqa_g_q, m_gqa_w_kv=m_gqa_w_kv, m_gqa_g_k=m_gqa_g_k, m_gqa_w_o=m_gqa_w_o, m_ffn_w_up=m_ffn_w_up, m_ffn_conv_w=m_ffn_conv_w, m_ffn_conv_b=m_ffn_conv_b, m_ffn_w_down=m_ffn_w_down, v_c_ctx=v_c_ctx, v_w_mod=v_w_mod, v_b_mod=v_b_mod, v_norm_mix=v_norm_mix, v_norm_ffn=v_norm_ffn, v_mla_w_dq=v_mla_w_dq, v_mla_g_dq=v_mla_g_dq, v_mla_w_uq=v_mla_w_uq, v_mla_g_q_nope=v_mla_g_q_nope, v_mla_g_q_pe=v_mla_g_q_pe, v_mla_w_dkv=v_mla_w_dkv, v_mla_g_dkv=v_mla_g_dkv, v_mla_g_k_pe=v_mla_g_k_pe, v_mla_w_ukv=v_mla_w_ukv, v_mla_g_k_nope=v_mla_g_k_nope, v_mla_w_o=v_mla_w_o, v_gqa_w_q=v_gqa_w_q, v_gqa_g_q=v_gqa_g_q, v_gqa_w_kv=v_gqa_w_kv, v_gqa_g_k=v_gqa_g_k, v_gqa_w_o=v_gqa_w_o, v_ffn_w_up=v_ffn_w_up, v_ffn_conv_w=v_ffn_conv_w, v_ffn_conv_b=v_ffn_conv_b, v_ffn_w_down=v_ffn_w_down)
    weights = {n: given[n] for n in TWIN_WEIGHTS}
    shared = {n: given[n] for n in SHARED_INPUTS}
    per_example = {n: given[n] for n in ['x', 'c', 'ctx']}
    grad_fn = _jax.value_and_grad(_loss, argnums=(0, 1))

    def one_microbatch(ex, loss_target):
        ex = dict(ex)
        diff = ex.pop(TWIN_DIFF_INPUT)
        return grad_fn(weights, diff, {**shared, **ex}, loss_target)

    if N_MICROBATCH == 1:
        loss, (grad_w, grad_x) = one_microbatch(per_example, given["loss_target"])
    else:
        def body(carry, xs):
            loss_sum, grad_sum = carry
            l_k, (gw_k, gx_k) = one_microbatch(xs[0], xs[1])
            with _jax.named_scope("update"):
                return (loss_sum + l_k, _jax.tree.map(_jnp.add, grad_sum, gw_k)), gx_k

        init = (_jnp.zeros((), _jnp.float32), _jax.tree.map(_jnp.zeros_like, weights))
        (loss, grad_w), grad_x = _jax.lax.scan(body, init, (per_example, given["loss_target"]))
    with _jax.named_scope("update"):
        delta_w, new_m, new_v = {}, {}, {}
        for n in TWIN_WEIGHTS:
            delta_w[n], new_m[n], new_v[n] = _adamw(weights[n], grad_w[n], given["m_" + n], given["v_" + n])
    return (loss, grad_x, *[grad_w[n] for n in TWIN_WEIGHTS], *[delta_w[n] for n in TWIN_WEIGHTS],
            *[new_m[n] for n in TWIN_WEIGHTS], *[new_v[n] for n in TWIN_WEIGHTS])
```

```python
import functools

import jax
import jax.numpy as jnp
from jax import lax
from jax.experimental import pallas as pl
from jax.experimental.pallas import tpu as pltpu

F32 = jnp.float32
BF16 = jnp.bfloat16

EPS = 1e-6
ROPE_BASE = 10000.0
GRID_W = 64
N_MOD = 6
MLA_NOPE = 128
MLA_ROPE = 64
MLA_V = 128
GQA_HEAD_DIM = 128
GQA_KV_HEADS = 4
ADAM_LR = 0.001
ADAM_B1 = 0.9
ADAM_B2 = 0.999
ADAM_EPS = 1e-08
ADAM_WD = 0.01
ADAM_STEP = 10
N_DEV = 8
MOD_ROWS = 16

VMEM_LIMIT = 48 * 1024 * 1024
LANES = 128
SUBLANES = 8


def _pc(body, **kw):
    return pl.pallas_call(body, **kw)


def _cp(*sem):
    return pltpu.CompilerParams(dimension_semantics=sem if sem else None, vmem_limit_bytes=VMEM_LIMIT)


def _tile(n, cap, mult=LANES):
    if n <= cap:
        return n
    t = (cap // mult) * mult
    while t >= mult:
        if n % t == 0:
            return t
        t -= mult
    return n


def _matmul(a, b, *, ta=False, tb=False, name):
    assert not (ta and tb)
    kd, m = a.shape if ta else a.shape[::-1]
    n, kb = b.shape if tb else b.shape[::-1]
    assert kd == kb, (a.shape, b.shape, ta, tb)
    tm, tn, tk = _tile(m, 1024), _tile(n, 1024), _tile(kd, 1024)
    nk = kd // tk
    if ta:
        dims = (((0,), (0,)), ((), ()))
        a_spec = pl.BlockSpec((tk, tm), lambda i, j, k: (k, i))
    else:
        dims = (((1,), (1 if tb else 0,)), ((), ()))
        a_spec = pl.BlockSpec((tm, tk), lambda i, j, k: (i, k))
    if tb:
        b_spec = pl.BlockSpec((tn, tk), lambda i, j, k: (j, k))
    else:
        b_spec = pl.BlockSpec((tk, tn), lambda i, j, k: (k, j))

    def body(a_ref, b_ref, o_ref, acc_ref):
        k = pl.program_id(2)

        @pl.when(k == 0)
        def _():
            acc_ref[...] = jnp.zeros_like(acc_ref)

        acc_ref[...] += lax.dot_general(a_ref[...].astype(BF16), b_ref[...].astype(BF16), dims,
                                        preferred_element_type=F32)

        @pl.when(k == nk - 1)
        def _():
            o_ref[...] = acc_ref[...]

    return _pc(
        body, name=name, grid=(m // tm, n // tn, nk),
        in_specs=[a_spec, b_spec],
        out_specs=pl.BlockSpec((tm, tn), lambda i, j, k: (i, j)),
        out_shape=jax.ShapeDtypeStruct((m, n), F32),
        scratch_shapes=[pltpu.VMEM((tm, tn), F32)],
        compiler_params=_cp("parallel", "parallel", "arbitrary"),
    )(a, b)


def _mm(name):
    @jax.custom_vjp
    def f(a, w):
        return _matmul(a, w, name=name + "_fwd")

    def fwd(a, w):
        return f(a, w), (a, w)

    def bwd(res, dc):
        a, w = res
        return _matmul(dc, w, tb=True, name=name + "_da"), _matmul(a, dc, ta=True, name=name + "_dw")

    f.defvjp(fwd, bwd)
    return f


def _row_tile(r, d):
    return _tile(r, max(SUBLANES, min(4096, (2 << 20) // (4 * d))), SUBLANES)


def _norm_fwd_call(x, g, sc, sh, name):
    r, d = x.shape
    tr = _row_tile(r, d)
    mod = sc is not None

    def body(*refs):
        if mod:
            x_ref, g_ref, sc_ref, sh_ref, y_ref = refs
        else:
            x_ref, g_ref, y_ref = refs
        xv = x_ref[...]
        y = xv * lax.rsqrt(jnp.mean(xv * xv, axis=-1, keepdims=True) + EPS) * g_ref[...]
        if mod:
            y = y * (1.0 + sc_ref[...]) + sh_ref[...]
        y_ref[...] = y

    vec = pl.BlockSpec((1, d), lambda i: (0, 0))
    args = [x, g.reshape(1, d)] + ([sc.reshape(1, d), sh.reshape(1, d)] if mod else [])
    return _pc(
        body, name=name, grid=(r // tr,),
        in_specs=[pl.BlockSpec((tr, d), lambda i: (i, 0))] + [vec] * (len(args) - 1),
        out_specs=pl.BlockSpec((tr, d), lambda i: (i, 0)),
        out_shape=jax.ShapeDtypeStruct((r, d), F32),
        compiler_params=_cp("parallel"),
    )(*args)


def _norm_bwd_call(x, g, sc, dy, name):
    r, d = x.shape
    tr = _row_tile(r, d)
    mod = sc is not None

    def body(*refs):
        if mod:
            x_ref, g_ref, sc_ref, dy_ref, dx_ref, dg_ref, dsc_ref, dsh_ref = refs
        else:
            x_ref, g_ref, dy_ref, dx_ref, dg_ref = refs

        @pl.when(pl.program_id(0) == 0)
        def _():
            dg_ref[...] = jnp.zeros_like(dg_ref)
            if mod:
                dsc_ref[...] = jnp.zeros_like(dsc_ref)
                dsh_ref[...] = jnp.zeros_like(dsh_ref)

        xv = x_ref[...]
        gv = g_ref[...]
        dyv = dy_ref[...]
        rs = lax.rsqrt(jnp.mean(xv * xv, axis=-1, keepdims=True) + EPS)
        xh = xv * rs
        if mod:
            dsc_ref[...] += jnp.sum(dyv * (xh * gv), axis=0, keepdims=True)
            dsh_ref[...] += jnp.sum(dyv, axis=0, keepdims=True)
            t = dyv * (1.0 + sc_ref[...])
        else:
            t = dyv
        dg_ref[...] += jnp.sum(t * xh, axis=0, keepdims=True)
        dxh = t * gv
        dx_ref[...] = rs * (dxh - xh * jnp.mean(dxh * xh, axis=-1, keepdims=True))

    vec = pl.BlockSpec((1, d), lambda i: (0, 0))
    blk = pl.BlockSpec((tr, d), lambda i: (i, 0))
    args = [x, g.reshape(1, d)] + ([sc.reshape(1, d)] if mod else []) + [dy]
    n_vec_out = 3 if mod else 1
    outs = _pc(
        body, name=name, grid=(r // tr,),
        in_specs=[blk] + [vec] * (len(args) - 2) + [blk],
        out_specs=[blk] + [vec] * n_vec_out,
        out_shape=[jax.ShapeDtypeStruct((r, d), F32)] + [jax.ShapeDtypeStruct((1, d), F32)] * n_vec_out,
        compiler_params=_cp("arbitrary"),
    )(*args)
    return outs


def _norm(name):
    @jax.custom_vjp
    def f(x, g):
        return _norm_fwd_call(x, g, None, None, name + "_fwd")

    def fwd(x, g):
        return f(x, g), (x, g)

    def bwd(res, dy):
        x, g = res
        dx, dg = _norm_bwd_call(x, g, None, dy, name + "_bwd")
        return dx, dg.reshape(g.shape)

    f.defvjp(fwd, bwd)
    return f


def _modulate(name):
    @jax.custom_vjp
    def f(x, g, sh, sc):
        return _norm_fwd_call(x, g, sc, sh, name + "_fwd")

    def fwd(x, g, sh, sc):
        return f(x, g, sh, sc), (x, g, sc)

    def bwd(res, dy):
        x, g, sc = res
        dx, dg, dsc, dsh = _norm_bwd_call(x, g, sc, dy, name + "_bwd")
        return dx, dg.reshape(g.shape), dsh.reshape(g.shape), dsc.reshape(g.shape)

    f.defvjp(fwd, bwd)
    return f


def _gres(name):
    def fwd_call(x, gate, y):
        r, d = x.shape
        tr = _row_tile(r, d)

        def body(x_ref, g_ref, y_ref, o_ref):
            o_ref[...] = x_ref[...] + g_ref[...] * y_ref[...]

        blk = pl.BlockSpec((tr, d), lambda i: (i, 0))
        return _pc(
            body, name=name + "_fwd", grid=(r // tr,),
            in_specs=[blk, pl.BlockSpec((1, d), lambda i: (0, 0)), blk], out_specs=blk,
            out_shape=jax.ShapeDtypeStruct((r, d), F32), compiler_params=_cp("parallel"),
        )(x, gate.reshape(1, d), y)

    def bwd_call(do, gate, y):
        r, d = do.shape
        tr = _row_tile(r, d)

        def body(do_ref, g_ref, y_ref, dy_ref, dg_ref):
            @pl.when(pl.program_id(0) == 0)
            def _():
                dg_ref[...] = jnp.zeros_like(dg_ref)

            dov = do_ref[...]
            dy_ref[...] = g_ref[...] * dov
            dg_ref[...] += jnp.sum(dov * y_ref[...], axis=0, keepdims=True)

        blk = pl.BlockSpec((tr, d), lambda i: (i, 0))
        vec = pl.BlockSpec((1, d), lambda i: (0, 0))
        return _pc(
            body, name=name + "_bwd", grid=(r // tr,),
            in_specs=[blk, vec, blk], out_specs=[blk, vec],
            out_shape=[jax.ShapeDtypeStruct((r, d), F32), jax.ShapeDtypeStruct((1, d), F32)],
            compiler_params=_cp("arbitrary"),
        )(do, gate.reshape(1, d), y)

    @jax.custom_vjp
    def f(x, gate, y):
        return fwd_call(x, gate, y)

    def fwd(x, gate, y):
        return f(x, gate, y), (gate, y)

    def bwd(res, do):
        gate, y = res
        dy, dg = bwd_call(do, gate, y)
        return do, dg.reshape(gate.shape), dy

    f.defvjp(fwd, bwd)
    return f


_NT = (((1,), (1,)), ((), ()))
_NN = (((1,), (0,)), ((), ()))
_TN = (((0,), (0,)), ((), ()))
ATT_TQ = 512
ATT_TK = 1024


def _attn_fwd_call(q, k, v, scale, name):
    h, sq, dq = q.shape
    hkv, t, dv = v.shape
    grp = h // hkv
    tq, tk = _tile(sq, ATT_TQ), _tile(t, ATT_TK)
    nk = t // tk

    def body(q_ref, k_ref, v_ref, o_ref, lse_ref, m_s, l_s, acc_s):
        j = pl.program_id(2)

        @pl.when(j == 0)
        def _():
            m_s[...] = jnp.full_like(m_s, -jnp.inf)
            l_s[...] = jnp.zeros_like(l_s)
            acc_s[...] = jnp.zeros_like(acc_s)

        s = lax.dot_general(q_ref[0], k_ref[0], _NT, preferred_element_type=F32) * scale
        m_prev = m_s[...]
        m_new = jnp.maximum(m_prev, jnp.max(s, axis=-1, keepdims=True))
        alpha = jnp.exp(m_prev - m_new)
        p = jnp.exp(s - m_new)
        l_s[...] = alpha * l_s[...] + jnp.sum(p, axis=-1, keepdims=True)
        acc_s[...] = alpha * acc_s[...] + lax.dot_general(p.astype(BF16), v_ref[0], _NN, preferred_element_type=F32)
        m_s[...] = m_new

        @pl.when(j == nk - 1)
        def _():
            o_ref[0] = acc_s[...] / l_s[...]
            lse_ref[0] = m_s[...] + jnp.log(l_s[...])

    return _pc(
        body, name=name, grid=(h, sq // tq, nk),
        in_specs=[pl.BlockSpec((1, tq, dq), lambda hh, i, j: (hh, i, 0)),
                  pl.BlockSpec((1, tk, dq), lambda hh, i, j: (hh // grp, j, 0)),
                  pl.BlockSpec((1, tk, dv), lambda hh, i, j: (hh // grp, j, 0))],
        out_specs=[pl.BlockSpec((1, tq, dv), lambda hh, i, j: (hh, i, 0)),
                   pl.BlockSpec((1, tq, 1), lambda hh, i, j: (hh, i, 0))],
        out_shape=[jax.ShapeDtypeStruct((h, sq, dv), F32), jax.ShapeDtypeStruct((h, sq, 1), F32)],
        scratch_shapes=[pltpu.VMEM((tq, 1), F32), pltpu.VMEM((tq, 1), F32), pltpu.VMEM((tq, dv), F32)],
        compiler_params=_cp("parallel", "parallel", "arbitrary"),
    )(q, k, v)


def _attn_bwd_call(q, k, v, o, do, lse, scale, name):
    h, sq, dq = q.shape
    hkv, t, dv = v.shape
    grp = h // hkv
    tq, tk = _tile(sq, ATT_TQ), _tile(t, ATT_TK)
    nq = sq // tq

    def body(q_ref, k_ref, v_ref, o_ref, do_ref, lse_ref, dq_ref, dk_ref, dv_ref, dk_s, dv_s):
        j = pl.program_id(1)
        i = pl.program_id(2)

        @pl.when(i == 0)
        def _():
            dk_s[...] = jnp.zeros_like(dk_s)
            dv_s[...] = jnp.zeros_like(dv_s)

        qv, kv, vv, dov = q_ref[0], k_ref[0], v_ref[0], do_ref[0]
        s = lax.dot_general(qv, kv, _NT, preferred_element_type=F32) * scale
        p = jnp.exp(s - lse_ref[0])
        delta = jnp.sum(dov.astype(F32) * o_ref[0], axis=-1, keepdims=True)
        dv_s[...] += lax.dot_general(p.astype(BF16), dov, _TN, preferred_element_type=F32)
        dp = lax.dot_general(dov, vv, _NT, preferred_element_type=F32)
        ds = (p * (dp - delta) * scale).astype(BF16)
        dk_s[...] += lax.dot_general(ds, qv, _TN, preferred_element_type=F32)
        dq_blk = lax.dot_general(ds, kv, _NN, preferred_element_type=F32)
        rows = pl.ds(pl.multiple_of(i * tq, tq), tq)

        @pl.when(j == 0)
        def _():
            dq_ref[0, rows, :] = dq_blk

        @pl.when(j > 0)
        def _():
            dq_ref[0, rows, :] += dq_blk

        @pl.when(i == nq - 1)
        def _():
            dk_ref[0] = dk_s[...]
            dv_ref[0] = dv_s[...]

    return _pc(
        body, name=name, grid=(h, t // tk, nq),
        in_specs=[pl.BlockSpec((1, tq, dq), lambda hh, j, i: (hh, i, 0)),
                  pl.BlockSpec((1, tk, dq), lambda hh, j, i: (hh // grp, j, 0)),
                  pl.BlockSpec((1, tk, dv), lambda hh, j, i: (hh // grp, j, 0)),
                  pl.BlockSpec((1, tq, dv), lambda hh, j, i: (hh, i, 0)),
                  pl.BlockSpec((1, tq, dv), lambda hh, j, i: (hh, i, 0)),
                  pl.BlockSpec((1, tq, 1), lambda hh, j, i: (hh, i, 0))],
        out_specs=[pl.BlockSpec((1, sq, dq), lambda hh, j, i: (hh, 0, 0)),
                   pl.BlockSpec((1, tk, dq), lambda hh, j, i: (hh, j, 0)),
                   pl.BlockSpec((1, tk, dv), lambda hh, j, i: (hh, j, 0))],
        out_shape=[jax.ShapeDtypeStruct((h, sq, dq), F32), jax.ShapeDtypeStruct((h, t, dq), F32),
                   jax.ShapeDtypeStruct((h, t, dv), F32)],
        scratch_shapes=[pltpu.VMEM((tk, dq), F32), pltpu.VMEM((tk, dv), F32)],
        compiler_params=_cp("parallel", "arbitrary", "arbitrary"),
    )(q, k, v, o, do, lse)


def _attention(name, scale):
    def heads_first(z):
        return z.transpose(1, 0, 2).astype(BF16)

    @jax.custom_vjp
    def f(q, k, v):
        return fwd(q, k, v)[0]

    def fwd(q, k, v):
        qt, kt, vt = heads_first(q), heads_first(k), heads_first(v)
        o, lse = _attn_fwd_call(qt, kt, vt, scale, name + "_fwd")
        s, h, _ = q.shape
        return o.transpose(1, 0, 2).reshape(s, h * v.shape[-1]), (qt, kt, vt, o, lse)

    def bwd(res, do):
        qt, kt, vt, o, lse = res
        h, s, _ = qt.shape
        hkv, t, dv = vt.shape
        dot = heads_first(do.reshape(s, h, dv))
        dq, dk, dvv = _attn_bwd_call(qt, kt, vt, o, dot, lse, scale, name + "_bwd")
        if h != hkv:
            dk = dk.reshape(hkv, h // hkv, t, -1).sum(axis=1)
            dvv = dvv.reshape(hkv, h // hkv, t, -1).sum(axis=1)
        return dq.transpose(1, 0, 2), dk.transpose(1, 0, 2), dvv.transpose(1, 0, 2)

    f.defvjp(fwd, bwd)
    return f


CONV_TF = 128
CONV_ROWS = 256


def _conv_neighbours(ref, c, r0, cur, row, nchunks, s, rc):
    prev8 = ref[pl.ds(pl.multiple_of(jnp.maximum(r0 - SUBLANES, 0), SUBLANES), SUBLANES), :]
    next8 = ref[pl.ds(pl.multiple_of(jnp.minimum(r0 + rc, s - SUBLANES), SUBLANES), SUBLANES), :]
    prow = jnp.where(c > 0, prev8[SUBLANES - 1:SUBLANES, :], 0.0)
    nrow = jnp.where(c < nchunks - 1, next8[0:1, :], 0.0)
    before = jnp.where(row == 0, prow, pltpu.roll(cur, 1, 0))
    after = jnp.where(row == rc - 1, nrow, pltpu.roll(cur, rc - 1, 0))
    return before, after


def _conv_fwd_call(u, w, b, name):
    s, f2 = u.shape
    f = f2 // 2
    tf = CONV_TF
    nf = f // tf
    rc = min(CONV_ROWS, s)
    nchunks = s // rc

    def body(g_ref, v_ref, w_ref, b_ref, o_ref):
        w0, w1, w2, bv = w_ref[0:1, :], w_ref[1:2, :], w_ref[2:3, :], b_ref[...]
        row = lax.broadcasted_iota(jnp.int32, (rc, tf), 0)

        def chunk(c, carry):
            r0 = pl.multiple_of(c * rc, rc)
            cur = g_ref[pl.ds(r0, rc), :]
            before, after = _conv_neighbours(g_ref, c, r0, cur, row, nchunks, s, rc)
            gc = before * w0 + cur * w1 + after * w2 + bv
            o_ref[pl.ds(r0, rc), :] = gc * jax.nn.sigmoid(gc) * v_ref[pl.ds(r0, rc), :]
            return carry

        lax.fori_loop(0, nchunks, chunk, 0)

    return _pc(
        body, name=name, grid=(nf,),
        in_specs=[pl.BlockSpec((s, tf), lambda j: (0, j)), pl.BlockSpec((s, tf), lambda j: (0, j + nf)),
                  pl.BlockSpec((3, tf), lambda j: (0, j)), pl.BlockSpec((1, tf), lambda j: (0, j))],
        out_specs=pl.BlockSpec((s, tf), lambda j: (0, j)),
        out_shape=jax.ShapeDtypeStruct((s, f), F32),
        compiler_params=_cp("parallel"),
    )(u, u, w, b.reshape(1, f))


def _conv_bwd_call(u, w, b, da, name):
    s, f2 = u.shape
    f = f2 // 2
    tf = CONV_TF
    nf = f // tf
    rc = min(CONV_ROWS, s)
    nchunks = s // rc

    def body(g_ref, v_ref, da_ref, w_ref, b_ref, dg_ref, dv_ref, dw_ref, db_ref, dgc_s):
        w0, w1, w2, bv = w_ref[0:1, :], w_ref[1:2, :], w_ref[2:3, :], b_ref[...]
        row = lax.broadcasted_iota(jnp.int32, (rc, tf), 0)

        def chunk1(c, carry):
            a0, a1, a2, ab = carry
            r0 = pl.multiple_of(c * rc, rc)
            cur = g_ref[pl.ds(r0, rc), :]
            before, after = _conv_neighbours(g_ref, c, r0, cur, row, nchunks, s, rc)
            gc = before * w0 + cur * w1 + after * w2 + bv
            sig = jax.nn.sigmoid(gc)
            dav = da_ref[pl.ds(r0, rc), :]
            dv_ref[pl.ds(r0, rc), :] = dav * (gc * sig)
            dgc = dav * v_ref[pl.ds(r0, rc), :] * (sig * (1.0 + gc * (1.0 - sig)))
            dgc_s[pl.ds(r0, rc), :] = dgc
            return (a0 + jnp.sum(dgc * before, axis=0, keepdims=True),
                    a1 + jnp.sum(dgc * cur, axis=0, keepdims=True),
                    a2 + jnp.sum(dgc * after, axis=0, keepdims=True),
                    ab + jnp.sum(dgc, axis=0, keepdims=True))

        z = jnp.zeros((1, tf), F32)
        a0, a1, a2, ab = lax.fori_loop(0, nchunks, chunk1, (z, z, z, z))
        dw_ref[0:1, :] = a0
        dw_ref[1:2, :] = a1
        dw_ref[2:3, :] = a2
        db_ref[...] = ab

        def chunk2(c, carry):
            r0 = pl.multiple_of(c * rc, rc)
            cur = dgc_s[pl.ds(r0, rc), :]
            before, after = _conv_neighbours(dgc_s, c, r0, cur, row, nchunks, s, rc)
            dg_ref[pl.ds(r0, rc), :] = after * w0 + cur * w1 + before * w2
            return carry

        lax.fori_loop(0, nchunks, chunk2, 0)

    col = pl.BlockSpec((s, tf), lambda j: (0, j))
    return _pc(
        body, name=name, grid=(nf,),
        in_specs=[col, pl.BlockSpec((s, tf), lambda j: (0, j + nf)), col,
                  pl.BlockSpec((3, tf), lambda j: (0, j)), pl.BlockSpec((1, tf), lambda j: (0, j))],
        out_specs=[col, col, pl.BlockSpec((3, tf), lambda j: (0, j)), pl.BlockSpec((1, tf), lambda j: (0, j))],
        out_shape=[jax.ShapeDtypeStruct((s, f), F32), jax.ShapeDtypeStruct((s, f), F32),
                   jax.ShapeDtypeStruct((3, f), F32), jax.ShapeDtypeStruct((1, f), F32)],
        scratch_shapes=[pltpu.VMEM((s, tf), F32)],
        compiler_params=_cp("parallel"),
    )(u, u, da, w, b.reshape(1, f))


def _convgate(name):
    @jax.custom_vjp
    def f(u, w, b):
        return _conv_fwd_call(u, w, b, name + "_fwd")

    def fwd(u, w, b):
        return f(u, w, b), (u, w, b)

    def bwd(res, da):
        u, w, b = res
        dg, dv, dw, db = _conv_bwd_call(u, w, b, da, name + "_bwd")
        return jnp.concatenate([dg, dv], axis=-1), dw, db.reshape(b.shape)

    f.defvjp(fwd, bwd)
    return f


def _loss_call(y, tgt):
    r, d = y.shape
    tr = _row_tile(r, d)

    def body(y_ref, t_ref, dy_ref, part_ref):
        @pl.when(pl.program_id(0) == 0)
        def _():
            part_ref[...] = jnp.zeros_like(part_ref)

        diff = y_ref[...] - t_ref[...]
        dy_ref[...] = diff / d
        part_ref[...] += jnp.sum(diff * diff, axis=0, keepdims=True)

    blk = pl.BlockSpec((tr, d), lambda i: (i, 0))
    vec = pl.BlockSpec((1, d), lambda i: (0, 0))
    dy, part = _pc(
        body, name="loss_head", grid=(r // tr,), in_specs=[blk, blk], out_specs=[blk, vec],
        out_shape=[jax.ShapeDtypeStruct((r, d), F32), jax.ShapeDtypeStruct((1, d), F32)],
        compiler_params=_cp("arbitrary"),
    )(y, tgt)
    return 0.5 * (jnp.sum(part) / d), dy


def _adamw_call(w, m, v, g, name, summed):
    r, c = w.shape
    tr = _tile(r, max(SUBLANES, (1 << 18) // c), SUBLANES)

    def body(g_ref, w_ref, m_ref, v_ref, go_ref, d_ref, mo_ref, vo_ref):
        if summed:
            gv = g_ref[0]
            for dev in range(1, N_DEV):
                gv = gv + g_ref[dev]
        else:
            gv = g_ref[...]
        mn = ADAM_B1 * m_ref[...] + (1.0 - ADAM_B1) * gv
        vn = ADAM_B2 * v_ref[...] + (1.0 - ADAM_B2) * (gv * gv)
        m_hat = mn / (1.0 - ADAM_B1 ** ADAM_STEP)
        v_hat = vn / (1.0 - ADAM_B2 ** ADAM_STEP)
        go_ref[...] = gv
        d_ref[...] = -ADAM_LR * (m_hat / (jnp.sqrt(v_hat) + ADAM_EPS) + ADAM_WD * w_ref[...])
        mo_ref[...] = mn
        vo_ref[...] = vn

    blk = pl.BlockSpec((tr, c), lambda i: (i, 0))
    gblk = pl.BlockSpec((N_DEV, tr, c), lambda i: (0, i, 0)) if summed else blk
    return _pc(
        body, name=name, grid=(r // tr,), in_specs=[gblk, blk, blk, blk], out_specs=[blk] * 4,
        out_shape=[jax.ShapeDtypeStruct((r, c), F32)] * 4, compiler_params=_cp("parallel"),
    )(g, w, m, v)


_ANY = pl.BlockSpec(memory_space=pl.ANY)
MESH = pl.DeviceIdType.MESH


def _window(ref, axis, idx, size):
    if axis == 0:
        return ref.at[pl.ds(idx * size, size), :]
    return ref.at[:, pl.ds(idx * size, size)]


def _allgather(shards, axes, name):
    n_p = len(shards)
    layers = [s.shape[0] for s in shards]
    out_shape = []
    for s, ax in zip(shards, axes):
        _, k, n = s.shape
        full = (k * N_DEV, n) if ax == 0 else (k, n * N_DEV)
        out_shape += [jax.ShapeDtypeStruct(full, s.dtype)] * s.shape[0]
    n_out = len(out_shape)

    def body(*refs):
        x_refs = refs[:n_p]
        flat = refs[n_p:n_p + n_out]
        send_sems, recv_sems, local_sems = refs[n_p + n_out:]
        outs, at = [], 0
        for cnt in layers:
            outs.append(flat[at:at + cnt])
            at += cnt
        x, y, c = lax.axis_index("x"), lax.axis_index("y"), lax.axis_index("c")
        me, sibling = (x, y, c), (x, y, 1 - c)
        chips = [(1 - x, y), (x, 1 - y), (1 - x, 1 - y)]

        def lin(d):
            return 4 * d[0] + 2 * d[1] + d[2]

        def copies(p, sem, block, to, from_shard):
            size = shards[p].shape[1 + axes[p]]
            res = []
            for l in range(layers[p]):
                dst = _window(outs[p][l], axes[p], lin(block), size)
                res.append(pltpu.make_async_remote_copy(
                    src_ref=x_refs[p].at[l] if from_shard else dst, dst_ref=dst,
                    send_sem=send_sems.at[p, sem], recv_sem=recv_sems.at[p, sem], device_id=to, device_id_type=MESH))
            return res

        def drained(p, sem):
            return pltpu.make_async_remote_copy(
                src_ref=x_refs[p], dst_ref=x_refs[p], send_sem=send_sems.at[p, sem], recv_sem=recv_sems.at[p, sem],
                device_id=me, device_id_type=MESH)

        for p in range(n_p):
            size = shards[p].shape[1 + axes[p]]
            for l in range(layers[p]):
                pltpu.make_async_copy(x_refs[p].at[l], _window(outs[p][l], axes[p], lin(me), size), local_sems.at[p]).start()
            for cp in copies(p, 0, me, sibling, True):
                cp.start()
            for j, chip in enumerate(chips):
                for cp in copies(p, 1 + j, me, (*chip, c), True):
                    cp.start()
        for p in range(n_p):
            for j, chip in enumerate(chips):
                drained(p, 1 + j).wait_recv()
                for cp in copies(p, 4 + j, (*chip, c), sibling, False):
                    cp.start()
        for p in range(n_p):
            for sem in (0, 4, 5, 6):
                drained(p, sem).wait_recv()
            for sem in range(7):
                drained(p, sem).wait_send()
            pltpu.make_async_copy(x_refs[p], x_refs[p], local_sems.at[p]).wait()

    flat = _pc(
        body, name=name, in_specs=[_ANY] * n_p, out_specs=[_ANY] * n_out, out_shape=out_shape,
        scratch_shapes=[pltpu.SemaphoreType.DMA((n_p, 7)), pltpu.SemaphoreType.DMA((n_p, 7)),
                        pltpu.SemaphoreType.DMA((n_p,))],
    )(*shards)
    res, at = [], 0
    for cnt in layers:
        res.append(list(flat[at:at + cnt]))
        at += cnt
    return res


def _alltoall(grads, axes, name):
    n_p = len(grads)
    layers = [len(g) for g in grads]
    n_in = sum(layers)
    blocks = []
    for g, ax in zip(grads, axes):
        kk, nn = g[0].shape
        blocks.append((kk // N_DEV, nn) if ax == 0 else (kk, nn // N_DEV))
    out_shape = [jax.ShapeDtypeStruct((N_DEV, cnt) + blk, F32) for cnt, blk in zip(layers, blocks)]

    def body(*refs):
        flat = refs[:n_in]
        lands = refs[n_in:n_in + n_p]
        send_sems, recv_sems, local_sems = refs[n_in + n_p:]
        g_refs, at = [], 0
        for cnt in layers:
            g_refs.append(flat[at:at + cnt])
            at += cnt
        x, y, c = lax.axis_index("x"), lax.axis_index("y"), lax.axis_index("c")
        me = 4 * x + 2 * y + c
        for p in range(n_p):
            size = blocks[p][axes[p]]
            for l in range(layers[p]):
                pltpu.make_async_copy(_window(g_refs[p][l], axes[p], me, size), lands[p].at[me, l], local_sems.at[p]).start()
            for rel in range(1, N_DEV):
                px = 1 - x if rel & 4 else x
                py = 1 - y if rel & 2 else y
                pc = 1 - c if rel & 1 else c
                for l in range(layers[p]):
                    pltpu.make_async_remote_copy(
                        src_ref=_window(g_refs[p][l], axes[p], 4 * px + 2 * py + pc, size), dst_ref=lands[p].at[me, l],
                        send_sem=send_sems.at[p, rel - 1], recv_sem=recv_sems.at[p, rel - 1],
                        device_id=(px, py, pc), device_id_type=MESH).start()
        for p in range(n_p):
            slab = lands[p].at[0]
            for rel in range(1, N_DEV):
                pltpu.make_async_remote_copy(
                    src_ref=slab, dst_ref=slab, send_sem=send_sems.at[p, rel - 1], recv_sem=recv_sems.at[p, rel - 1],
                    device_id=(x, y, c), device_id_type=MESH).wait()
            pltpu.make_async_copy(slab, slab, local_sems.at[p]).wait()

    return _pc(
        body, name=name, in_specs=[_ANY] * n_in, out_specs=[_ANY] * n_p, out_shape=out_shape,
        scratch_shapes=[pltpu.SemaphoreType.DMA((n_p, 7)), pltpu.SemaphoreType.DMA((n_p, 7)),
                        pltpu.SemaphoreType.DMA((n_p,))],
    )(*[g for gl in grads for g in gl])


def _gather_rows(a, name):
    r, c = a.shape
    return _allgather([a[None]], [0], name)[0][0].reshape(N_DEV, r, c)


def _conditioning(me):
    def forward(s16, w, b):
        nl, _, cols = w.shape
        part = jnp.concatenate([_matmul(s16, w[i], name="mod_fwd") for i in range(nl)], axis=0)
        full = _gather_rows(part, "mod_gather").reshape(N_DEV, nl, MOD_ROWS, cols)
        return full.transpose(1, 2, 0, 3).reshape(nl, MOD_ROWS, N_DEV * cols) + b[:, None, :]

    @jax.custom_vjp
    def f(s16, w, b):
        return forward(s16, w, b)

    def fwd(s16, w, b):
        return forward(s16, w, b), (s16, w)

    def bwd(res, dm):
        s16, w = res
        nl, dmodel, cols = w.shape
        width = dm.shape[-1]
        mine = lax.dynamic_slice_in_dim(dm, me, 1, axis=1)
        both = jnp.concatenate([mine, dm[:, N_DEV:N_DEV + 1]], axis=1).reshape(nl * 2, width)
        allrows = _gather_rows(both, "dmod_gather").reshape(N_DEV, nl, 2, width)
        total = jnp.concatenate([allrows[:, :, 0].transpose(1, 0, 2), jnp.sum(allrows[:, :, 1], axis=0)[:, None, :],
                                 jnp.zeros((nl, MOD_ROWS - N_DEV - 1, width), F32)], axis=1)
        db = jnp.sum(total, axis=1)
        my_cols = lax.dynamic_slice_in_dim(total, me * cols, cols, axis=2)
        dw = jnp.stack([_matmul(s16, my_cols[i], ta=True, name="mod_dw") for i in range(nl)])
        ds_part = _matmul(my_cols[0], w[0], tb=True, name="mod_ds")
        for i in range(1, nl):
            ds_part = ds_part + _matmul(my_cols[i], w[i], tb=True, name="mod_ds")
        ds = jnp.sum(_gather_rows(ds_part, "dcond_gather"), axis=0)
        return ds, dw, db

    f.defvjp(fwd, bwd)
    return f


def _rope_tables(s, rot_dim):
    t = jnp.arange(s, dtype=jnp.int32)
    rows, cols = t // GRID_W, t % GRID_W
    axis_dim = rot_dim // 2
    inv = jnp.power(ROPE_BASE, -jnp.arange(0, axis_dim, 2, dtype=F32) / axis_dim)
    ang_r = rows.astype(F32)[:, None] * inv
    ang_c = cols.astype(F32)[:, None] * inv
    ang = jnp.concatenate([ang_r, ang_r, ang_c, ang_c], axis=-1)
    return jnp.cos(ang), jnp.sin(ang)


def _rotate_half(z):
    z1, z2 = jnp.split(z, 2, axis=-1)
    return jnp.concatenate([-z2, z1], axis=-1)


def _rope(z, tables):
    if tables is None:
        return z
    cos, sin = tables
    half = z.shape[-1] // 2
    rot = jnp.concatenate([_rotate_half(z[..., :half]), _rotate_half(z[..., half:])], axis=-1)
    return z * cos[:, None, :] + rot * sin[:, None, :]


def _head_norm(name, z, g):
    s, h, d = z.shape
    return _norm(name)(z.reshape(s * h, d), g).reshape(s, h, d)


def _mla_q(tag, hx, rope, w):
    s = hx.shape[0]
    cq = _norm(tag + "_cq_norm")(_mm(tag + "_dq")(hx, w["mla_w_dq"]), w["mla_g_dq"])
    q = _mm(tag + "_uq")(cq, w["mla_w_uq"]).reshape(s, -1, MLA_NOPE + MLA_ROPE)
    q_nope = _head_norm(tag + "_qn_norm", q[..., :MLA_NOPE], w["mla_g_q_nope"])
    q_pe = _rope(_head_norm(tag + "_qp_norm", q[..., MLA_NOPE:], w["mla_g_q_pe"]), rope)
    return jnp.concatenate([q_nope, q_pe], axis=-1)


def _mla_kv(tag, hx, rope, w):
    s = hx.shape[0]
    kv_a = _mm(tag + "_dkv")(hx, w["mla_w_dkv"])
    rank = kv_a.shape[-1] - MLA_ROPE
    c_kv = _norm(tag + "_ckv_norm")(kv_a[:, :rank], w["mla_g_dkv"])
    k_pe = _rope(_norm(tag + "_kp_norm")(kv_a[:, rank:], w["mla_g_k_pe"])[:, None, :], rope)
    kv = _mm(tag + "_ukv")(c_kv, w["mla_w_ukv"]).reshape(s, -1, MLA_NOPE + MLA_V)
    heads = kv.shape[1]
    k_nope = _head_norm(tag + "_kn_norm", kv[..., :MLA_NOPE], w["mla_g_k_nope"])
    k = jnp.concatenate([k_nope, jnp.broadcast_to(k_pe, (s, heads, MLA_ROPE))], axis=-1)
    return k, kv[..., MLA_NOPE:]


def _gqa_q(tag, hx, rope, w):
    s = hx.shape[0]
    q = _mm(tag + "_q")(hx, w["gqa_w_q"]).reshape(s, -1, GQA_HEAD_DIM)
    return _rope(_head_norm(tag + "_q_norm", q, w["gqa_g_q"]), rope)


def _gqa_kv(tag, hx, rope, w):
    s = hx.shape[0]
    kv = _mm(tag + "_kv")(hx, w["gqa_w_kv"]).reshape(s, 2, GQA_KV_HEADS, GQA_HEAD_DIM)
    k = _rope(_head_norm(tag + "_k_norm", kv[:, 0], w["gqa_g_k"]), rope)
    return k, kv[:, 1]


def _conv_ffn(tag, hx, w):
    u = _mm(tag + "_up")(hx, w["ffn_w_up"])
    a = _convgate(tag + "_conv")(u, w["ffn_conv_w"], w["ffn_conv_b"])
    return _mm(tag + "_down")(a, w["ffn_w_down"])


def _forward(leaves, ctx, silu_c_all, me, depth):
    x = leaves["x"]
    s, d = x.shape
    rope_mla = _rope_tables(s, MLA_ROPE)
    rope_gqa = _rope_tables(s, GQA_HEAD_DIM)
    silu_cc = jax.nn.silu(leaves["c_ctx"])
    s16 = jnp.concatenate([silu_c_all, silu_cc[None, :], jnp.zeros((MOD_ROWS - N_DEV - 1, d), F32)], axis=0)
    mods = _conditioning(me)(s16, leaves["w_mod"], leaves["b_mod"])
    for i in range(depth):
        last = i == depth - 1
        w = {k: v[i // 2] for k, v in leaves["mixer"][i % 2].items()}
        w.update({k: v[i] for k, v in leaves["ffn"].items()})
        mod = lax.dynamic_index_in_dim(mods[i], me, axis=0, keepdims=False)
        sh1, sc1, g1, sh2, sc2, g2 = jnp.split(mod, N_MOD)
        csh1, csc1, cg1, csh2, csc2, cg2 = jnp.split(mods[i, N_DEV], N_MOD)
        tag = f"l{i}"
        hx = _modulate(tag + "_mix_mod")(x, leaves["norm_mix"][i], sh1, sc1)
        hc = _modulate(tag + "c_mix_mod")(ctx, leaves["norm_mix"][i], csh1, csc1)
        if i % 2 == 0:
            q_fn, kv_fn, w_o = _mla_q, _mla_kv, w["mla_w_o"]
            rope, scale = rope_mla, 1.0 / float(MLA_NOPE + MLA_ROPE) ** 0.5
        else:
            q_fn, kv_fn, w_o = _gqa_q, _gqa_kv, w["gqa_w_o"]
            rope, scale = rope_gqa, 1.0 / float(GQA_HEAD_DIM) ** 0.5
        k_lat, v_lat = kv_fn(tag, hx, rope, w)
        k_ctx, v_ctx = kv_fn(tag + "c", hc, None, w)
        o = _attention(tag + "_attn", scale)(q_fn(tag, hx, rope, w), jnp.concatenate([k_lat, k_ctx], axis=0),
                                             jnp.concatenate([v_lat, v_ctx], axis=0))
        x = _gres(tag + "_mix_res")(x, g1, _mm(tag + "_o")(o, w_o))
        if not last:
            oc = _attention(tag + "c_attn", scale)(q_fn(tag + "c", hc, None, w), k_ctx, v_ctx)
            ctx = _gres(tag + "c_mix_res")(ctx, cg1, _mm(tag + "c_o")(oc, w_o))
        hx = _modulate(tag + "_ffn_mod")(x, leaves["norm_ffn"][i], sh2, sc2)
        x = _gres(tag + "_ffn_res")(x, g2, _conv_ffn(tag, hx, w))
        if not last:
            hc = _modulate(tag + "c_ffn_mod")(ctx, leaves["norm_ffn"][i], csh2, csc2)
            ctx = _gres(tag + "c_ffn_res")(ctx, cg2, _conv_ffn(tag + "c", hc, w))
    return x


_MLA_BIG = {"mla_w_dq": 0, "mla_w_uq": 1, "mla_w_dkv": 0, "mla_w_ukv": 1, "mla_w_o": 0}
_GQA_BIG = {"gqa_w_q": 0, "gqa_w_kv": 0, "gqa_w_o": 0}
_FFN_BIG = {"ffn_w_up": 1, "ffn_w_down": 0}
_BIG_GROUPS = [("mla", _MLA_BIG), ("gqa", _GQA_BIG), ("ffn_up", {"ffn_w_up": 1}), ("ffn_down", {"ffn_w_down": 0})]
_MLA_GAINS = ["mla_g_dq", "mla_g_q_nope", "mla_g_q_pe", "mla_g_dkv", "mla_g_k_pe", "mla_g_k_nope"]
_GQA_GAINS = ["gqa_g_q", "gqa_g_k"]
_SHARED = ["norm_mix", "norm_ffn"] + _MLA_GAINS + _GQA_GAINS + ["ffn_conv_b"]
_SUMMED = ["c_ctx", "b_mod"]

_NAMES = ["c_ctx", "w_mod", "b_mod", "norm_mix", "norm_ffn", "mla_w_dq", "mla_g_dq", "mla_w_uq", "mla_g_q_nope",
          "mla_g_q_pe", "mla_w_dkv", "mla_g_dkv", "mla_g_k_pe", "mla_w_ukv", "mla_g_k_nope", "mla_w_o", "gqa_w_q",
          "gqa_g_q", "gqa_w_kv", "gqa_g_k", "gqa_w_o", "ffn_w_up", "ffn_conv_w", "ffn_conv_b", "ffn_w_down"]


def _rows128(a):
    return a.reshape(-1, LANES)


def kernel(x, c, ctx, c_ctx, w_mod, b_mod, norm_mix, norm_ffn, mla_w_dq, mla_g_dq, mla_w_uq, mla_g_q_nope, mla_g_q_pe, mla_w_dkv, mla_g_dkv, mla_g_k_pe, mla_w_ukv, mla_g_k_nope, mla_w_o, gqa_w_q, gqa_g_q, gqa_w_kv, gqa_g_k, gqa_w_o, ffn_w_up, ffn_conv_w, ffn_conv_b, ffn_w_down, loss_target, m_c_ctx, m_w_mod, m_b_mod, m_norm_mix, m_norm_ffn, m_mla_w_dq, m_mla_g_dq, m_mla_w_uq, m_mla_g_q_nope, m_mla_g_q_pe, m_mla_w_dkv, m_mla_g_dkv, m_mla_g_k_pe, m_mla_w_ukv, m_mla_g_k_nope, m_mla_w_o, m_gqa_w_q, m_gqa_g_q, m_gqa_w_kv, m_gqa_g_k, m_gqa_w_o, m_ffn_w_up, m_ffn_conv_w, m_ffn_conv_b, m_ffn_w_down, v_c_ctx, v_w_mod, v_b_mod, v_norm_mix, v_norm_ffn, v_mla_w_dq, v_mla_g_dq, v_mla_w_uq, v_mla_g_q_nope, v_mla_g_q_pe, v_mla_w_dkv, v_mla_g_dkv, v_mla_g_k_pe, v_mla_w_ukv, v_mla_g_k_nope, v_mla_w_o, v_gqa_w_q, v_gqa_g_q, v_gqa_w_kv, v_gqa_g_k, v_gqa_w_o, v_ffn_w_up, v_ffn_conv_w, v_ffn_conv_b, v_ffn_w_down):
    args = locals()
    wts = {n: args[n] for n in _NAMES}
    mom = {n: args["m_" + n] for n in _NAMES}
    var = {n: args["v_" + n] for n in _NAMES}
    me = 4 * lax.axis_index("x") + 2 * lax.axis_index("y") + lax.axis_index("c")
    depth = w_mod.shape[0]
    d_model = x.shape[-1]
    d_ff = ffn_conv_b.shape[-1]

    taps = ffn_conv_w.reshape(-1)
    packed = jnp.concatenate([jax.nn.silu(c).reshape(-1), taps])
    packed = jnp.concatenate([packed, jnp.zeros((-packed.size % (SUBLANES * LANES),), F32)])
    got = _gather_rows(_rows128(packed), "cond_gather").reshape(N_DEV, -1)
    silu_c_all = got[:, :d_model]
    conv_w_full = got[:, d_model:d_model + taps.size].reshape(N_DEV, depth, 3, -1)
    conv_w_full = conv_w_full.transpose(1, 2, 0, 3).reshape(depth, 3, d_ff)

    full = {}
    for gname, group in _BIG_GROUPS:
        names = list(group)
        got_w = _allgather([wts[n] for n in names], [group[n] for n in names], "gather_" + gname)
        full.update(dict(zip(names, got_w)))

    leaves = {
        "x": x[0], "c_ctx": c_ctx, "w_mod": w_mod, "b_mod": b_mod, "norm_mix": norm_mix, "norm_ffn": norm_ffn,
        "mixer": [
            {**{n: full[n] for n in _MLA_BIG}, **{n: wts[n] for n in _MLA_GAINS}},
            {**{n: full[n] for n in _GQA_BIG}, **{n: wts[n] for n in _GQA_GAINS}},
        ],
        "ffn": {"ffn_w_up": full["ffn_w_up"], "ffn_w_down": full["ffn_w_down"], "ffn_conv_w": conv_w_full,
                "ffn_conv_b": ffn_conv_b},
    }
    y, pullback = jax.vjp(lambda lv: _forward(lv, ctx[0], silu_c_all, me, depth), leaves)
    loss_part, dy = _loss_call(y, loss_target[0])
    (gl,) = pullback(dy)
    loss = lax.psum(loss_part, ("x", "y", "c"))

    grads, deltas, new_m, new_v = {}, {}, {}, {}

    def put(n, outs, shape):
        grads[n], deltas[n], new_m[n], new_v[n] = (o.reshape(shape) for o in outs)

    gfull = {**{n: gl["mixer"][0][n] for n in _MLA_BIG}, **{n: gl["mixer"][1][n] for n in _GQA_BIG},
             "ffn_w_up": gl["ffn"]["ffn_w_up"], "ffn_w_down": gl["ffn"]["ffn_w_down"]}
    for gname, group in _BIG_GROUPS:
        names = list(group)
        lands = _alltoall([gfull[n] for n in names], [group[n] for n in names], "exchange_" + gname)
        for n, land in zip(names, lands):
            shape = wts[n].shape
            cols = shape[-1]
            outs = _adamw_call(wts[n].reshape(-1, cols), mom[n].reshape(-1, cols), var[n].reshape(-1, cols),
                               land.reshape(N_DEV, -1, cols), "adamw_" + n, True)
            put(n, outs, shape)

    cols = w_mod.shape[-1]
    outs = _adamw_call(w_mod.reshape(-1, cols), m_w_mod.reshape(-1, cols), v_w_mod.reshape(-1, cols),
                       gl["w_mod"].reshape(-1, cols), "adamw_w_mod", False)
    put("w_mod", outs, w_mod.shape)

    share = {"norm_mix": gl["norm_mix"], "norm_ffn": gl["norm_ffn"], "ffn_conv_b": gl["ffn"]["ffn_conv_b"]}
    share.update({n: gl["mixer"][0][n] for n in _MLA_GAINS})
    share.update({n: gl["mixer"][1][n] for n in _GQA_GAINS})
    whole = {n: jnp.where(me == 0, gl[n], 0.0) for n in _SUMMED}
    order = _SUMMED + _SHARED
    parts = [whole[n] if n in whole else share[n] for n in order]
    sizes = [p.size for p in parts]
    taps_g = gl["ffn"]["ffn_conv_w"]
    pack_g = jnp.concatenate([p.reshape(-1) for p in parts] + [taps_g.reshape(-1)])
    pad = (-pack_g.size // LANES) % SUBLANES * LANES
    pack_g = jnp.concatenate([pack_g, jnp.zeros((pad,), F32)])
    land = _gather_rows(_rows128(pack_g), "smallgrad_gather")

    def pack(src):
        flat = jnp.concatenate([src[n].reshape(-1) for n in order] + [jnp.zeros((taps_g.size + pad,), F32)])
        return _rows128(flat)

    outs = _adamw_call(pack(wts), pack(mom), pack(var), land, "adamw_small", True)
    at = 0
    for n, size in zip(order, sizes):
        put(n, [o.reshape(-1)[at:at + size] for o in outs], wts[n].shape)
        at += size
    taps_sum = outs[0].reshape(-1)[at:at + taps_g.size].reshape(taps_g.shape)
    my_taps = lax.dynamic_slice_in_dim(taps_sum, me * ffn_conv_w.shape[-1], ffn_conv_w.shape[-1], axis=2)
    outs = _adamw_call(_rows128(ffn_conv_w), _rows128(m_ffn_conv_w), _rows128(v_ffn_conv_w), _rows128(my_taps),
                       "adamw_conv_w", False)
    put("ffn_conv_w", outs, ffn_conv_w.shape)

    return (loss, gl["x"][None], *[grads[n] for n in _NAMES], *[deltas[n] for n in _NAMES],
            *[new_m[n] for n in _NAMES], *[new_v[n] for n in _NAMES])
```

```python
import functools

import jax
import jax.numpy as jnp
from jax import lax
from jax.experimental import pallas as pl
from jax.experimental.pallas import tpu as pltpu

F32 = jnp.float32
BF16 = jnp.bfloat16

EPS = 1e-6
ROPE_BASE = 10000.0
GRID_W = 64
N_MOD = 6
MLA_NOPE = 128
MLA_ROPE = 64
MLA_V = 128
GQA_HEAD_DIM = 128
GQA_KV_HEADS = 4
ADAM_LR = 0.001
ADAM_B1 = 0.9
ADAM_B2 = 0.999
ADAM_EPS = 1e-08
ADAM_WD = 0.01
ADAM_STEP = 10
N_DEV = 8
MOD_ROWS = 16

VMEM_LIMIT = 48 * 1024 * 1024
LANES = 128
SUBLANES = 8


def _pc(body, **kw):
    return pl.pallas_call(body, **kw)


def _cp(*sem):
    return pltpu.CompilerParams(dimension_semantics=sem if sem else None, vmem_limit_bytes=VMEM_LIMIT)


def _tile(n, cap, mult=LANES):
    if n <= cap:
        return n
    t = (cap // mult) * mult
    while t >= mult:
        if n % t == 0:
            return t
        t -= mult
    return n


def _matmul(a, b, *, ta=False, tb=False, name, out_dtype=F32):
    assert not (ta and tb)
    kd, m = a.shape if ta else a.shape[::-1]
    n, kb = b.shape if tb else b.shape[::-1]
    assert kd == kb, (a.shape, b.shape, ta, tb)
    tm, tn, tk = _tile(m, 1024), _tile(n, 1024), _tile(kd, 1024)
    nk = kd // tk
    if ta:
        dims = (((0,), (0,)), ((), ()))
        a_spec = pl.BlockSpec((tk, tm), lambda i, j, k: (k, i))
    else:
        dims = (((1,), (1 if tb else 0,)), ((), ()))
        a_spec = pl.BlockSpec((tm, tk), lambda i, j, k: (i, k))
    if tb:
        b_spec = pl.BlockSpec((tn, tk), lambda i, j, k: (j, k))
    else:
        b_spec = pl.BlockSpec((tk, tn), lambda i, j, k: (k, j))

    def body(a_ref, b_ref, o_ref, acc_ref):
        k = pl.program_id(2)

        @pl.when(k == 0)
        def _():
            acc_ref[...] = jnp.zeros_like(acc_ref)

        acc_ref[...] += lax.dot_general(a_ref[...].astype(BF16), b_ref[...].astype(BF16), dims,
                                        preferred_element_type=F32)

        @pl.when(k == nk - 1)
        def _():
            o_ref[...] = acc_ref[...].astype(o_ref.dtype)

    return _pc(
        body, name=name, grid=(m // tm, n // tn, nk),
        in_specs=[a_spec, b_spec],
        out_specs=pl.BlockSpec((tm, tn), lambda i, j, k: (i, j)),
        out_shape=jax.ShapeDtypeStruct((m, n), out_dtype),
        scratch_shapes=[pltpu.VMEM((tm, tn), F32)],
        compiler_params=_cp("parallel", "parallel", "arbitrary"),
    )(a, b)


def _mm(name):
    @jax.custom_vjp
    def f(a, w):
        return _matmul(a, w, name=name + "_fwd")

    def fwd(a, w):
        return f(a, w), (a, w)

    def bwd(res, dc):
        a, w = res
        return (_matmul(dc, w, tb=True, name=name + "_da"),
                _matmul(a, dc, ta=True, name=name + "_dw", out_dtype=w.dtype))

    f.defvjp(fwd, bwd)
    return f


def _row_tile(r, d):
    return _tile(r, max(SUBLANES, min(4096, (2 << 20) // (4 * d))), SUBLANES)


def _norm_fwd_call(x, g, sc, sh, name):
    r, d = x.shape
    tr = _row_tile(r, d)
    mod = sc is not None

    def body(*refs):
        if mod:
            x_ref, g_ref, sc_ref, sh_ref, y_ref = refs
        else:
            x_ref, g_ref, y_ref = refs
        xv = x_ref[...]
        y = xv * lax.rsqrt(jnp.mean(xv * xv, axis=-1, keepdims=True) + EPS) * g_ref[...]
        if mod:
            y = y * (1.0 + sc_ref[...]) + sh_ref[...]
        y_ref[...] = y

    vec = pl.BlockSpec((1, d), lambda i: (0, 0))
    args = [x, g.reshape(1, d)] + ([sc.reshape(1, d), sh.reshape(1, d)] if mod else [])
    return _pc(
        body, name=name, grid=(r // tr,),
        in_specs=[pl.BlockSpec((tr, d), lambda i: (i, 0))] + [vec] * (len(args) - 1),
        out_specs=pl.BlockSpec((tr, d), lambda i: (i, 0)),
        out_shape=jax.ShapeDtypeStruct((r, d), F32),
        compiler_params=_cp("parallel"),
    )(*args)


def _norm_bwd_call(x, g, sc, dy, name):
    r, d = x.shape
    tr = _row_tile(r, d)
    mod = sc is not None

    def body(*refs):
        if mod:
            x_ref, g_ref, sc_ref, dy_ref, dx_ref, dg_ref, dsc_ref, dsh_ref = refs
        else:
            x_ref, g_ref, dy_ref, dx_ref, dg_ref = refs

        @pl.when(pl.program_id(0) == 0)
        def _():
            dg_ref[...] = jnp.zeros_like(dg_ref)
            if mod:
                dsc_ref[...] = jnp.zeros_like(dsc_ref)
                dsh_ref[...] = jnp.zeros_like(dsh_ref)

        xv = x_ref[...]
        gv = g_ref[...]
        dyv = dy_ref[...]
        rs = lax.rsqrt(jnp.mean(xv * xv, axis=-1, keepdims=True) + EPS)
        xh = xv * rs
        if mod:
            dsc_ref[...] += jnp.sum(dyv * (xh * gv), axis=0, keepdims=True)
            dsh_ref[...] += jnp.sum(dyv, axis=0, keepdims=True)
            t = dyv * (1.0 + sc_ref[...])
        else:
            t = dyv
        dg_ref[...] += jnp.sum(t * xh, axis=0, keepdims=True)
        dxh = t * gv
        dx_ref[...] = rs * (dxh - xh * jnp.mean(dxh * xh, axis=-1, keepdims=True))

    vec = pl.BlockSpec((1, d), lambda i: (0, 0))
    blk = pl.BlockSpec((tr, d), lambda i: (i, 0))
    args = [x, g.reshape(1, d)] + ([sc.reshape(1, d)] if mod else []) + [dy]
    n_vec_out = 3 if mod else 1
    outs = _pc(
        body, name=name, grid=(r // tr,),
        in_specs=[blk] + [vec] * (len(args) - 2) + [blk],
        out_specs=[blk] + [vec] * n_vec_out,
        out_shape=[jax.ShapeDtypeStruct((r, d), F32)] + [jax.ShapeDtypeStruct((1, d), F32)] * n_vec_out,
        compiler_params=_cp("arbitrary"),
    )(*args)
    return outs


def _norm(name):
    @jax.custom_vjp
    def f(x, g):
        return _norm_fwd_call(x, g, None, None, name + "_fwd")

    def fwd(x, g):
        return f(x, g), (x, g)

    def bwd(res, dy):
        x, g = res
        dx, dg = _norm_bwd_call(x, g, None, dy, name + "_bwd")
        return dx, dg.reshape(g.shape)

    f.defvjp(fwd, bwd)
    return f


def _modulate(name):
    @jax.custom_vjp
    def f(x, g, sh, sc):
        return _norm_fwd_call(x, g, sc, sh, name + "_fwd")

    def fwd(x, g, sh, sc):
        return f(x, g, sh, sc), (x, g, sc)

    def bwd(res, dy):
        x, g, sc = res
        dx, dg, dsc, dsh = _norm_bwd_call(x, g, sc, dy, name + "_bwd")
        return dx, dg.reshape(g.shape), dsh.reshape(g.shape), dsc.reshape(g.shape)

    f.defvjp(fwd, bwd)
    return f


def _gres(name):
    def fwd_call(x, gate, y):
        r, d = x.shape
        tr = _row_tile(r, d)

        def body(x_ref, g_ref, y_ref, o_ref):
            o_ref[...] = x_ref[...] + g_ref[...] * y_ref[...]

        blk = pl.BlockSpec((tr, d), lambda i: (i, 0))
        return _pc(
            body, name=name + "_fwd", grid=(r // tr,),
            in_specs=[blk, pl.BlockSpec((1, d), lambda i: (0, 0)), blk], out_specs=blk,
            out_shape=jax.ShapeDtypeStruct((r, d), F32), compiler_params=_cp("parallel"),
        )(x, gate.reshape(1, d), y)

    def bwd_call(do, gate, y):
        r, d = do.shape
        tr = _row_tile(r, d)

        def body(do_ref, g_ref, y_ref, dy_ref, dg_ref):
            @pl.when(pl.program_id(0) == 0)
            def _():
                dg_ref[...] = jnp.zeros_like(dg_ref)

            dov = do_ref[...]
            dy_ref[...] = g_ref[...] * dov
            dg_ref[...] += jnp.sum(dov * y_ref[...], axis=0, keepdims=True)

        blk = pl.BlockSpec((tr, d), lambda i: (i, 0))
        vec = pl.BlockSpec((1, d), lambda i: (0, 0))
        return _pc(
            body, name=name + "_bwd", grid=(r // tr,),
            in_specs=[blk, vec, blk], out_specs=[blk, vec],
            out_shape=[jax.ShapeDtypeStruct((r, d), F32), jax.ShapeDtypeStruct((1, d), F32)],
            compiler_params=_cp("arbitrary"),
        )(do, gate.reshape(1, d), y)

    @jax.custom_vjp
    def f(x, gate, y):
        return fwd_call(x, gate, y)

    def fwd(x, gate, y):
        return f(x, gate, y), (gate, y)

    def bwd(res, do):
        gate, y = res
        dy, dg = bwd_call(do, gate, y)
        return do, dg.reshape(gate.shape), dy

    f.defvjp(fwd, bwd)
    return f


_NT = (((1,), (1,)), ((), ()))
_NN = (((1,), (0,)), ((), ()))
_TN = (((0,), (0,)), ((), ()))
ATT_TQ = 2048
ATT_BTQ = 1024
ATT_SUB = 256
ATT_TK = 2816
LOG2E = 1.4426950408889634


def _attn_fwd_call(q, k, v1, scale, name):
    h, sq, dq = q.shape
    hkv, t, dvx = v1.shape
    dv = dvx - LANES
    grp = h // hkv
    tq, tk = _tile(sq, ATT_TQ), _tile(t, ATT_TK)
    sub = min(ATT_SUB, tq)
    nk = t // tk
    c = scale * LOG2E

    def body(q_ref, k_ref, v_ref, o_ref, lse_ref, m_s, acc_s):
        j = pl.program_id(2)

        @pl.when(j == 0)
        def _():
            m_s[...] = jnp.full_like(m_s, -jnp.inf)
            acc_s[...] = jnp.zeros_like(acc_s)

        kv, vv = k_ref[0], v_ref[0]
        for r in range(tq // sub):
            rows = pl.ds(r * sub, sub)
            s = lax.dot_general(q_ref[0, rows, :], kv, _NT, preferred_element_type=F32) * c
            m_prev = m_s[rows, :]
            m_new = jnp.maximum(m_prev, jnp.max(s, axis=-1, keepdims=True))
            p = jnp.exp2(s - m_new)
            acc_s[rows, :] = jnp.exp2(m_prev - m_new) * acc_s[rows, :] + lax.dot_general(
                p.astype(BF16), vv, _NN, preferred_element_type=F32)
            m_s[rows, :] = m_new

        @pl.when(j == nk - 1)
        def _():
            acc = acc_s[...]
            l = acc[:, dv:dv + 1]
            o_ref[0] = acc[:, :dv] / l
            lse_ref[0] = m_s[...] + jnp.log(l) * LOG2E

    return _pc(
        body, name=name, grid=(h, sq // tq, nk),
        in_specs=[pl.BlockSpec((1, tq, dq), lambda hh, i, j: (hh, i, 0)),
                  pl.BlockSpec((1, tk, dq), lambda hh, i, j: (hh // grp, j, 0)),
                  pl.BlockSpec((1, tk, dvx), lambda hh, i, j: (hh // grp, j, 0))],
        out_specs=[pl.BlockSpec((1, tq, dv), lambda hh, i, j: (hh, i, 0)),
                   pl.BlockSpec((1, tq, 1), lambda hh, i, j: (hh, i, 0))],
        out_shape=[jax.ShapeDtypeStruct((h, sq, dv), F32), jax.ShapeDtypeStruct((h, sq, 1), F32)],
        scratch_shapes=[pltpu.VMEM((tq, 1), F32), pltpu.VMEM((tq, dvx), F32)],
        compiler_params=_cp("parallel", "parallel", "arbitrary"),
    )(q, k, v1)


def _attn_bwd_call(q, k, v, o, do, lse, scale, name):
    h, sq, dq = q.shape
    hkv, t, dv = v.shape
    grp = h // hkv
    tq, tk = _tile(sq, ATT_BTQ), _tile(t, ATT_TK)
    sub = min(ATT_SUB, tq)
    nq = sq // tq
    c = scale * LOG2E

    def body(q_ref, k_ref, v_ref, o_ref, do_ref, lse_ref, dq_ref, dk_ref, dv_ref, dk_s, dv_s):
        j = pl.program_id(1)
        i = pl.program_id(2)

        @pl.when(i == 0)
        def _():
            dk_s[...] = jnp.zeros_like(dk_s)
            dv_s[...] = jnp.zeros_like(dv_s)

        kv, vv = k_ref[0], v_ref[0]
        for r in range(tq // sub):
            rows = pl.ds(r * sub, sub)
            qv, dov = q_ref[0, rows, :], do_ref[0, rows, :]
            s = lax.dot_general(qv, kv, _NT, preferred_element_type=F32) * c
            p = jnp.exp2(s - lse_ref[0, rows, :])
            delta = jnp.sum(dov.astype(F32) * o_ref[0, rows, :], axis=-1, keepdims=True)
            dv_s[...] += lax.dot_general(p.astype(BF16), dov, _TN, preferred_element_type=F32)
            dp = lax.dot_general(dov, vv, _NT, preferred_element_type=F32)
            ds = (p * (dp - delta) * scale).astype(BF16)
            dk_s[...] += lax.dot_general(ds, qv, _TN, preferred_element_type=F32)
            dq_blk = lax.dot_general(ds, kv, _NN, preferred_element_type=F32)
            out_rows = pl.ds(pl.multiple_of(i * tq + r * sub, sub), sub)

            @pl.when(j == 0)
            def _():
                dq_ref[0, out_rows, :] = dq_blk

            @pl.when(j > 0)
            def _():
                dq_ref[0, out_rows, :] += dq_blk

        @pl.when(i == nq - 1)
        def _():
            dk_ref[0] = dk_s[...]
            dv_ref[0] = dv_s[...]

    return _pc(
        body, name=name, grid=(h, t // tk, nq),
        in_specs=[pl.BlockSpec((1, tq, dq), lambda hh, j, i: (hh, i, 0)),
                  pl.BlockSpec((1, tk, dq), lambda hh, j, i: (hh // grp, j, 0)),
                  pl.BlockSpec((1, tk, dv), lambda hh, j, i: (hh // grp, j, 0)),
                  pl.BlockSpec((1, tq, dv), lambda hh, j, i: (hh, i, 0)),
                  pl.BlockSpec((1, tq, dv), lambda hh, j, i: (hh, i, 0)),
                  pl.BlockSpec((1, tq, 1), lambda hh, j, i: (hh, i, 0))],
        out_specs=[pl.BlockSpec((1, sq, dq), lambda hh, j, i: (hh, 0, 0)),
                   pl.BlockSpec((1, tk, dq), lambda hh, j, i: (hh, j, 0)),
                   pl.BlockSpec((1, tk, dv), lambda hh, j, i: (hh, j, 0))],
        out_shape=[jax.ShapeDtypeStruct((h, sq, dq), F32), jax.ShapeDtypeStruct((h, t, dq), F32),
                   jax.ShapeDtypeStruct((h, t, dv), F32)],
        scratch_shapes=[pltpu.VMEM((tk, dq), F32), pltpu.VMEM((tk, dv), F32)],
        compiler_params=_cp("parallel", "arbitrary", "arbitrary"),
    )(q, k, v, o, do, lse)


def _attention(name, scale):
    def heads_first(z):
        return z.transpose(1, 0, 2).astype(BF16)

    @jax.custom_vjp
    def f(q, k, v):
        return fwd(q, k, v)[0]

    def fwd(q, k, v):
        qt, kt, vt = heads_first(q), heads_first(k), heads_first(v)
        ones = jnp.zeros(vt.shape[:2] + (LANES,), BF16).at[:, :, 0].set(1.0)
        o, lse = _attn_fwd_call(qt, kt, jnp.concatenate([vt, ones], axis=-1), scale, name + "_fwd")
        s, h, _ = q.shape
        return o.transpose(1, 0, 2).reshape(s, h * v.shape[-1]), (qt, kt, vt, o, lse)

    def bwd(res, do):
        qt, kt, vt, o, lse = res
        h, s, _ = qt.shape
        hkv, t, dv = vt.shape
        dot = heads_first(do.reshape(s, h, dv))
        dq, dk, dvv = _attn_bwd_call(qt, kt, vt, o, dot, lse, scale, name + "_bwd")
        if h != hkv:
            dk = dk.reshape(hkv, h // hkv, t, -1).sum(axis=1)
            dvv = dvv.reshape(hkv, h // hkv, t, -1).sum(axis=1)
        return dq.transpose(1, 0, 2), dk.transpose(1, 0, 2), dvv.transpose(1, 0, 2)

    f.defvjp(fwd, bwd)
    return f


CONV_TF = 128
CONV_ROWS = 256


def _conv_neighbours(ref, c, r0, cur, row, nchunks, s, rc):
    prev8 = ref[pl.ds(pl.multiple_of(jnp.maximum(r0 - SUBLANES, 0), SUBLANES), SUBLANES), :]
    next8 = ref[pl.ds(pl.multiple_of(jnp.minimum(r0 + rc, s - SUBLANES), SUBLANES), SUBLANES), :]
    prow = jnp.where(c > 0, prev8[SUBLANES - 1:SUBLANES, :], 0.0)
    nrow = jnp.where(c < nchunks - 1, next8[0:1, :], 0.0)
    before = jnp.where(row == 0, prow, pltpu.roll(cur, 1, 0))
    after = jnp.where(row == rc - 1, nrow, pltpu.roll(cur, rc - 1, 0))
    return before, after


def _conv_fwd_call(u, w, b, name):
    s, f2 = u.shape
    f = f2 // 2
    tf = CONV_TF
    nf = f // tf
    rc = min(CONV_ROWS, s)
    nchunks = s // rc

    def body(g_ref, v_ref, w_ref, b_ref, o_ref):
        w0, w1, w2, bv = w_ref[0:1, :], w_ref[1:2, :], w_ref[2:3, :], b_ref[...]
        row = lax.broadcasted_iota(jnp.int32, (rc, tf), 0)

        def chunk(c, carry):
            r0 = pl.multiple_of(c * rc, rc)
            cur = g_ref[pl.ds(r0, rc), :]
            before, after = _conv_neighbours(g_ref, c, r0, cur, row, nchunks, s, rc)
            gc = before * w0 + cur * w1 + after * w2 + bv
            o_ref[pl.ds(r0, rc), :] = gc * jax.nn.sigmoid(gc) * v_ref[pl.ds(r0, rc), :]
            return carry

        lax.fori_loop(0, nchunks, chunk, 0)

    return _pc(
        body, name=name, grid=(nf,),
        in_specs=[pl.BlockSpec((s, tf), lambda j: (0, j)), pl.BlockSpec((s, tf), lambda j: (0, j + nf)),
                  pl.BlockSpec((3, tf), lambda j: (0, j)), pl.BlockSpec((1, tf), lambda j: (0, j))],
        out_specs=pl.BlockSpec((s, tf), lambda j: (0, j)),
        out_shape=jax.ShapeDtypeStruct((s, f), F32),
        compiler_params=_cp("parallel"),
    )(u, u, w, b.reshape(1, f))


def _conv_bwd_call(u, w, b, da, name):
    s, f2 = u.shape
    f = f2 // 2
    tf = CONV_TF
    nf = f // tf
    rc = min(CONV_ROWS, s)
    nchunks = s // rc

    def body(g_ref, v_ref, da_ref, w_ref, b_ref, dg_ref, dv_ref, dw_ref, db_ref, dgc_s):
        w0, w1, w2, bv = w_ref[0:1, :], w_ref[1:2, :], w_ref[2:3, :], b_ref[...]
        row = lax.broadcasted_iota(jnp.int32, (rc, tf), 0)

        def chunk1(c, carry):
            a0, a1, a2, ab = carry
            r0 = pl.multiple_of(c * rc, rc)
            cur = g_ref[pl.ds(r0, rc), :]
            before, after = _conv_neighbours(g_ref, c, r0, cur, row, nchunks, s, rc)
            gc = before * w0 + cur * w1 + after * w2 + bv
            sig = jax.nn.sigmoid(gc)
            dav = da_ref[pl.ds(r0, rc), :]
            dv_ref[pl.ds(r0, rc), :] = dav * (gc * sig)
            dgc = dav * v_ref[pl.ds(r0, rc), :] * (sig * (1.0 + gc * (1.0 - sig)))
            dgc_s[pl.ds(r0, rc), :] = dgc
            return (a0 + jnp.sum(dgc * before, axis=0, keepdims=True),
                    a1 + jnp.sum(dgc * cur, axis=0, keepdims=True),
                    a2 + jnp.sum(dgc * after, axis=0, keepdims=True),
                    ab + jnp.sum(dgc, axis=0, keepdims=True))

        z = jnp.zeros((1, tf), F32)
        a0, a1, a2, ab = lax.fori_loop(0, nchunks, chunk1, (z, z, z, z))
        dw_ref[0:1, :] = a0
        dw_ref[1:2, :] = a1
        dw_ref[2:3, :] = a2
        db_ref[...] = ab

        def chunk2(c, carry):
            r0 = pl.multiple_of(c * rc, rc)
            cur = dgc_s[pl.ds(r0, rc), :]
            before, after = _conv_neighbours(dgc_s, c, r0, cur, row, nchunks, s, rc)
            dg_ref[pl.ds(r0, rc), :] = after * w0 + cur * w1 + before * w2
            return carry

        lax.fori_loop(0, nchunks, chunk2, 0)

    col = pl.BlockSpec((s, tf), lambda j: (0, j))
    return _pc(
        body, name=name, grid=(nf,),
        in_specs=[col, pl.BlockSpec((s, tf), lambda j: (0, j + nf)), col,
                  pl.BlockSpec((3, tf), lambda j: (0, j)), pl.BlockSpec((1, tf), lambda j: (0, j))],
        out_specs=[col, col, pl.BlockSpec((3, tf), lambda j: (0, j)), pl.BlockSpec((1, tf), lambda j: (0, j))],
        out_shape=[jax.ShapeDtypeStruct((s, f), F32), jax.ShapeDtypeStruct((s, f), F32),
                   jax.ShapeDtypeStruct((3, f), F32), jax.ShapeDtypeStruct((1, f), F32)],
        scratch_shapes=[pltpu.VMEM((s, tf), F32)],
        compiler_params=_cp("parallel"),
    )(u, u, da, w, b.reshape(1, f))


def _convgate(name):
    @jax.custom_vjp
    def f(u, w, b):
        return _conv_fwd_call(u, w, b, name + "_fwd")

    def fwd(u, w, b):
        return f(u, w, b), (u, w, b)

    def bwd(res, da):
        u, w, b = res
        dg, dv, dw, db = _conv_bwd_call(u, w, b, da, name + "_bwd")
        return jnp.concatenate([dg, dv], axis=-1), dw, db.reshape(b.shape)

    f.defvjp(fwd, bwd)
    return f


def _loss_call(y, tgt):
    r, d = y.shape
    tr = _row_tile(r, d)

    def body(y_ref, t_ref, dy_ref, part_ref):
        @pl.when(pl.program_id(0) == 0)
        def _():
            part_ref[...] = jnp.zeros_like(part_ref)

        diff = y_ref[...] - t_ref[...]
        dy_ref[...] = diff / d
        part_ref[...] += jnp.sum(diff * diff, axis=0, keepdims=True)

    blk = pl.BlockSpec((tr, d), lambda i: (i, 0))
    vec = pl.BlockSpec((1, d), lambda i: (0, 0))
    dy, part = _pc(
        body, name="loss_head", grid=(r // tr,), in_specs=[blk, blk], out_specs=[blk, vec],
        out_shape=[jax.ShapeDtypeStruct((r, d), F32), jax.ShapeDtypeStruct((1, d), F32)],
        compiler_params=_cp("arbitrary"),
    )(y, tgt)
    return 0.5 * (jnp.sum(part) / d), dy


def _adamw_call(w, m, v, g, name, summed):
    r, c = w.shape
    tr = _tile(r, max(2 * SUBLANES, (1 << 18) // c), 2 * SUBLANES)

    def body(g_ref, w_ref, m_ref, v_ref, go_ref, d_ref, mo_ref, vo_ref):
        if summed:
            gv = g_ref[0].astype(F32)
            for dev in range(1, N_DEV):
                gv = gv + g_ref[dev].astype(F32)
        else:
            gv = g_ref[...]
        mn = ADAM_B1 * m_ref[...] + (1.0 - ADAM_B1) * gv
        vn = ADAM_B2 * v_ref[...] + (1.0 - ADAM_B2) * (gv * gv)
        m_hat = mn / (1.0 - ADAM_B1 ** ADAM_STEP)
        v_hat = vn / (1.0 - ADAM_B2 ** ADAM_STEP)
        go_ref[...] = gv
        d_ref[...] = -ADAM_LR * (m_hat / (jnp.sqrt(v_hat) + ADAM_EPS) + ADAM_WD * w_ref[...])
        mo_ref[...] = mn
        vo_ref[...] = vn

    blk = pl.BlockSpec((tr, c), lambda i: (i, 0))
    gblk = pl.BlockSpec((N_DEV, tr, c), lambda i: (0, i, 0)) if summed else blk
    return _pc(
        body, name=name, grid=(r // tr,), in_specs=[gblk, blk, blk, blk], out_specs=[blk] * 4,
        out_shape=[jax.ShapeDtypeStruct((r, c), F32)] * 4, compiler_params=_cp("parallel"),
    )(g, w, m, v)


_ANY = pl.BlockSpec(memory_space=pl.ANY)
MESH = pl.DeviceIdType.MESH


def _window(ref, axis, idx, size):
    if axis == 0:
        return ref.at[pl.ds(idx * size, size), :]
    return ref.at[:, pl.ds(idx * size, size)]


def _allgather(shards, axes, name):
    n_p = len(shards)
    layers = [s.shape[0] for s in shards]
    out_shape = []
    for s, ax in zip(shards, axes):
        _, k, n = s.shape
        full = (k * N_DEV, n) if ax == 0 else (k, n * N_DEV)
        out_shape += [jax.ShapeDtypeStruct(full, s.dtype)] * s.shape[0]
    n_out = len(out_shape)

    def body(*refs):
        x_refs = refs[:n_p]
        flat = refs[n_p:n_p + n_out]
        send_sems, recv_sems, local_sems = refs[n_p + n_out:]
        outs, at = [], 0
        for cnt in layers:
            outs.append(flat[at:at + cnt])
            at += cnt
        x, y, c = lax.axis_index("x"), lax.axis_index("y"), lax.axis_index("c")
        me, sibling = (x, y, c), (x, y, 1 - c)
        chips = [(1 - x, y), (x, 1 - y), (1 - x, 1 - y)]

        def lin(d):
            return 4 * d[0] + 2 * d[1] + d[2]

        def copies(p, sem, block, to, from_shard):
            size = shards[p].shape[1 + axes[p]]
            res = []
            for l in range(layers[p]):
                dst = _window(outs[p][l], axes[p], lin(block), size)
                res.append(pltpu.make_async_remote_copy(
                    src_ref=x_refs[p].at[l] if from_shard else dst, dst_ref=dst,
                    send_sem=send_sems.at[p, sem], recv_sem=recv_sems.at[p, sem], device_id=to, device_id_type=MESH))
            return res

        def drained(p, sem):
            return pltpu.make_async_remote_copy(
                src_ref=x_refs[p], dst_ref=x_refs[p], send_sem=send_sems.at[p, sem], recv_sem=recv_sems.at[p, sem],
                device_id=me, device_id_type=MESH)

        for p in range(n_p):
            size = shards[p].shape[1 + axes[p]]
            for l in range(layers[p]):
                pltpu.make_async_copy(x_refs[p].at[l], _window(outs[p][l], axes[p], lin(me), size), local_sems.at[p]).start()
            for cp in copies(p, 0, me, sibling, True):
                cp.start()
            for j, chip in enumerate(chips):
                for cp in copies(p, 1 + j, me, (*chip, c), True):
                    cp.start()
        for p in range(n_p):
            for j, chip in enumerate(chips):
                drained(p, 1 + j).wait_recv()
                for cp in copies(p, 4 + j, (*chip, c), sibling, False):
                    cp.start()
        for p in range(n_p):
            for sem in (0, 4, 5, 6):
                drained(p, sem).wait_recv()
            for sem in range(7):
                drained(p, sem).wait_send()
            pltpu.make_async_copy(x_refs[p], x_refs[p], local_sems.at[p]).wait()

    flat = _pc(
        body, name=name, in_specs=[_ANY] * n_p, out_specs=[_ANY] * n_out, out_shape=out_shape,
        scratch_shapes=[pltpu.SemaphoreType.DMA((n_p, 7)), pltpu.SemaphoreType.DMA((n_p, 7)),
                        pltpu.SemaphoreType.DMA((n_p,))],
    )(*shards)
    res, at = [], 0
    for cnt in layers:
        res.append(list(flat[at:at + cnt]))
        at += cnt
    return res


def _alltoall(grads, axes, name):
    n_p = len(grads)
    layers = [len(g) for g in grads]
    n_in = sum(layers)
    blocks = []
    for g, ax in zip(grads, axes):
        kk, nn = g[0].shape
        blocks.append((kk // N_DEV, nn) if ax == 0 else (kk, nn // N_DEV))
    out_shape = [jax.ShapeDtypeStruct((N_DEV, cnt) + blk, g[0].dtype) for g, cnt, blk in zip(grads, layers, blocks)]

    def body(*refs):
        flat = refs[:n_in]
        lands = refs[n_in:n_in + n_p]
        send_sems, recv_sems, local_sems = refs[n_in + n_p:]
        g_refs, at = [], 0
        for cnt in layers:
            g_refs.append(flat[at:at + cnt])
            at += cnt
        x, y, c = lax.axis_index("x"), lax.axis_index("y"), lax.axis_index("c")
        me = 4 * x + 2 * y + c
        for p in range(n_p):
            size = blocks[p][axes[p]]
            for l in range(layers[p]):
                pltpu.make_async_copy(_window(g_refs[p][l], axes[p], me, size), lands[p].at[me, l], local_sems.at[p]).start()
            for rel in range(1, N_DEV):
                px = 1 - x if rel & 4 else x
                py = 1 - y if rel & 2 else y
                pc = 1 - c if rel & 1 else c
                for l in range(layers[p]):
                    pltpu.make_async_remote_copy(
                        src_ref=_window(g_refs[p][l], axes[p], 4 * px + 2 * py + pc, size), dst_ref=lands[p].at[me, l],
                        send_sem=send_sems.at[p, rel - 1], recv_sem=recv_sems.at[p, rel - 1],
                        device_id=(px, py, pc), device_id_type=MESH).start()
        for p in range(n_p):
            slab = lands[p].at[0]
            for rel in range(1, N_DEV):
                pltpu.make_async_remote_copy(
                    src_ref=slab, dst_ref=slab, send_sem=send_sems.at[p, rel - 1], recv_sem=recv_sems.at[p, rel - 1],
                    device_id=(x, y, c), device_id_type=MESH).wait()
            pltpu.make_async_copy(slab, slab, local_sems.at[p]).wait()

    return _pc(
        body, name=name, in_specs=[_ANY] * n_in, out_specs=[_ANY] * n_p, out_shape=out_shape,
        scratch_shapes=[pltpu.SemaphoreType.DMA((n_p, 7)), pltpu.SemaphoreType.DMA((n_p, 7)),
                        pltpu.SemaphoreType.DMA((n_p,))],
    )(*[g for gl in grads for g in gl])


def _gather_rows(a, name):
    r, c = a.shape
    return _allgather([a[None]], [0], name)[0][0].reshape(N_DEV, r, c)


def _conditioning(me):
    def forward(s16, w, b):
        nl, _, cols = w.shape
        part = jnp.concatenate([_matmul(s16, w[i], name="mod_fwd") for i in range(nl)], axis=0)
        full = _gather_rows(part, "mod_gather").reshape(N_DEV, nl, MOD_ROWS, cols)
        return full.transpose(1, 2, 0, 3).reshape(nl, MOD_ROWS, N_DEV * cols) + b[:, None, :]

    @jax.custom_vjp
    def f(s16, w, b):
        return forward(s16, w, b)

    def fwd(s16, w, b):
        return forward(s16, w, b), (s16, w)

    def bwd(res, dm):
        s16, w = res
        nl, dmodel, cols = w.shape
        width = dm.shape[-1]
        mine = lax.dynamic_slice_in_dim(dm, me, 1, axis=1)
        both = jnp.concatenate([mine, dm[:, N_DEV:N_DEV + 1]], axis=1).reshape(nl * 2, width)
        allrows = _gather_rows(both, "dmod_gather").reshape(N_DEV, nl, 2, width)
        total = jnp.concatenate([allrows[:, :, 0].transpose(1, 0, 2), jnp.sum(allrows[:, :, 1], axis=0)[:, None, :],
                                 jnp.zeros((nl, MOD_ROWS - N_DEV - 1, width), F32)], axis=1)
        db = jnp.sum(total, axis=1)
        my_cols = lax.dynamic_slice_in_dim(total, me * cols, cols, axis=2)
        dw = jnp.stack([_matmul(s16, my_cols[i], ta=True, name="mod_dw") for i in range(nl)])
        ds_part = _matmul(my_cols[0], w[0], tb=True, name="mod_ds")
        for i in range(1, nl):
            ds_part = ds_part + _matmul(my_cols[i], w[i], tb=True, name="mod_ds")
        ds = jnp.sum(_gather_rows(ds_part, "dcond_gather"), axis=0)
        return ds, dw, db

    f.defvjp(fwd, bwd)
    return f


def _rope_tables(s, rot_dim):
    t = jnp.arange(s, dtype=jnp.int32)
    rows, cols = t // GRID_W, t % GRID_W
    axis_dim = rot_dim // 2
    inv = jnp.power(ROPE_BASE, -jnp.arange(0, axis_dim, 2, dtype=F32) / axis_dim)
    ang_r = rows.astype(F32)[:, None] * inv
    ang_c = cols.astype(F32)[:, None] * inv
    ang = jnp.concatenate([ang_r, ang_r, ang_c, ang_c], axis=-1)
    return jnp.cos(ang), jnp.sin(ang)


def _rotate_half(z):
    z1, z2 = jnp.split(z, 2, axis=-1)
    return jnp.concatenate([-z2, z1], axis=-1)


def _rope(z, tables):
    if tables is None:
        return z
    cos, sin = tables
    half = z.shape[-1] // 2
    rot = jnp.concatenate([_rotate_half(z[..., :half]), _rotate_half(z[..., half:])], axis=-1)
    return z * cos[:, None, :] + rot * sin[:, None, :]


def _head_norm(name, z, g):
    s, h, d = z.shape
    return _norm(name)(z.reshape(s * h, d), g).reshape(s, h, d)


def _mla_q(tag, hx, rope, w):
    s = hx.shape[0]
    cq = _norm(tag + "_cq_norm")(_mm(tag + "_dq")(hx, w["mla_w_dq"]), w["mla_g_dq"])
    q = _mm(tag + "_uq")(cq, w["mla_w_uq"]).reshape(s, -1, MLA_NOPE + MLA_ROPE)
    q_nope = _head_norm(tag + "_qn_norm", q[..., :MLA_NOPE], w["mla_g_q_nope"])
    q_pe = _rope(_head_norm(tag + "_qp_norm", q[..., MLA_NOPE:], w["mla_g_q_pe"]), rope)
    return jnp.concatenate([q_nope, q_pe], axis=-1)


def _mla_kv(tag, hx, rope, w):
    s = hx.shape[0]
    kv_a = _mm(tag + "_dkv")(hx, w["mla_w_dkv"])
    rank = kv_a.shape[-1] - MLA_ROPE
    c_kv = _norm(tag + "_ckv_norm")(kv_a[:, :rank], w["mla_g_dkv"])
    k_pe = _rope(_norm(tag + "_kp_norm")(kv_a[:, rank:], w["mla_g_k_pe"])[:, None, :], rope)
    kv = _mm(tag + "_ukv")(c_kv, w["mla_w_ukv"]).reshape(s, -1, MLA_NOPE + MLA_V)
    heads = kv.shape[1]
    k_nope = _head_norm(tag + "_kn_norm", kv[..., :MLA_NOPE], w["mla_g_k_nope"])
    k = jnp.concatenate([k_nope, jnp.broadcast_to(k_pe, (s, heads, MLA_ROPE))], axis=-1)
    return k, kv[..., MLA_NOPE:]


def _gqa_q(tag, hx, rope, w):
    s = hx.shape[0]
    q = _mm(tag + "_q")(hx, w["gqa_w_q"]).reshape(s, -1, GQA_HEAD_DIM)
    return _rope(_head_norm(tag + "_q_norm", q, w["gqa_g_q"]), rope)


def _gqa_kv(tag, hx, rope, w):
    s = hx.shape[0]
    kv = _mm(tag + "_kv")(hx, w["gqa_w_kv"]).reshape(s, 2, GQA_KV_HEADS, GQA_HEAD_DIM)
    k = _rope(_head_norm(tag + "_k_norm", kv[:, 0], w["gqa_g_k"]), rope)
    return k, kv[:, 1]


def _conv_ffn(tag, hx, w):
    u = _mm(tag + "_up")(hx, w["ffn_w_up"])
    a = _convgate(tag + "_conv")(u, w["ffn_conv_w"], w["ffn_conv_b"])
    return _mm(tag + "_down")(a, w["ffn_w_down"])


def _forward(leaves, ctx, silu_c_all, me, depth):
    x = leaves["x"]
    s, d = x.shape
    rope_mla = _rope_tables(s, MLA_ROPE)
    rope_gqa = _rope_tables(s, GQA_HEAD_DIM)
    silu_cc = jax.nn.silu(leaves["c_ctx"])
    s16 = jnp.concatenate([silu_c_all, silu_cc[None, :], jnp.zeros((MOD_ROWS - N_DEV - 1, d), F32)], axis=0)
    mods = _conditioning(me)(s16, leaves["w_mod"], leaves["b_mod"])
    for i in range(depth):
        last = i == depth - 1
        w = {k: v[i // 2] for k, v in leaves["mixer"][i % 2].items()}
        w.update({k: v[i] for k, v in leaves["ffn"].items()})
        mod = lax.dynamic_index_in_dim(mods[i], me, axis=0, keepdims=False)
        sh1, sc1, g1, sh2, sc2, g2 = jnp.split(mod, N_MOD)
        csh1, csc1, cg1, csh2, csc2, cg2 = jnp.split(mods[i, N_DEV], N_MOD)
        tag = f"l{i}"
        hx = _modulate(tag + "_mix_mod")(x, leaves["norm_mix"][i], sh1, sc1)
        hc = _modulate(tag + "c_mix_mod")(ctx, leaves["norm_mix"][i], csh1, csc1)
        if i % 2 == 0:
            q_fn, kv_fn, w_o = _mla_q, _mla_kv, w["mla_w_o"]
            rope, scale = rope_mla, 1.0 / float(MLA_NOPE + MLA_ROPE) ** 0.5
        else:
            q_fn, kv_fn, w_o = _gqa_q, _gqa_kv, w["gqa_w_o"]
            rope, scale = rope_gqa, 1.0 / float(GQA_HEAD_DIM) ** 0.5
        k_lat, v_lat = kv_fn(tag, hx, rope, w)
        k_ctx, v_ctx = kv_fn(tag + "c", hc, None, w)
        o = _attention(tag + "_attn", scale)(q_fn(tag, hx, rope, w), jnp.concatenate([k_lat, k_ctx], axis=0),
                                             jnp.concatenate([v_lat, v_ctx], axis=0))
        x = _gres(tag + "_mix_res")(x, g1, _mm(tag + "_o")(o, w_o))
        if not last:
            oc = _attention(tag + "c_attn", scale)(q_fn(tag + "c", hc, None, w), k_ctx, v_ctx)
            ctx = _gres(tag + "c_mix_res")(ctx, cg1, _mm(tag + "c_o")(oc, w_o))
        hx = _modulate(tag + "_ffn_mod")(x, leaves["norm_ffn"][i], sh2, sc2)
        x = _gres(tag + "_ffn_res")(x, g2, _conv_ffn(tag, hx, w))
        if not last:
            hc = _modulate(tag + "c_ffn_mod")(ctx, leaves["norm_ffn"][i], csh2, csc2)
            ctx = _gres(tag + "c_ffn_res")(ctx, cg2, _conv_ffn(tag + "c", hc, w))
    return x


_MLA_BIG = {"mla_w_dq": 0, "mla_w_uq": 1, "mla_w_dkv": 0, "mla_w_ukv": 1, "mla_w_o": 0}
_GQA_BIG = {"gqa_w_q": 0, "gqa_w_kv": 0, "gqa_w_o": 0}
_FFN_BIG = {"ffn_w_up": 1, "ffn_w_down": 0}
_BIG_GROUPS = [("mla", _MLA_BIG), ("gqa", _GQA_BIG), ("ffn_up", {"ffn_w_up": 1}), ("ffn_down", {"ffn_w_down": 0})]
_MLA_GAINS = ["mla_g_dq", "mla_g_q_nope", "mla_g_q_pe", "mla_g_dkv", "mla_g_k_pe", "mla_g_k_nope"]
_GQA_GAINS = ["gqa_g_q", "gqa_g_k"]
_SHARED = ["norm_mix", "norm_ffn"] + _MLA_GAINS + _GQA_GAINS + ["ffn_conv_b"]
_SUMMED = ["c_ctx", "b_mod"]

_NAMES = ["c_ctx", "w_mod", "b_mod", "norm_mix", "norm_ffn", "mla_w_dq", "mla_g_dq", "mla_w_uq", "mla_g_q_nope",
          "mla_g_q_pe", "mla_w_dkv", "mla_g_dkv", "mla_g_k_pe", "mla_w_ukv", "mla_g_k_nope", "mla_w_o", "gqa_w_q",
          "gqa_g_q", "gqa_w_kv", "gqa_g_k", "gqa_w_o", "ffn_w_up", "ffn_conv_w", "ffn_conv_b", "ffn_w_down"]


def _rows128(a):
    return a.reshape(-1, LANES)


def kernel(x, c, ctx, c_ctx, w_mod, b_mod, norm_mix, norm_ffn, mla_w_dq, mla_g_dq, mla_w_uq, mla_g_q_nope, mla_g_q_pe, mla_w_dkv, mla_g_dkv, mla_g_k_pe, mla_w_ukv, mla_g_k_nope, mla_w_o, gqa_w_q, gqa_g_q, gqa_w_kv, gqa_g_k, gqa_w_o, ffn_w_up, ffn_conv_w, ffn_conv_b, ffn_w_down, loss_target, m_c_ctx, m_w_mod, m_b_mod, m_norm_mix, m_norm_ffn, m_mla_w_dq, m_mla_g_dq, m_mla_w_uq, m_mla_g_q_nope, m_mla_g_q_pe, m_mla_w_dkv, m_mla_g_dkv, m_mla_g_k_pe, m_mla_w_ukv, m_mla_g_k_nope, m_mla_w_o, m_gqa_w_q, m_gqa_g_q, m_gqa_w_kv, m_gqa_g_k, m_gqa_w_o, m_ffn_w_up, m_ffn_conv_w, m_ffn_conv_b, m_ffn_w_down, v_c_ctx, v_w_mod, v_b_mod, v_norm_mix, v_norm_ffn, v_mla_w_dq, v_mla_g_dq, v_mla_w_uq, v_mla_g_q_nope, v_mla_g_q_pe, v_mla_w_dkv, v_mla_g_dkv, v_mla_g_k_pe, v_mla_w_ukv, v_mla_g_k_nope, v_mla_w_o, v_gqa_w_q, v_gqa_g_q, v_gqa_w_kv, v_gqa_g_k, v_gqa_w_o, v_ffn_w_up, v_ffn_conv_w, v_ffn_conv_b, v_ffn_w_down):
    args = locals()
    wts = {n: args[n] for n in _NAMES}
    mom = {n: args["m_" + n] for n in _NAMES}
    var = {n: args["v_" + n] for n in _NAMES}
    me = 4 * lax.axis_index("x") + 2 * lax.axis_index("y") + lax.axis_index("c")
    depth = w_mod.shape[0]
    d_model = x.shape[-1]
    d_ff = ffn_conv_b.shape[-1]

    taps = ffn_conv_w.reshape(-1)
    packed = jnp.concatenate([jax.nn.silu(c).reshape(-1), taps])
    packed = jnp.concatenate([packed, jnp.zeros((-packed.size % (SUBLANES * LANES),), F32)])
    got = _gather_rows(_rows128(packed), "cond_gather").reshape(N_DEV, -1)
    silu_c_all = got[:, :d_model]
    conv_w_full = got[:, d_model:d_model + taps.size].reshape(N_DEV, depth, 3, -1)
    conv_w_full = conv_w_full.transpose(1, 2, 0, 3).reshape(depth, 3, d_ff)

    full = {}
    for gname, group in _BIG_GROUPS:
        names = list(group)
        got_w = _allgather([wts[n].astype(BF16) for n in names], [group[n] for n in names], "gather_" + gname)
        full.update(dict(zip(names, got_w)))

    leaves = {
        "x": x[0], "c_ctx": c_ctx, "w_mod": w_mod, "b_mod": b_mod, "norm_mix": norm_mix, "norm_ffn": norm_ffn,
        "mixer": [
            {**{n: full[n] for n in _MLA_BIG}, **{n: wts[n] for n in _MLA_GAINS}},
            {**{n: full[n] for n in _GQA_BIG}, **{n: wts[n] for n in _GQA_GAINS}},
        ],
        "ffn": {"ffn_w_up": full["ffn_w_up"], "ffn_w_down": full["ffn_w_down"], "ffn_conv_w": conv_w_full,
                "ffn_conv_b": ffn_conv_b},
    }
    y, pullback = jax.vjp(lambda lv: _forward(lv, ctx[0], silu_c_all, me, depth), leaves)
    loss_part, dy = _loss_call(y, loss_target[0])
    (gl,) = pullback(dy)
    loss = lax.psum(loss_part, ("x", "y", "c"))

    grads, deltas, new_m, new_v = {}, {}, {}, {}

    def put(n, outs, shape):
        grads[n], deltas[n], new_m[n], new_v[n] = (o.reshape(shape) for o in outs)

    gfull = {**{n: gl["mixer"][0][n] for n in _MLA_BIG}, **{n: gl["mixer"][1][n] for n in _GQA_BIG},
             "ffn_w_up": gl["ffn"]["ffn_w_up"], "ffn_w_down": gl["ffn"]["ffn_w_down"]}
    for gname, group in _BIG_GROUPS:
        names = list(group)
        lands = _alltoall([gfull[n] for n in names], [group[n] for n in names], "exchange_" + gname)
        for n, land in zip(names, lands):
            shape = wts[n].shape
            cols = shape[-1]
            outs = _adamw_call(wts[n].reshape(-1, cols), mom[n].reshape(-1, cols), var[n].reshape(-1, cols),
                               land.reshape(N_DEV, -1, cols), "adamw_" + n, True)
            put(n, outs, shape)

    cols = w_mod.shape[-1]
    outs = _adamw_call(w_mod.reshape(-1, cols), m_w_mod.reshape(-1, cols), v_w_mod.reshape(-1, cols),
                       gl["w_mod"].reshape(-1, cols), "adamw_w_mod", False)
    put("w_mod", outs, w_mod.shape)

    share = {"norm_mix": gl["norm_mix"], "norm_ffn": gl["norm_ffn"], "ffn_conv_b": gl["ffn"]["ffn_conv_b"]}
    share.update({n: gl["mixer"][0][n] for n in _MLA_GAINS})
    share.update({n: gl["mixer"][1][n] for n in _GQA_GAINS})
    whole = {n: jnp.where(me == 0, gl[n], 0.0) for n in _SUMMED}
    order = _SUMMED + _SHARED
    parts = [whole[n] if n in whole else share[n] for n in order]
    sizes = [p.size for p in parts]
    taps_g = gl["ffn"]["ffn_conv_w"]
    pack_g = jnp.concatenate([p.reshape(-1) for p in parts] + [taps_g.reshape(-1)])
    pad = (-pack_g.size // LANES) % SUBLANES * LANES
    pack_g = jnp.concatenate([pack_g, jnp.zeros((pad,), F32)])
    land = _gather_rows(_rows128(pack_g), "smallgrad_gather")

    def pack(src):
        flat = jnp.concatenate([src[n].reshape(-1) for n in order] + [jnp.zeros((taps_g.size + pad,), F32)])
        return _rows128(flat)

    outs = _adamw_call(pack(wts), pack(mom), pack(var), land, "adamw_small", True)
    at = 0
    for n, size in zip(order, sizes):
        put(n, [o.reshape(-1)[at:at + size] for o in outs], wts[n].shape)
        at += size
    taps_sum = outs[0].reshape(-1)[at:at + taps_g.size].reshape(taps_g.shape)
    my_taps = lax.dynamic_slice_in_dim(taps_sum, me * ffn_conv_w.shape[-1], ffn_conv_w.shape[-1], axis=2)
    outs = _adamw_call(_rows128(ffn_conv_w), _rows128(m_ffn_conv_w), _rows128(v_ffn_conv_w), _rows128(my_taps),
                       "adamw_conv_w", False)
    put("ffn_conv_w", outs, ffn_conv_w.shape)

    return (loss, gl["x"][None], *[grads[n] for n in _NAMES], *[deltas[n] for n in _NAMES],
            *[new_m[n] for n in _NAMES], *[new_v[n] for n in _NAMES])
```

```python
import functools

import jax
import jax.numpy as jnp
from jax import lax
from jax.experimental import pallas as pl
from jax.experimental.pallas import tpu as pltpu

F32 = jnp.float32
BF16 = jnp.bfloat16

EPS = 1e-6
ROPE_BASE = 10000.0
GRID_W = 64
N_MOD = 6
MLA_NOPE = 128
MLA_ROPE = 64
MLA_V = 128
GQA_HEAD_DIM = 128
GQA_KV_HEADS = 4
ADAM_LR = 0.001
ADAM_B1 = 0.9
ADAM_B2 = 0.999
ADAM_EPS = 1e-08
ADAM_WD = 0.01
ADAM_STEP = 10
N_DEV = 8
MOD_ROWS = 16

VMEM_LIMIT = 48 * 1024 * 1024
MATMUL_VMEM = 40 * 1024 * 1024
LANES = 128
SUBLANES = 8


def _pc(body, **kw):
    return pl.pallas_call(body, **kw)


def _cp(*sem):
    return pltpu.CompilerParams(dimension_semantics=sem if sem else None, vmem_limit_bytes=VMEM_LIMIT)


def _tile(n, cap, mult=LANES):
    if n <= cap:
        return n
    t = (cap // mult) * mult
    while t >= mult:
        if n % t == 0:
            return t
        t -= mult
    return n


def _matmul(a, b, *, ta=False, tb=False, name, out_dtype=F32):
    assert not (ta and tb)
    kd, m = a.shape if ta else a.shape[::-1]
    n, kb = b.shape if tb else b.shape[::-1]
    assert kd == kb, (a.shape, b.shape, ta, tb)
    tm, tn = _tile(m, 1024), _tile(n, 1024)
    out_bytes = jnp.dtype(out_dtype).itemsize

    def vmem_need(t):
        return 2 * t * (tm * a.dtype.itemsize + tn * b.dtype.itemsize) + tm * tn * (2 * out_bytes + 4)

    tk = next(t for t in (_tile(kd, 2048), _tile(kd, 1024), _tile(kd, 512)) if vmem_need(t) <= MATMUL_VMEM or t <= 512)
    nk = kd // tk
    if ta:
        dims = (((0,), (0,)), ((), ()))
        a_spec = pl.BlockSpec((tk, tm), lambda i, j, k: (k, i))
    else:
        dims = (((1,), (1 if tb else 0,)), ((), ()))
        a_spec = pl.BlockSpec((tm, tk), lambda i, j, k: (i, k))
    if tb:
        b_spec = pl.BlockSpec((tn, tk), lambda i, j, k: (j, k))
    else:
        b_spec = pl.BlockSpec((tk, tn), lambda i, j, k: (k, j))

    def product(a_ref, b_ref):
        return lax.dot_general(a_ref[...].astype(BF16), b_ref[...].astype(BF16), dims, preferred_element_type=F32)

    def body_one(a_ref, b_ref, o_ref):
        o_ref[...] = product(a_ref, b_ref).astype(o_ref.dtype)

    def body(a_ref, b_ref, o_ref, acc_ref):
        k = pl.program_id(2)

        @pl.when(k == 0)
        def _():
            acc_ref[...] = jnp.zeros_like(acc_ref)

        acc_ref[...] += product(a_ref, b_ref)

        @pl.when(k == nk - 1)
        def _():
            o_ref[...] = acc_ref[...].astype(o_ref.dtype)

    return _pc(
        body_one if nk == 1 else body, name=name, grid=(m // tm, n // tn, nk),
        in_specs=[a_spec, b_spec],
        out_specs=pl.BlockSpec((tm, tn), lambda i, j, k: (i, j)),
        out_shape=jax.ShapeDtypeStruct((m, n), out_dtype),
        scratch_shapes=[] if nk == 1 else [pltpu.VMEM((tm, tn), F32)],
        compiler_params=_cp("parallel", "parallel", "arbitrary"),
    )(a, b)


def _mm(name, out_dtype=F32):
    @jax.custom_vjp
    def f(a, w):
        return _matmul(a, w, name=name + "_fwd", out_dtype=out_dtype)

    def fwd(a, w):
        return f(a, w), (a, w)

    def bwd(res, dc):
        a, w = res
        return (_matmul(dc, w, tb=True, name=name + "_da", out_dtype=a.dtype),
                _matmul(a, dc, ta=True, name=name + "_dw", out_dtype=w.dtype))

    f.defvjp(fwd, bwd)
    return f


def _row_tile(r, d):
    return _tile(r, max(SUBLANES, min(4096, (2 << 20) // (4 * d))), SUBLANES)


def _norm_fwd_call(x, g, sc, sh, name, out_dtype=F32):
    r, d = x.shape
    tr = _row_tile(r, d)
    mod = sc is not None

    def body(*refs):
        if mod:
            x_ref, g_ref, sc_ref, sh_ref, y_ref = refs
        else:
            x_ref, g_ref, y_ref = refs
        xv = x_ref[...]
        y = xv * lax.rsqrt(jnp.mean(xv * xv, axis=-1, keepdims=True) + EPS) * g_ref[...]
        if mod:
            y = y * (1.0 + sc_ref[...]) + sh_ref[...]
        y_ref[...] = y.astype(y_ref.dtype)

    vec = pl.BlockSpec((1, d), lambda i: (0, 0))
    args = [x, g.reshape(1, d)] + ([sc.reshape(1, d), sh.reshape(1, d)] if mod else [])
    return _pc(
        body, name=name, grid=(r // tr,),
        in_specs=[pl.BlockSpec((tr, d), lambda i: (i, 0))] + [vec] * (len(args) - 1),
        out_specs=pl.BlockSpec((tr, d), lambda i: (i, 0)),
        out_shape=jax.ShapeDtypeStruct((r, d), out_dtype),
        compiler_params=_cp("parallel"),
    )(*args)


def _norm_bwd_call(x, g, sc, dy, name):
    r, d = x.shape
    tr = _row_tile(r, d)
    mod = sc is not None

    def body(*refs):
        if mod:
            x_ref, g_ref, sc_ref, dy_ref, dx_ref, dg_ref, dsc_ref, dsh_ref = refs
        else:
            x_ref, g_ref, dy_ref, dx_ref, dg_ref = refs

        @pl.when(pl.program_id(0) == 0)
        def _():
            dg_ref[...] = jnp.zeros_like(dg_ref)
            if mod:
                dsc_ref[...] = jnp.zeros_like(dsc_ref)
                dsh_ref[...] = jnp.zeros_like(dsh_ref)

        xv = x_ref[...]
        gv = g_ref[...]
        dyv = dy_ref[...].astype(F32)
        rs = lax.rsqrt(jnp.mean(xv * xv, axis=-1, keepdims=True) + EPS)
        xh = xv * rs
        if mod:
            dsc_ref[...] += jnp.sum(dyv * (xh * gv), axis=0, keepdims=True)
            dsh_ref[...] += jnp.sum(dyv, axis=0, keepdims=True)
            t = dyv * (1.0 + sc_ref[...])
        else:
            t = dyv
        dg_ref[...] += jnp.sum(t * xh, axis=0, keepdims=True)
        dxh = t * gv
        dx_ref[...] = rs * (dxh - xh * jnp.mean(dxh * xh, axis=-1, keepdims=True))

    vec = pl.BlockSpec((1, d), lambda i: (0, 0))
    blk = pl.BlockSpec((tr, d), lambda i: (i, 0))
    args = [x, g.reshape(1, d)] + ([sc.reshape(1, d)] if mod else []) + [dy]
    n_vec_out = 3 if mod else 1
    outs = _pc(
        body, name=name, grid=(r // tr,),
        in_specs=[blk] + [vec] * (len(args) - 2) + [blk],
        out_specs=[blk] + [vec] * n_vec_out,
        out_shape=[jax.ShapeDtypeStruct((r, d), F32)] + [jax.ShapeDtypeStruct((1, d), F32)] * n_vec_out,
        compiler_params=_cp("arbitrary"),
    )(*args)
    return outs


def _norm(name):
    @jax.custom_vjp
    def f(x, g):
        return _norm_fwd_call(x, g, None, None, name + "_fwd")

    def fwd(x, g):
        return f(x, g), (x, g)

    def bwd(res, dy):
        x, g = res
        dx, dg = _norm_bwd_call(x, g, None, dy, name + "_bwd")
        return dx, dg.reshape(g.shape)

    f.defvjp(fwd, bwd)
    return f


def _modulate(name):
    @jax.custom_vjp
    def f(x, g, sh, sc):
        return _norm_fwd_call(x, g, sc, sh, name + "_fwd", out_dtype=BF16)

    def fwd(x, g, sh, sc):
        return f(x, g, sh, sc), (x, g, sc)

    def bwd(res, dy):
        x, g, sc = res
        dx, dg, dsc, dsh = _norm_bwd_call(x, g, sc, dy, name + "_bwd")
        return dx, dg.reshape(g.shape), dsh.reshape(g.shape), dsc.reshape(g.shape)

    f.defvjp(fwd, bwd)
    return f


def _gres(name):
    def fwd_call(x, gate, y):
        r, d = x.shape
        tr = _row_tile(r, d)

        def body(x_ref, g_ref, y_ref, o_ref):
            o_ref[...] = x_ref[...] + g_ref[...] * y_ref[...]

        blk = pl.BlockSpec((tr, d), lambda i: (i, 0))
        return _pc(
            body, name=name + "_fwd", grid=(r // tr,),
            in_specs=[blk, pl.BlockSpec((1, d), lambda i: (0, 0)), blk], out_specs=blk,
            out_shape=jax.ShapeDtypeStruct((r, d), F32), compiler_params=_cp("parallel"),
        )(x, gate.reshape(1, d), y)

    def bwd_call(do, gate, y):
        r, d = do.shape
        tr = _row_tile(r, d)

        def body(do_ref, g_ref, y_ref, dy_ref, dg_ref):
            @pl.when(pl.program_id(0) == 0)
            def _():
                dg_ref[...] = jnp.zeros_like(dg_ref)

            dov = do_ref[...]
            dy_ref[...] = g_ref[...] * dov
            dg_ref[...] += jnp.sum(dov * y_ref[...], axis=0, keepdims=True)

        blk = pl.BlockSpec((tr, d), lambda i: (i, 0))
        vec = pl.BlockSpec((1, d), lambda i: (0, 0))
        return _pc(
            body, name=name + "_bwd", grid=(r // tr,),
            in_specs=[blk, vec, blk], out_specs=[blk, vec],
            out_shape=[jax.ShapeDtypeStruct((r, d), F32), jax.ShapeDtypeStruct((1, d), F32)],
            compiler_params=_cp("arbitrary"),
        )(do, gate.reshape(1, d), y)

    @jax.custom_vjp
    def f(x, gate, y):
        return fwd_call(x, gate, y)

    def fwd(x, gate, y):
        return f(x, gate, y), (gate, y)

    def bwd(res, do):
        gate, y = res
        dy, dg = bwd_call(do, gate, y)
        return do, dg.reshape(gate.shape), dy

    f.defvjp(fwd, bwd)
    return f


_NT = (((1,), (1,)), ((), ()))
_NN = (((1,), (0,)), ((), ()))
_TN = (((0,), (0,)), ((), ()))
ATT_TQ = 2048
ATT_BTQ = 1024
ATT_SUB = 256
ATT_TK = 2816
LOG2E = 1.4426950408889634


def _attn_fwd_call(q, k, v1, scale, name):
    h, sq, dq = q.shape
    hkv, t, dvx = v1.shape
    dv = dvx - LANES
    grp = h // hkv
    tq, tk = _tile(sq, ATT_TQ), _tile(t, ATT_TK)
    sub = min(ATT_SUB, tq)
    nk = t // tk
    c = scale * LOG2E

    def body(q_ref, k_ref, v_ref, o_ref, lse_ref, m_s, acc_s):
        j = pl.program_id(2)

        @pl.when(j == 0)
        def _():
            m_s[...] = jnp.full_like(m_s, -jnp.inf)
            acc_s[...] = jnp.zeros_like(acc_s)

        kv, vv = k_ref[0], v_ref[0]
        for r in range(tq // sub):
            rows = pl.ds(r * sub, sub)
            s = lax.dot_general(q_ref[0, rows, :], kv, _NT, preferred_element_type=F32) * c
            m_prev = m_s[rows, :]
            m_new = jnp.maximum(m_prev, jnp.max(s, axis=-1, keepdims=True))
            p = jnp.exp2(s - m_new)
            acc_s[rows, :] = jnp.exp2(m_prev - m_new) * acc_s[rows, :] + lax.dot_general(
                p.astype(BF16), vv, _NN, preferred_element_type=F32)
            m_s[rows, :] = m_new

        @pl.when(j == nk - 1)
        def _():
            acc = acc_s[...]
            l = acc[:, dv:dv + 1]
            o_ref[0] = acc[:, :dv] / l
            lse_ref[0] = m_s[...] + jnp.log(l) * LOG2E

    return _pc(
        body, name=name, grid=(h, sq // tq, nk),
        in_specs=[pl.BlockSpec((1, tq, dq), lambda hh, i, j: (hh, i, 0)),
                  pl.BlockSpec((1, tk, dq), lambda hh, i, j: (hh // grp, j, 0)),
                  pl.BlockSpec((1, tk, dvx), lambda hh, i, j: (hh // grp, j, 0))],
        out_specs=[pl.BlockSpec((1, tq, dv), lambda hh, i, j: (hh, i, 0)),
                   pl.BlockSpec((1, tq, 1), lambda hh, i, j: (hh, i, 0))],
        out_shape=[jax.ShapeDtypeStruct((h, sq, dv), F32), jax.ShapeDtypeStruct((h, sq, 1), F32)],
        scratch_shapes=[pltpu.VMEM((tq, 1), F32), pltpu.VMEM((tq, dvx), F32)],
        compiler_params=_cp("parallel", "parallel", "arbitrary"),
    )(q, k, v1)


def _attn_bwd_call(q, k, v, o, do, lse, scale, name):
    h, sq, dq = q.shape
    hkv, t, dv = v.shape
    grp = h // hkv
    tq, tk = _tile(sq, ATT_BTQ), _tile(t, ATT_TK)
    sub = min(ATT_SUB, tq)
    nq = sq // tq
    c = scale * LOG2E

    def body(q_ref, k_ref, v_ref, o_ref, do_ref, lse_ref, dq_ref, dk_ref, dv_ref, dk_s, dv_s):
        j = pl.program_id(1)
        i = pl.program_id(2)

        @pl.when(i == 0)
        def _():
            dk_s[...] = jnp.zeros_like(dk_s)
            dv_s[...] = jnp.zeros_like(dv_s)

        kv, vv = k_ref[0], v_ref[0]
        for r in range(tq // sub):
            rows = pl.ds(r * sub, sub)
            qv, dov = q_ref[0, rows, :], do_ref[0, rows, :]
            s = lax.dot_general(qv, kv, _NT, preferred_element_type=F32) * c
            p = jnp.exp2(s - lse_ref[0, rows, :])
            delta = jnp.sum(dov.astype(F32) * o_ref[0, rows, :], axis=-1, keepdims=True)
            dv_s[...] += lax.dot_general(p.astype(BF16), dov, _TN, preferred_element_type=F32)
            dp = lax.dot_general(dov, vv, _NT, preferred_element_type=F32)
            ds = (p * (dp - delta) * scale).astype(BF16)
            dk_s[...] += lax.dot_general(ds, qv, _TN, preferred_element_type=F32)
            dq_blk = lax.dot_general(ds, kv, _NN, preferred_element_type=F32)
            out_rows = pl.ds(pl.multiple_of(i * tq + r * sub, sub), sub)

            @pl.when(j == 0)
            def _():
                dq_ref[0, out_rows, :] = dq_blk

            @pl.when(j > 0)
            def _():
                dq_ref[0, out_rows, :] += dq_blk

        @pl.when(i == nq - 1)
        def _():
            dk_ref[0] = dk_s[...]
            dv_ref[0] = dv_s[...]

    return _pc(
        body, name=name, grid=(h, t // tk, nq),
        in_specs=[pl.BlockSpec((1, tq, dq), lambda hh, j, i: (hh, i, 0)),
                  pl.BlockSpec((1, tk, dq), lambda hh, j, i: (hh // grp, j, 0)),
                  pl.BlockSpec((1, tk, dv), lambda hh, j, i: (hh // grp, j, 0)),
                  pl.BlockSpec((1, tq, dv), lambda hh, j, i: (hh, i, 0)),
                  pl.BlockSpec((1, tq, dv), lambda hh, j, i: (hh, i, 0)),
                  pl.BlockSpec((1, tq, 1), lambda hh, j, i: (hh, i, 0))],
        out_specs=[pl.BlockSpec((1, sq, dq), lambda hh, j, i: (hh, 0, 0)),
                   pl.BlockSpec((1, tk, dq), lambda hh, j, i: (hh, j, 0)),
                   pl.BlockSpec((1, tk, dv), lambda hh, j, i: (hh, j, 0))],
        out_shape=[jax.ShapeDtypeStruct((h, sq, dq), F32), jax.ShapeDtypeStruct((h, t, dq), F32),
                   jax.ShapeDtypeStruct((h, t, dv), F32)],
        scratch_shapes=[pltpu.VMEM((tk, dq), F32), pltpu.VMEM((tk, dv), F32)],
        compiler_params=_cp("parallel", "arbitrary", "arbitrary"),
    )(q, k, v, o, do, lse)


def _attention(name, scale):
    def heads_first(z):
        return z.transpose(1, 0, 2).astype(BF16)

    @jax.custom_vjp
    def f(q, k, v):
        return fwd(q, k, v)[0]

    def fwd(q, k, v):
        qt, kt, vt = heads_first(q), heads_first(k), heads_first(v)
        ones = jnp.zeros(vt.shape[:2] + (LANES,), BF16).at[:, :, 0].set(1.0)
        o, lse = _attn_fwd_call(qt, kt, jnp.concatenate([vt, ones], axis=-1), scale, name + "_fwd")
        s, h, _ = q.shape
        return o.transpose(1, 0, 2).reshape(s, h * v.shape[-1]).astype(BF16), (qt, kt, vt, o, lse)

    def bwd(res, do):
        qt, kt, vt, o, lse = res
        h, s, _ = qt.shape
        hkv, t, dv = vt.shape
        dot = heads_first(do.reshape(s, h, dv))
        dq, dk, dvv = _attn_bwd_call(qt, kt, vt, o, dot, lse, scale, name + "_bwd")
        if h != hkv:
            dk = dk.reshape(hkv, h // hkv, t, -1).sum(axis=1)
            dvv = dvv.reshape(hkv, h // hkv, t, -1).sum(axis=1)
        return dq.transpose(1, 0, 2), dk.transpose(1, 0, 2), dvv.transpose(1, 0, 2)

    f.defvjp(fwd, bwd)
    return f


CONV_TF = 128
CONV_ROWS = 256


def _conv_neighbours(ref, c, r0, cur, row, nchunks, s, rc):
    halo = SUBLANES * (4 // ref.dtype.itemsize)
    prev = ref[pl.ds(pl.multiple_of(jnp.maximum(r0 - halo, 0), halo), halo), :].astype(F32)
    nxt = ref[pl.ds(pl.multiple_of(jnp.minimum(r0 + rc, s - halo), halo), halo), :].astype(F32)
    prow = jnp.where(c > 0, prev[halo - 1:halo, :], 0.0)
    nrow = jnp.where(c < nchunks - 1, nxt[0:1, :], 0.0)
    before = jnp.where(row == 0, prow, pltpu.roll(cur, 1, 0))
    after = jnp.where(row == rc - 1, nrow, pltpu.roll(cur, rc - 1, 0))
    return before, after


def _conv_fwd_call(u, w, b, name):
    s, f2 = u.shape
    f = f2 // 2
    tf = CONV_TF
    nf = f // tf
    rc = min(CONV_ROWS, s)
    nchunks = s // rc

    def body(g_ref, v_ref, w_ref, b_ref, o_ref):
        w0, w1, w2, bv = w_ref[0:1, :], w_ref[1:2, :], w_ref[2:3, :], b_ref[...]
        row = lax.broadcasted_iota(jnp.int32, (rc, tf), 0)

        def chunk(c, carry):
            r0 = pl.multiple_of(c * rc, rc)
            cur = g_ref[pl.ds(r0, rc), :].astype(F32)
            before, after = _conv_neighbours(g_ref, c, r0, cur, row, nchunks, s, rc)
            gc = before * w0 + cur * w1 + after * w2 + bv
            val = v_ref[pl.ds(r0, rc), :].astype(F32)
            o_ref[pl.ds(r0, rc), :] = (gc * jax.nn.sigmoid(gc) * val).astype(o_ref.dtype)
            return carry

        lax.fori_loop(0, nchunks, chunk, 0)

    return _pc(
        body, name=name, grid=(nf,),
        in_specs=[pl.BlockSpec((s, tf), lambda j: (0, j)), pl.BlockSpec((s, tf), lambda j: (0, j + nf)),
                  pl.BlockSpec((3, tf), lambda j: (0, j)), pl.BlockSpec((1, tf), lambda j: (0, j))],
        out_specs=pl.BlockSpec((s, tf), lambda j: (0, j)),
        out_shape=jax.ShapeDtypeStruct((s, f), u.dtype),
        compiler_params=_cp("parallel"),
    )(u, u, w, b.reshape(1, f))


def _conv_bwd_call(u, w, b, da, name):
    s, f2 = u.shape
    f = f2 // 2
    tf = CONV_TF
    nf = f // tf
    rc = min(CONV_ROWS, s)
    nchunks = s // rc

    def body(g_ref, v_ref, da_ref, w_ref, b_ref, dg_ref, dv_ref, dw_ref, db_ref, dgc_s):
        w0, w1, w2, bv = w_ref[0:1, :], w_ref[1:2, :], w_ref[2:3, :], b_ref[...]
        row = lax.broadcasted_iota(jnp.int32, (rc, tf), 0)

        def chunk1(c, carry):
            a0, a1, a2, ab = carry
            r0 = pl.multiple_of(c * rc, rc)
            cur = g_ref[pl.ds(r0, rc), :].astype(F32)
            before, after = _conv_neighbours(g_ref, c, r0, cur, row, nchunks, s, rc)
            gc = before * w0 + cur * w1 + after * w2 + bv
            sig = jax.nn.sigmoid(gc)
            dav = da_ref[pl.ds(r0, rc), :].astype(F32)
            dv_ref[pl.ds(r0, rc), :] = (dav * (gc * sig)).astype(dv_ref.dtype)
            dgc = dav * v_ref[pl.ds(r0, rc), :].astype(F32) * (sig * (1.0 + gc * (1.0 - sig)))
            dgc_s[pl.ds(r0, rc), :] = dgc
            return (a0 + jnp.sum(dgc * before, axis=0, keepdims=True),
                    a1 + jnp.sum(dgc * cur, axis=0, keepdims=True),
                    a2 + jnp.sum(dgc * after, axis=0, keepdims=True),
                    ab + jnp.sum(dgc, axis=0, keepdims=True))

        z = jnp.zeros((1, tf), F32)
        a0, a1, a2, ab = lax.fori_loop(0, nchunks, chunk1, (z, z, z, z))
        dw_ref[0:1, :] = a0
        dw_ref[1:2, :] = a1
        dw_ref[2:3, :] = a2
        db_ref[...] = ab

        def chunk2(c, carry):
            r0 = pl.multiple_of(c * rc, rc)
            cur = dgc_s[pl.ds(r0, rc), :]
            before, after = _conv_neighbours(dgc_s, c, r0, cur, row, nchunks, s, rc)
            dg_ref[pl.ds(r0, rc), :] = (after * w0 + cur * w1 + before * w2).astype(dg_ref.dtype)
            return carry

        lax.fori_loop(0, nchunks, chunk2, 0)

    col = pl.BlockSpec((s, tf), lambda j: (0, j))
    return _pc(
        body, name=name, grid=(nf,),
        in_specs=[col, pl.BlockSpec((s, tf), lambda j: (0, j + nf)), col,
                  pl.BlockSpec((3, tf), lambda j: (0, j)), pl.BlockSpec((1, tf), lambda j: (0, j))],
        out_specs=[col, col, pl.BlockSpec((3, tf), lambda j: (0, j)), pl.BlockSpec((1, tf), lambda j: (0, j))],
        out_shape=[jax.ShapeDtypeStruct((s, f), u.dtype), jax.ShapeDtypeStruct((s, f), u.dtype),
                   jax.ShapeDtypeStruct((3, f), F32), jax.ShapeDtypeStruct((1, f), F32)],
        scratch_shapes=[pltpu.VMEM((s, tf), F32)],
        compiler_params=_cp("parallel"),
    )(u, u, da, w, b.reshape(1, f))


def _convgate(name):
    @jax.custom_vjp
    def f(u, w, b):
        return _conv_fwd_call(u, w, b, name + "_fwd")

    def fwd(u, w, b):
        return f(u, w, b), (u, w, b)

    def bwd(res, da):
        u, w, b = res
        dg, dv, dw, db = _conv_bwd_call(u, w, b, da, name + "_bwd")
        return jnp.concatenate([dg, dv], axis=-1), dw, db.reshape(b.shape)

    f.defvjp(fwd, bwd)
    return f


def _loss_call(y, tgt):
    r, d = y.shape
    tr = _row_tile(r, d)

    def body(y_ref, t_ref, dy_ref, part_ref):
        @pl.when(pl.program_id(0) == 0)
        def _():
            part_ref[...] = jnp.zeros_like(part_ref)

        diff = y_ref[...] - t_ref[...]
        dy_ref[...] = diff / d
        part_ref[...] += jnp.sum(diff * diff, axis=0, keepdims=True)

    blk = pl.BlockSpec((tr, d), lambda i: (i, 0))
    vec = pl.BlockSpec((1, d), lambda i: (0, 0))
    dy, part = _pc(
        body, name="loss_head", grid=(r // tr,), in_specs=[blk, blk], out_specs=[blk, vec],
        out_shape=[jax.ShapeDtypeStruct((r, d), F32), jax.ShapeDtypeStruct((1, d), F32)],
        compiler_params=_cp("arbitrary"),
    )(y, tgt)
    return 0.5 * (jnp.sum(part) / d), dy


def _adamw_call(w, m, v, g, name, summed):
    r, c = w.shape
    tr = _tile(r, max(2 * SUBLANES, (1 << 18) // c), 2 * SUBLANES)

    def body(g_ref, w_ref, m_ref, v_ref, go_ref, d_ref, mo_ref, vo_ref):
        if summed:
            gv = g_ref[0].astype(F32)
            for dev in range(1, N_DEV):
                gv = gv + g_ref[dev].astype(F32)
        else:
            gv = g_ref[...]
        mn = ADAM_B1 * m_ref[...] + (1.0 - ADAM_B1) * gv
        vn = ADAM_B2 * v_ref[...] + (1.0 - ADAM_B2) * (gv * gv)
        m_hat = mn / (1.0 - ADAM_B1 ** ADAM_STEP)
        v_hat = vn / (1.0 - ADAM_B2 ** ADAM_STEP)
        go_ref[...] = gv
        d_ref[...] = -ADAM_LR * (m_hat / (jnp.sqrt(v_hat) + ADAM_EPS) + ADAM_WD * w_ref[...])
        mo_ref[...] = mn
        vo_ref[...] = vn

    blk = pl.BlockSpec((tr, c), lambda i: (i, 0))
    gblk = pl.BlockSpec((N_DEV, tr, c), lambda i: (0, i, 0)) if summed else blk
    return _pc(
        body, name=name, grid=(r // tr,), in_specs=[gblk, blk, blk, blk], out_specs=[blk] * 4,
        out_shape=[jax.ShapeDtypeStruct((r, c), F32)] * 4, compiler_params=_cp("parallel"),
    )(g, w, m, v)


_ANY = pl.BlockSpec(memory_space=pl.ANY)
MESH = pl.DeviceIdType.MESH


def _window(ref, axis, idx, size):
    if axis == 0:
        return ref.at[pl.ds(idx * size, size), :]
    return ref.at[:, pl.ds(idx * size, size)]


def _allgather(shards, axes, name):
    n_p = len(shards)
    layers = [s.shape[0] for s in shards]
    out_shape = []
    for s, ax in zip(shards, axes):
        _, k, n = s.shape
        full = (k * N_DEV, n) if ax == 0 else (k, n * N_DEV)
        out_shape += [jax.ShapeDtypeStruct(full, s.dtype)] * s.shape[0]
    n_out = len(out_shape)

    def body(*refs):
        x_refs = refs[:n_p]
        flat = refs[n_p:n_p + n_out]
        send_sems, recv_sems, local_sems = refs[n_p + n_out:]
        outs, at = [], 0
        for cnt in layers:
            outs.append(flat[at:at + cnt])
            at += cnt
        x, y, c = lax.axis_index("x"), lax.axis_index("y"), lax.axis_index("c")
        me, sibling = (x, y, c), (x, y, 1 - c)
        chips = [(1 - x, y), (x, 1 - y), (1 - x, 1 - y)]

        def lin(d):
            return 4 * d[0] + 2 * d[1] + d[2]

        def copies(p, sem, block, to, from_shard):
            size = shards[p].shape[1 + axes[p]]
            res = []
            for l in range(layers[p]):
                dst = _window(outs[p][l], axes[p], lin(block), size)
                res.append(pltpu.make_async_remote_copy(
                    src_ref=x_refs[p].at[l] if from_shard else dst, dst_ref=dst,
                    send_sem=send_sems.at[p, sem], recv_sem=recv_sems.at[p, sem], device_id=to, device_id_type=MESH))
            return res

        def drained(p, sem):
            return pltpu.make_async_remote_copy(
                src_ref=x_refs[p], dst_ref=x_refs[p], send_sem=send_sems.at[p, sem], recv_sem=recv_sems.at[p, sem],
                device_id=me, device_id_type=MESH)

        for p in range(n_p):
            size = shards[p].shape[1 + axes[p]]
            for l in range(layers[p]):
                pltpu.make_async_copy(x_refs[p].at[l], _window(outs[p][l], axes[p], lin(me), size), local_sems.at[p]).start()
            for cp in copies(p, 0, me, sibling, True):
                cp.start()
            for j, chip in enumerate(chips):
                for cp in copies(p, 1 + j, me, (*chip, c), True):
                    cp.start()
        for p in range(n_p):
            for j, chip in enumerate(chips):
                drained(p, 1 + j).wait_recv()
                for cp in copies(p, 4 + j, (*chip, c), sibling, False):
                    cp.start()
        for p in range(n_p):
            for sem in (0, 4, 5, 6):
                drained(p, sem).wait_recv()
            for sem in range(7):
                drained(p, sem).wait_send()
            pltpu.make_async_copy(x_refs[p], x_refs[p], local_sems.at[p]).wait()

    flat = _pc(
        body, name=name, in_specs=[_ANY] * n_p, out_specs=[_ANY] * n_out, out_shape=out_shape,
        scratch_shapes=[pltpu.SemaphoreType.DMA((n_p, 7)), pltpu.SemaphoreType.DMA((n_p, 7)),
                        pltpu.SemaphoreType.DMA((n_p,))],
    )(*shards)
    res, at = [], 0
    for cnt in layers:
        res.append(list(flat[at:at + cnt]))
        at += cnt
    return res


def _alltoall(grads, axes, name):
    n_p = len(grads)
    layers = [len(g) for g in grads]
    n_in = sum(layers)
    blocks = []
    for g, ax in zip(grads, axes):
        kk, nn = g[0].shape
        blocks.append((kk // N_DEV, nn) if ax == 0 else (kk, nn // N_DEV))
    out_shape = [jax.ShapeDtypeStruct((N_DEV, cnt) + blk, g[0].dtype) for g, cnt, blk in zip(grads, layers, blocks)]

    def body(*refs):
        flat = refs[:n_in]
        lands = refs[n_in:n_in + n_p]
        send_sems, recv_sems, local_sems = refs[n_in + n_p:]
        g_refs, at = [], 0
        for cnt in layers:
            g_refs.append(flat[at:at + cnt])
            at += cnt
        x, y, c = lax.axis_index("x"), lax.axis_index("y"), lax.axis_index("c")
        me = 4 * x + 2 * y + c
        for p in range(n_p):
            size = blocks[p][axes[p]]
            for l in range(layers[p]):
                pltpu.make_async_copy(_window(g_refs[p][l], axes[p], me, size), lands[p].at[me, l], local_sems.at[p]).start()
            for rel in range(1, N_DEV):
                px = 1 - x if rel & 4 else x
                py = 1 - y if rel & 2 else y
                pc = 1 - c if rel & 1 else c
                for l in range(layers[p]):
                    pltpu.make_async_remote_copy(
                        src_ref=_window(g_refs[p][l], axes[p], 4 * px + 2 * py + pc, size), dst_ref=lands[p].at[me, l],
                        send_sem=send_sems.at[p, rel - 1], recv_sem=recv_sems.at[p, rel - 1],
                        device_id=(px, py, pc), device_id_type=MESH).start()
        for p in range(n_p):
            slab = lands[p].at[0]
            for rel in range(1, N_DEV):
                pltpu.make_async_remote_copy(
                    src_ref=slab, dst_ref=slab, send_sem=send_sems.at[p, rel - 1], recv_sem=recv_sems.at[p, rel - 1],
                    device_id=(x, y, c), device_id_type=MESH).wait()
            pltpu.make_async_copy(slab, slab, local_sems.at[p]).wait()

    return _pc(
        body, name=name, in_specs=[_ANY] * n_in, out_specs=[_ANY] * n_p, out_shape=out_shape,
        scratch_shapes=[pltpu.SemaphoreType.DMA((n_p, 7)), pltpu.SemaphoreType.DMA((n_p, 7)),
                        pltpu.SemaphoreType.DMA((n_p,))],
    )(*[g for gl in grads for g in gl])


def _gather_rows(a, name):
    r, c = a.shape
    return _allgather([a[None]], [0], name)[0][0].reshape(N_DEV, r, c)


def _conditioning(me):
    def forward(s16, w, b):
        nl, _, cols = w.shape
        part = jnp.concatenate([_matmul(s16, w[i], name="mod_fwd") for i in range(nl)], axis=0)
        full = _gather_rows(part, "mod_gather").reshape(N_DEV, nl, MOD_ROWS, cols)
        return full.transpose(1, 2, 0, 3).reshape(nl, MOD_ROWS, N_DEV * cols) + b[:, None, :]

    @jax.custom_vjp
    def f(s16, w, b):
        return forward(s16, w, b)

    def fwd(s16, w, b):
        return forward(s16, w, b), (s16, w)

    def bwd(res, dm):
        s16, w = res
        nl, dmodel, cols = w.shape
        width = dm.shape[-1]
        mine = lax.dynamic_slice_in_dim(dm, me, 1, axis=1)
        both = jnp.concatenate([mine, dm[:, N_DEV:N_DEV + 1]], axis=1).reshape(nl * 2, width)
        allrows = _gather_rows(both, "dmod_gather").reshape(N_DEV, nl, 2, width)
        total = jnp.concatenate([allrows[:, :, 0].transpose(1, 0, 2), jnp.sum(allrows[:, :, 1], axis=0)[:, None, :],
                                 jnp.zeros((nl, MOD_ROWS - N_DEV - 1, width), F32)], axis=1)
        db = jnp.sum(total, axis=1)
        my_cols = lax.dynamic_slice_in_dim(total, me * cols, cols, axis=2)
        dw = jnp.stack([_matmul(s16, my_cols[i], ta=True, name="mod_dw") for i in range(nl)])
        ds_part = _matmul(my_cols[0], w[0], tb=True, name="mod_ds")
        for i in range(1, nl):
            ds_part = ds_part + _matmul(my_cols[i], w[i], tb=True, name="mod_ds")
        ds = jnp.sum(_gather_rows(ds_part, "dcond_gather"), axis=0)
        return ds, dw, db

    f.defvjp(fwd, bwd)
    return f


def _rope_tables(s, rot_dim):
    t = jnp.arange(s, dtype=jnp.int32)
    rows, cols = t // GRID_W, t % GRID_W
    axis_dim = rot_dim // 2
    inv = jnp.power(ROPE_BASE, -jnp.arange(0, axis_dim, 2, dtype=F32) / axis_dim)
    ang_r = rows.astype(F32)[:, None] * inv
    ang_c = cols.astype(F32)[:, None] * inv
    ang = jnp.concatenate([ang_r, ang_r, ang_c, ang_c], axis=-1)
    return jnp.cos(ang), jnp.sin(ang)


def _rotate_half(z):
    z1, z2 = jnp.split(z, 2, axis=-1)
    return jnp.concatenate([-z2, z1], axis=-1)


def _rope(z, tables):
    if tables is None:
        return z
    cos, sin = tables
    half = z.shape[-1] // 2
    rot = jnp.concatenate([_rotate_half(z[..., :half]), _rotate_half(z[..., half:])], axis=-1)
    return z * cos[:, None, :] + rot * sin[:, None, :]


def _head_norm(name, z, g):
    s, h, d = z.shape
    return _norm(name)(z.reshape(s * h, d), g).reshape(s, h, d)


def _mla_q(tag, hx, rope, w):
    s = hx.shape[0]
    cq = _norm(tag + "_cq_norm")(_mm(tag + "_dq")(hx, w["mla_w_dq"]), w["mla_g_dq"])
    q = _mm(tag + "_uq")(cq, w["mla_w_uq"]).reshape(s, -1, MLA_NOPE + MLA_ROPE)
    q_nope = _head_norm(tag + "_qn_norm", q[..., :MLA_NOPE], w["mla_g_q_nope"])
    q_pe = _rope(_head_norm(tag + "_qp_norm", q[..., MLA_NOPE:], w["mla_g_q_pe"]), rope)
    return jnp.concatenate([q_nope, q_pe], axis=-1)


def _mla_kv(tag, hx, rope, w):
    s = hx.shape[0]
    kv_a = _mm(tag + "_dkv")(hx, w["mla_w_dkv"])
    rank = kv_a.shape[-1] - MLA_ROPE
    c_kv = _norm(tag + "_ckv_norm")(kv_a[:, :rank], w["mla_g_dkv"])
    k_pe = _rope(_norm(tag + "_kp_norm")(kv_a[:, rank:], w["mla_g_k_pe"])[:, None, :], rope)
    kv = _mm(tag + "_ukv")(c_kv, w["mla_w_ukv"]).reshape(s, -1, MLA_NOPE + MLA_V)
    heads = kv.shape[1]
    k_nope = _head_norm(tag + "_kn_norm", kv[..., :MLA_NOPE], w["mla_g_k_nope"])
    k = jnp.concatenate([k_nope, jnp.broadcast_to(k_pe, (s, heads, MLA_ROPE))], axis=-1)
    return k, kv[..., MLA_NOPE:]


def _gqa_q(tag, hx, rope, w):
    s = hx.shape[0]
    q = _mm(tag + "_q")(hx, w["gqa_w_q"]).reshape(s, -1, GQA_HEAD_DIM)
    return _rope(_head_norm(tag + "_q_norm", q, w["gqa_g_q"]), rope)


def _gqa_kv(tag, hx, rope, w):
    s = hx.shape[0]
    kv = _mm(tag + "_kv")(hx, w["gqa_w_kv"]).reshape(s, 2, GQA_KV_HEADS, GQA_HEAD_DIM)
    k = _rope(_head_norm(tag + "_k_norm", kv[:, 0], w["gqa_g_k"]), rope)
    return k, kv[:, 1]


def _conv_ffn(tag, hx, w):
    u = _mm(tag + "_up", BF16)(hx, w["ffn_w_up"])
    a = _convgate(tag + "_conv")(u, w["ffn_conv_w"], w["ffn_conv_b"])
    return _mm(tag + "_down")(a, w["ffn_w_down"])


def _forward(leaves, ctx, silu_c_all, me, depth):
    x = leaves["x"]
    s, d = x.shape
    rope_mla = _rope_tables(s, MLA_ROPE)
    rope_gqa = _rope_tables(s, GQA_HEAD_DIM)
    silu_cc = jax.nn.silu(leaves["c_ctx"])
    s16 = jnp.concatenate([silu_c_all, silu_cc[None, :], jnp.zeros((MOD_ROWS - N_DEV - 1, d), F32)], axis=0)
    mods = _conditioning(me)(s16, leaves["w_mod"], leaves["b_mod"])
    for i in range(depth):
        last = i == depth - 1
        w = {k: v[i // 2] for k, v in leaves["mixer"][i % 2].items()}
        w.update({k: v[i] for k, v in leaves["ffn"].items()})
        mod = lax.dynamic_index_in_dim(mods[i], me, axis=0, keepdims=False)
        sh1, sc1, g1, sh2, sc2, g2 = jnp.split(mod, N_MOD)
        csh1, csc1, cg1, csh2, csc2, cg2 = jnp.split(mods[i, N_DEV], N_MOD)
        tag = f"l{i}"
        hx = _modulate(tag + "_mix_mod")(x, leaves["norm_mix"][i], sh1, sc1)
        hc = _modulate(tag + "c_mix_mod")(ctx, leaves["norm_mix"][i], csh1, csc1)
        if i % 2 == 0:
            q_fn, kv_fn, w_o = _mla_q, _mla_kv, w["mla_w_o"]
            rope, scale = rope_mla, 1.0 / float(MLA_NOPE + MLA_ROPE) ** 0.5
        else:
            q_fn, kv_fn, w_o = _gqa_q, _gqa_kv, w["gqa_w_o"]
            rope, scale = rope_gqa, 1.0 / float(GQA_HEAD_DIM) ** 0.5
        k_lat, v_lat = kv_fn(tag, hx, rope, w)
        k_ctx, v_ctx = kv_fn(tag + "c", hc, None, w)
        o = _attention(tag + "_attn", scale)(q_fn(tag, hx, rope, w), jnp.concatenate([k_lat, k_ctx], axis=0),
                                             jnp.concatenate([v_lat, v_ctx], axis=0))
        x = _gres(tag + "_mix_res")(x, g1, _mm(tag + "_o")(o, w_o))
        if not last:
            oc = _attention(tag + "c_attn", scale)(q_fn(tag + "c", hc, None, w), k_ctx, v_ctx)
            ctx = _gres(tag + "c_mix_res")(ctx, cg1, _mm(tag + "c_o")(oc, w_o))
        hx = _modulate(tag + "_ffn_mod")(x, leaves["norm_ffn"][i], sh2, sc2)
        x = _gres(tag + "_ffn_res")(x, g2, _conv_ffn(tag, hx, w))
        if not last:
            hc = _modulate(tag + "c_ffn_mod")(ctx, leaves["norm_ffn"][i], csh2, csc2)
            ctx = _gres(tag + "c_ffn_res")(ctx, cg2, _conv_ffn(tag + "c", hc, w))
    return x


_MLA_BIG = {"mla_w_dq": 0, "mla_w_uq": 1, "mla_w_dkv": 0, "mla_w_ukv": 1, "mla_w_o": 0}
_GQA_BIG = {"gqa_w_q": 0, "gqa_w_kv": 0, "gqa_w_o": 0}
_FFN_BIG = {"ffn_w_up": 1, "ffn_w_down": 0}
_BIG_GROUPS = [("mla", _MLA_BIG), ("gqa", _GQA_BIG), ("ffn_up", {"ffn_w_up": 1}), ("ffn_down", {"ffn_w_down": 0})]
_MLA_GAINS = ["mla_g_dq", "mla_g_q_nope", "mla_g_q_pe", "mla_g_dkv", "mla_g_k_pe", "mla_g_k_nope"]
_GQA_GAINS = ["gqa_g_q", "gqa_g_k"]
_SHARED = ["norm_mix", "norm_ffn"] + _MLA_GAINS + _GQA_GAINS + ["ffn_conv_b"]
_SUMMED = ["c_ctx", "b_mod"]

_NAMES = ["c_ctx", "w_mod", "b_mod", "norm_mix", "norm_ffn", "mla_w_dq", "mla_g_dq", "mla_w_uq", "mla_g_q_nope",
          "mla_g_q_pe", "mla_w_dkv", "mla_g_dkv", "mla_g_k_pe", "mla_w_ukv", "mla_g_k_nope", "mla_w_o", "gqa_w_q",
          "gqa_g_q", "gqa_w_kv", "gqa_g_k", "gqa_w_o", "ffn_w_up", "ffn_conv_w", "ffn_conv_b", "ffn_w_down"]


def _rows128(a):
    return a.reshape(-1, LANES)


def kernel(x, c, ctx, c_ctx, w_mod, b_mod, norm_mix, norm_ffn, mla_w_dq, mla_g_dq, mla_w_uq, mla_g_q_nope, mla_g_q_pe, mla_w_dkv, mla_g_dkv, mla_g_k_pe, mla_w_ukv, mla_g_k_nope, mla_w_o, gqa_w_q, gqa_g_q, gqa_w_kv, gqa_g_k, gqa_w_o, ffn_w_up, ffn_conv_w, ffn_conv_b, ffn_w_down, loss_target, m_c_ctx, m_w_mod, m_b_mod, m_norm_mix, m_norm_ffn, m_mla_w_dq, m_mla_g_dq, m_mla_w_uq, m_mla_g_q_nope, m_mla_g_q_pe, m_mla_w_dkv, m_mla_g_dkv, m_mla_g_k_pe, m_mla_w_ukv, m_mla_g_k_nope, m_mla_w_o, m_gqa_w_q, m_gqa_g_q, m_gqa_w_kv, m_gqa_g_k, m_gqa_w_o, m_ffn_w_up, m_ffn_conv_w, m_ffn_conv_b, m_ffn_w_down, v_c_ctx, v_w_mod, v_b_mod, v_norm_mix, v_norm_ffn, v_mla_w_dq, v_mla_g_dq, v_mla_w_uq, v_mla_g_q_nope, v_mla_g_q_pe, v_mla_w_dkv, v_mla_g_dkv, v_mla_g_k_pe, v_mla_w_ukv, v_mla_g_k_nope, v_mla_w_o, v_gqa_w_q, v_gqa_g_q, v_gqa_w_kv, v_gqa_g_k, v_gqa_w_o, v_ffn_w_up, v_ffn_conv_w, v_ffn_conv_b, v_ffn_w_down):
    args = locals()
    wts = {n: args[n] for n in _NAMES}
    mom = {n: args["m_" + n] for n in _NAMES}
    var = {n: args["v_" + n] for n in _NAMES}
    me = 4 * lax.axis_index("x") + 2 * lax.axis_index("y") + lax.axis_index("c")
    depth = w_mod.shape[0]
    d_model = x.shape[-1]
    d_ff = ffn_conv_b.shape[-1]

    taps = ffn_conv_w.reshape(-1)
    packed = jnp.concatenate([jax.nn.silu(c).reshape(-1), taps])
    packed = jnp.concatenate([packed, jnp.zeros((-packed.size % (SUBLANES * LANES),), F32)])
    got = _gather_rows(_rows128(packed), "cond_gather").reshape(N_DEV, -1)
    silu_c_all = got[:, :d_model]
    conv_w_full = got[:, d_model:d_model + taps.size].reshape(N_DEV, depth, 3, -1)
    conv_w_full = conv_w_full.transpose(1, 2, 0, 3).reshape(depth, 3, d_ff)

    full = {}
    for gname, group in _BIG_GROUPS:
        names = list(group)
        got_w = _allgather([wts[n].astype(BF16) for n in names], [group[n] for n in names], "gather_" + gname)
        full.update(dict(zip(names, got_w)))

    leaves = {
        "x": x[0], "c_ctx": c_ctx, "w_mod": w_mod, "b_mod": b_mod, "norm_mix": norm_mix, "norm_ffn": norm_ffn,
        "mixer": [
            {**{n: full[n] for n in _MLA_BIG}, **{n: wts[n] for n in _MLA_GAINS}},
            {**{n: full[n] for n in _GQA_BIG}, **{n: wts[n] for n in _GQA_GAINS}},
        ],
        "ffn": {"ffn_w_up": full["ffn_w_up"], "ffn_w_down": full["ffn_w_down"], "ffn_conv_w": conv_w_full,
                "ffn_conv_b": ffn_conv_b},
    }
    y, pullback = jax.vjp(lambda lv: _forward(lv, ctx[0], silu_c_all, me, depth), leaves)
    loss_part, dy = _loss_call(y, loss_target[0])
    (gl,) = pullback(dy)
    loss = lax.psum(loss_part, ("x", "y", "c"))

    grads, deltas, new_m, new_v = {}, {}, {}, {}

    def put(n, outs, shape):
        grads[n], deltas[n], new_m[n], new_v[n] = (o.reshape(shape) for o in outs)

    gfull = {**{n: gl["mixer"][0][n] for n in _MLA_BIG}, **{n: gl["mixer"][1][n] for n in _GQA_BIG},
             "ffn_w_up": gl["ffn"]["ffn_w_up"], "ffn_w_down": gl["ffn"]["ffn_w_down"]}
    for gname, group in _BIG_GROUPS:
        names = list(group)
        lands = _alltoall([gfull[n] for n in names], [group[n] for n in names], "exchange_" + gname)
        for n, land in zip(names, lands):
            shape = wts[n].shape
            cols = shape[-1]
            outs = _adamw_call(wts[n].reshape(-1, cols), mom[n].reshape(-1, cols), var[n].reshape(-1, cols),
                               land.reshape(N_DEV, -1, cols), "adamw_" + n, True)
            put(n, outs, shape)

    cols = w_mod.shape[-1]
    outs = _adamw_call(w_mod.reshape(-1, cols), m_w_mod.reshape(-1, cols), v_w_mod.reshape(-1, cols),
                       gl["w_mod"].reshape(-1, cols), "adamw_w_mod", False)
    put("w_mod", outs, w_mod.shape)

    share = {"norm_mix": gl["norm_mix"], "norm_ffn": gl["norm_ffn"], "ffn_conv_b": gl["ffn"]["ffn_conv_b"]}
    share.update({n: gl["mixer"][0][n] for n in _MLA_GAINS})
    share.update({n: gl["mixer"][1][n] for n in _GQA_GAINS})
    whole = {n: jnp.where(me == 0, gl[n], 0.0) for n in _SUMMED}
    order = _SUMMED + _SHARED
    parts = [whole[n] if n in whole else share[n] for n in order]
    sizes = [p.size for p in parts]
    taps_g = gl["ffn"]["ffn_conv_w"]
    pack_g = jnp.concatenate([p.reshape(-1) for p in parts] + [taps_g.reshape(-1)])
    pad = (-pack_g.size // LANES) % SUBLANES * LANES
    pack_g = jnp.concatenate([pack_g, jnp.zeros((pad,), F32)])
    land = _gather_rows(_rows128(pack_g), "smallgrad_gather")

    def pack(src):
        flat = jnp.concatenate([src[n].reshape(-1) for n in order] + [jnp.zeros((taps_g.size + pad,), F32)])
        return _rows128(flat)

    outs = _adamw_call(pack(wts), pack(mom), pack(var), land, "adamw_small", True)
    at = 0
    for n, size in zip(order, sizes):
        put(n, [o.reshape(-1)[at:at + size] for o in outs], wts[n].shape)
        at += size
    taps_sum = outs[0].reshape(-1)[at:at + taps_g.size].reshape(taps_g.shape)
    my_taps = lax.dynamic_slice_in_dim(taps_sum, me * ffn_conv_w.shape[-1], ffn_conv_w.shape[-1], axis=2)
    outs = _adamw_call(_rows128(ffn_conv_w), _rows128(m_ffn_conv_w), _rows128(v_ffn_conv_w), _rows128(my_taps),
                       "adamw_conv_w", False)
    put("ffn_conv_w", outs, ffn_conv_w.shape)

    return (loss, gl["x"][None], *[grads[n] for n in _NAMES], *[deltas[n] for n in _NAMES],
            *[new_m[n] for n in _NAMES], *[new_v[n] for n in _NAMES])
```

```python
import functools

import jax
import jax.numpy as jnp
from jax import lax
from jax.experimental import pallas as pl
from jax.experimental.pallas import tpu as pltpu

F32 = jnp.float32
BF16 = jnp.bfloat16

EPS = 1e-6
ROPE_BASE = 10000.0
GRID_W = 64
N_MOD = 6
MLA_NOPE = 128
MLA_ROPE = 64
MLA_V = 128
GQA_HEAD_DIM = 128
GQA_KV_HEADS = 4
ADAM_LR = 0.001
ADAM_B1 = 0.9
ADAM_B2 = 0.999
ADAM_EPS = 1e-08
ADAM_WD = 0.01
ADAM_STEP = 10
N_DEV = 8
MOD_ROWS = 16

VMEM_LIMIT = 48 * 1024 * 1024
MATMUL_VMEM = 40 * 1024 * 1024
LANES = 128
SUBLANES = 8


def _pc(body, **kw):
    return pl.pallas_call(body, **kw)


def _cp(*sem):
    return pltpu.CompilerParams(dimension_semantics=sem if sem else None, vmem_limit_bytes=VMEM_LIMIT)


def _tile(n, cap, mult=LANES):
    if n <= cap:
        return n
    t = (cap // mult) * mult
    while t >= mult:
        if n % t == 0:
            return t
        t -= mult
    return n


def _matmul(a, b, *, ta=False, tb=False, name, out_dtype=F32):
    assert not (ta and tb)
    kd, m = a.shape if ta else a.shape[::-1]
    n, kb = b.shape if tb else b.shape[::-1]
    assert kd == kb, (a.shape, b.shape, ta, tb)
    tm, tn = _tile(m, 1024), _tile(n, 1024)
    out_bytes = jnp.dtype(out_dtype).itemsize

    def vmem_need(t):
        return 2 * t * (tm * a.dtype.itemsize + tn * b.dtype.itemsize) + tm * tn * (2 * out_bytes + 4)

    tk = next(t for t in (_tile(kd, 2048), _tile(kd, 1024), _tile(kd, 512)) if vmem_need(t) <= MATMUL_VMEM or t <= 512)
    nk = kd // tk
    if ta:
        dims = (((0,), (0,)), ((), ()))
        a_spec = pl.BlockSpec((tk, tm), lambda i, j, k: (k, i))
    else:
        dims = (((1,), (1 if tb else 0,)), ((), ()))
        a_spec = pl.BlockSpec((tm, tk), lambda i, j, k: (i, k))
    if tb:
        b_spec = pl.BlockSpec((tn, tk), lambda i, j, k: (j, k))
    else:
        b_spec = pl.BlockSpec((tk, tn), lambda i, j, k: (k, j))

    def product(a_ref, b_ref):
        return lax.dot_general(a_ref[...].astype(BF16), b_ref[...].astype(BF16), dims, preferred_element_type=F32)

    def body_one(a_ref, b_ref, o_ref):
        o_ref[...] = product(a_ref, b_ref).astype(o_ref.dtype)

    def body(a_ref, b_ref, o_ref, acc_ref):
        k = pl.program_id(2)

        @pl.when(k == 0)
        def _():
            acc_ref[...] = jnp.zeros_like(acc_ref)

        acc_ref[...] += product(a_ref, b_ref)

        @pl.when(k == nk - 1)
        def _():
            o_ref[...] = acc_ref[...].astype(o_ref.dtype)

    return _pc(
        body_one if nk == 1 else body, name=name, grid=(m // tm, n // tn, nk),
        in_specs=[a_spec, b_spec],
        out_specs=pl.BlockSpec((tm, tn), lambda i, j, k: (i, j)),
        out_shape=jax.ShapeDtypeStruct((m, n), out_dtype),
        scratch_shapes=[] if nk == 1 else [pltpu.VMEM((tm, tn), F32)],
        compiler_params=_cp("parallel", "parallel", "arbitrary"),
    )(a, b)


def _mm(name, out_dtype=F32):
    @jax.custom_vjp
    def f(a, w):
        return _matmul(a, w, name=name + "_fwd", out_dtype=out_dtype)

    def fwd(a, w):
        return f(a, w), (a, w)

    def bwd(res, dc):
        a, w = res
        return (_matmul(dc, w, tb=True, name=name + "_da", out_dtype=a.dtype),
                _matmul(a, dc, ta=True, name=name + "_dw", out_dtype=w.dtype))

    f.defvjp(fwd, bwd)
    return f


def _row_tile(r, d):
    return _tile(r, max(SUBLANES, min(4096, (2 << 20) // (4 * d))), SUBLANES)


def _norm_fwd_call(x, g, sc, sh, name, out_dtype=F32):
    r, d = x.shape
    tr = _row_tile(r, d)
    mod = sc is not None

    def body(*refs):
        if mod:
            x_ref, g_ref, sc_ref, sh_ref, y_ref = refs
        else:
            x_ref, g_ref, y_ref = refs
        xv = x_ref[...]
        y = xv * lax.rsqrt(jnp.mean(xv * xv, axis=-1, keepdims=True) + EPS) * g_ref[...]
        if mod:
            y = y * (1.0 + sc_ref[...]) + sh_ref[...]
        y_ref[...] = y.astype(y_ref.dtype)

    vec = pl.BlockSpec((1, d), lambda i: (0, 0))
    args = [x, g.reshape(1, d)] + ([sc.reshape(1, d), sh.reshape(1, d)] if mod else [])
    return _pc(
        body, name=name, grid=(r // tr,),
        in_specs=[pl.BlockSpec((tr, d), lambda i: (i, 0))] + [vec] * (len(args) - 1),
        out_specs=pl.BlockSpec((tr, d), lambda i: (i, 0)),
        out_shape=jax.ShapeDtypeStruct((r, d), out_dtype),
        compiler_params=_cp("parallel"),
    )(*args)


def _norm_bwd_call(x, g, sc, dy, name, add=None):
    r, d = x.shape
    tr = _row_tile(r, d)
    mod = sc is not None

    def body(*refs):
        add_ref = None
        if add is not None:
            add_ref, refs = refs[0], refs[1:]
        if mod:
            x_ref, g_ref, sc_ref, dy_ref, dx_ref, dg_ref, dsc_ref, dsh_ref = refs
        else:
            x_ref, g_ref, dy_ref, dx_ref, dg_ref = refs

        @pl.when(pl.program_id(0) == 0)
        def _():
            dg_ref[...] = jnp.zeros_like(dg_ref)
            if mod:
                dsc_ref[...] = jnp.zeros_like(dsc_ref)
                dsh_ref[...] = jnp.zeros_like(dsh_ref)

        xv = x_ref[...]
        gv = g_ref[...]
        dyv = dy_ref[...].astype(F32)
        rs = lax.rsqrt(jnp.mean(xv * xv, axis=-1, keepdims=True) + EPS)
        xh = xv * rs
        if mod:
            dsc_ref[...] += jnp.sum(dyv * (xh * gv), axis=0, keepdims=True)
            dsh_ref[...] += jnp.sum(dyv, axis=0, keepdims=True)
            t = dyv * (1.0 + sc_ref[...])
        else:
            t = dyv
        dg_ref[...] += jnp.sum(t * xh, axis=0, keepdims=True)
        dxh = t * gv
        dx = rs * (dxh - xh * jnp.mean(dxh * xh, axis=-1, keepdims=True))
        dx_ref[...] = dx if add_ref is None else add_ref[...] + dx

    vec = pl.BlockSpec((1, d), lambda i: (0, 0))
    blk = pl.BlockSpec((tr, d), lambda i: (i, 0))
    args = [x, g.reshape(1, d)] + ([sc.reshape(1, d)] if mod else []) + [dy]
    n_vec_out = 3 if mod else 1
    in_specs = [blk] + [vec] * (len(args) - 2) + [blk]
    if add is not None:
        args, in_specs = [add] + args, [blk] + in_specs
    outs = _pc(
        body, name=name, grid=(r // tr,),
        in_specs=in_specs,
        out_specs=[blk] + [vec] * n_vec_out,
        out_shape=[jax.ShapeDtypeStruct((r, d), F32)] + [jax.ShapeDtypeStruct((1, d), F32)] * n_vec_out,
        compiler_params=_cp("arbitrary"),
    )(*args)
    return outs


def _norm(name):
    @jax.custom_vjp
    def f(x, g):
        return _norm_fwd_call(x, g, None, None, name + "_fwd", out_dtype=BF16)

    def fwd(x, g):
        return f(x, g), (x, g)

    def bwd(res, dy):
        x, g = res
        dx, dg = _norm_bwd_call(x, g, None, dy, name + "_bwd")
        return dx, dg.reshape(g.shape)

    f.defvjp(fwd, bwd)
    return f


def _modulate(name):
    @jax.custom_vjp
    def f(x, g, sh, sc):
        return x, _norm_fwd_call(x, g, sc, sh, name + "_fwd", out_dtype=BF16)

    def fwd(x, g, sh, sc):
        return f(x, g, sh, sc), (x, g, sc)

    def bwd(res, cts):
        x, g, sc = res
        dx_res, dy = cts
        dx, dg, dsc, dsh = _norm_bwd_call(x, g, sc, dy, name + "_bwd", add=dx_res)
        return dx, dg.reshape(g.shape), dsh.reshape(g.shape), dsc.reshape(g.shape)

    f.defvjp(fwd, bwd)
    return f


def _gres(name):
    def fwd_call(x, gate, y):
        r, d = x.shape
        tr = _row_tile(r, d)

        def body(x_ref, g_ref, y_ref, o_ref):
            o_ref[...] = x_ref[...] + g_ref[...] * y_ref[...]

        blk = pl.BlockSpec((tr, d), lambda i: (i, 0))
        return _pc(
            body, name=name + "_fwd", grid=(r // tr,),
            in_specs=[blk, pl.BlockSpec((1, d), lambda i: (0, 0)), blk], out_specs=blk,
            out_shape=jax.ShapeDtypeStruct((r, d), F32), compiler_params=_cp("parallel"),
        )(x, gate.reshape(1, d), y)

    def bwd_call(do, gate, y):
        r, d = do.shape
        tr = _row_tile(r, d)

        def body(do_ref, g_ref, y_ref, dy_ref, dg_ref):
            @pl.when(pl.program_id(0) == 0)
            def _():
                dg_ref[...] = jnp.zeros_like(dg_ref)

            dov = do_ref[...]
            dy_ref[...] = g_ref[...] * dov
            dg_ref[...] += jnp.sum(dov * y_ref[...], axis=0, keepdims=True)

        blk = pl.BlockSpec((tr, d), lambda i: (i, 0))
        vec = pl.BlockSpec((1, d), lambda i: (0, 0))
        return _pc(
            body, name=name + "_bwd", grid=(r // tr,),
            in_specs=[blk, vec, blk], out_specs=[blk, vec],
            out_shape=[jax.ShapeDtypeStruct((r, d), F32), jax.ShapeDtypeStruct((1, d), F32)],
            compiler_params=_cp("arbitrary"),
        )(do, gate.reshape(1, d), y)

    @jax.custom_vjp
    def f(x, gate, y):
        return fwd_call(x, gate, y)

    def fwd(x, gate, y):
        return f(x, gate, y), (gate, y)

    def bwd(res, do):
        gate, y = res
        dy, dg = bwd_call(do, gate, y)
        return do, dg.reshape(gate.shape), dy

    f.defvjp(fwd, bwd)
    return f


_NT = (((1,), (1,)), ((), ()))
_NN = (((1,), (0,)), ((), ()))
_TN = (((0,), (0,)), ((), ()))
ATT_TQ = 2048
ATT_BTQ = 1024
ATT_SUB = 256
ATT_TK = 2816
LOG2E = 1.4426950408889634


def _attn_fwd_call(q, k, v1, scale, name):
    h, sq, dq = q.shape
    hkv, t, dvx = v1.shape
    dv = dvx - LANES
    grp = h // hkv
    tq, tk = _tile(sq, ATT_TQ), _tile(t, ATT_TK)
    sub = min(ATT_SUB, tq)
    nk = t // tk
    c = scale * LOG2E

    def body(q_ref, k_ref, v_ref, o_ref, lse_ref, m_s, acc_s):
        j = pl.program_id(2)

        @pl.when(j == 0)
        def _():
            m_s[...] = jnp.full_like(m_s, -jnp.inf)
            acc_s[...] = jnp.zeros_like(acc_s)

        kv, vv = k_ref[0], v_ref[0]
        for r in range(tq // sub):
            rows = pl.ds(r * sub, sub)
            s = lax.dot_general(q_ref[0, rows, :], kv, _NT, preferred_element_type=F32) * c
            m_prev = m_s[rows, :]
            m_new = jnp.maximum(m_prev, jnp.max(s, axis=-1, keepdims=True))
            p = jnp.exp2(s - m_new)
            acc_s[rows, :] = jnp.exp2(m_prev - m_new) * acc_s[rows, :] + lax.dot_general(
                p.astype(BF16), vv, _NN, preferred_element_type=F32)
            m_s[rows, :] = m_new

        @pl.when(j == nk - 1)
        def _():
            acc = acc_s[...]
            l = acc[:, dv:dv + 1]
            o_ref[0] = acc[:, :dv] / l
            lse_ref[0] = m_s[...] + jnp.log(l) * LOG2E

    return _pc(
        body, name=name, grid=(h, sq // tq, nk),
        in_specs=[pl.BlockSpec((1, tq, dq), lambda hh, i, j: (hh, i, 0)),
                  pl.BlockSpec((1, tk, dq), lambda hh, i, j: (hh // grp, j, 0)),
                  pl.BlockSpec((1, tk, dvx), lambda hh, i, j: (hh // grp, j, 0))],
        out_specs=[pl.BlockSpec((1, tq, dv), lambda hh, i, j: (hh, i, 0)),
                   pl.BlockSpec((1, tq, 1), lambda hh, i, j: (hh, i, 0))],
        out_shape=[jax.ShapeDtypeStruct((h, sq, dv), F32), jax.ShapeDtypeStruct((h, sq, 1), F32)],
        scratch_shapes=[pltpu.VMEM((tq, 1), F32), pltpu.VMEM((tq, dvx), F32)],
        compiler_params=_cp("parallel", "parallel", "arbitrary"),
    )(q, k, v1)


def _attn_bwd_call(q, k, v, o, do, lse, scale, name):
    h, sq, dq = q.shape
    hkv, t, dv = v.shape
    grp = h // hkv
    tq, tk = _tile(sq, ATT_BTQ), _tile(t, ATT_TK)
    sub = min(ATT_SUB, tq)
    nq = sq // tq
    c = scale * LOG2E

    def body(q_ref, k_ref, v_ref, o_ref, do_ref, lse_ref, dq_ref, dk_ref, dv_ref, dk_s, dv_s):
        j = pl.program_id(1)
        i = pl.program_id(2)

        @pl.when(i == 0)
        def _():
            dk_s[...] = jnp.zeros_like(dk_s)
            dv_s[...] = jnp.zeros_like(dv_s)

        kv, vv = k_ref[0], v_ref[0]
        for r in range(tq // sub):
            rows = pl.ds(r * sub, sub)
            qv, dov = q_ref[0, rows, :], do_ref[0, rows, :]
            s = lax.dot_general(qv, kv, _NT, preferred_element_type=F32) * c
            p = jnp.exp2(s - lse_ref[0, rows, :])
            delta = jnp.sum(dov.astype(F32) * o_ref[0, rows, :], axis=-1, keepdims=True)
            dv_s[...] += lax.dot_general(p.astype(BF16), dov, _TN, preferred_element_type=F32)
            dp = lax.dot_general(dov, vv, _NT, preferred_element_type=F32)
            ds = (p * (dp - delta) * scale).astype(BF16)
            dk_s[...] += lax.dot_general(ds, qv, _TN, preferred_element_type=F32)
            dq_blk = lax.dot_general(ds, kv, _NN, preferred_element_type=F32)
            out_rows = pl.ds(pl.multiple_of(i * tq + r * sub, sub), sub)

            @pl.when(j == 0)
            def _():
                dq_ref[0, out_rows, :] = dq_blk

            @pl.when(j > 0)
            def _():
                dq_ref[0, out_rows, :] += dq_blk

        @pl.when(i == nq - 1)
        def _():
            dk_ref[0] = dk_s[...]
            dv_ref[0] = dv_s[...]

    return _pc(
        body, name=name, grid=(h, t // tk, nq),
        in_specs=[pl.BlockSpec((1, tq, dq), lambda hh, j, i: (hh, i, 0)),
                  pl.BlockSpec((1, tk, dq), lambda hh, j, i: (hh // grp, j, 0)),
                  pl.BlockSpec((1, tk, dv), lambda hh, j, i: (hh // grp, j, 0)),
                  pl.BlockSpec((1, tq, dv), lambda hh, j, i: (hh, i, 0)),
                  pl.BlockSpec((1, tq, dv), lambda hh, j, i: (hh, i, 0)),
                  pl.BlockSpec((1, tq, 1), lambda hh, j, i: (hh, i, 0))],
        out_specs=[pl.BlockSpec((1, sq, dq), lambda hh, j, i: (hh, 0, 0)),
                   pl.BlockSpec((1, tk, dq), lambda hh, j, i: (hh, j, 0)),
                   pl.BlockSpec((1, tk, dv), lambda hh, j, i: (hh, j, 0))],
        out_shape=[jax.ShapeDtypeStruct((h, sq, dq), F32), jax.ShapeDtypeStruct((h, t, dq), F32),
                   jax.ShapeDtypeStruct((h, t, dv), F32)],
        scratch_shapes=[pltpu.VMEM((tk, dq), F32), pltpu.VMEM((tk, dv), F32)],
        compiler_params=_cp("parallel", "arbitrary", "arbitrary"),
    )(q, k, v, o, do, lse)


def _attention(name, scale):
    def heads_first(z):
        return z.transpose(1, 0, 2).astype(BF16)

    @jax.custom_vjp
    def f(q, k, v):
        return fwd(q, k, v)[0]

    def fwd(q, k, v):
        qt, kt, vt = heads_first(q), heads_first(k), heads_first(v)
        ones = jnp.zeros(vt.shape[:2] + (LANES,), BF16).at[:, :, 0].set(1.0)
        o, lse = _attn_fwd_call(qt, kt, jnp.concatenate([vt, ones], axis=-1), scale, name + "_fwd")
        s, h, _ = q.shape
        like = tuple(jnp.zeros((0,), z.dtype) for z in (q, k, v))
        return o.transpose(1, 0, 2).reshape(s, h * v.shape[-1]).astype(BF16), (qt, kt, vt, o, lse, like)

    def bwd(res, do):
        qt, kt, vt, o, lse, like = res
        h, s, _ = qt.shape
        hkv, t, dv = vt.shape
        dot = heads_first(do.reshape(s, h, dv))
        dq, dk, dvv = _attn_bwd_call(qt, kt, vt, o, dot, lse, scale, name + "_bwd")
        if h != hkv:
            dk = dk.reshape(hkv, h // hkv, t, -1).sum(axis=1)
            dvv = dvv.reshape(hkv, h // hkv, t, -1).sum(axis=1)
        return tuple(d.transpose(1, 0, 2).astype(z.dtype) for d, z in zip((dq, dk, dvv), like))

    f.defvjp(fwd, bwd)
    return f


CONV_TF = 128
CONV_ROWS = 256


def _conv_neighbours(ref, c, r0, cur, row, nchunks, s, rc):
    halo = SUBLANES * (4 // ref.dtype.itemsize)
    prev = ref[pl.ds(pl.multiple_of(jnp.maximum(r0 - halo, 0), halo), halo), :].astype(F32)
    nxt = ref[pl.ds(pl.multiple_of(jnp.minimum(r0 + rc, s - halo), halo), halo), :].astype(F32)
    prow = jnp.where(c > 0, prev[halo - 1:halo, :], 0.0)
    nrow = jnp.where(c < nchunks - 1, nxt[0:1, :], 0.0)
    before = jnp.where(row == 0, prow, pltpu.roll(cur, 1, 0))
    after = jnp.where(row == rc - 1, nrow, pltpu.roll(cur, rc - 1, 0))
    return before, after


def _conv_fwd_call(u, w, b, name):
    s, f2 = u.shape
    f = f2 // 2
    tf = CONV_TF
    nf = f // tf
    rc = min(CONV_ROWS, s)
    nchunks = s // rc

    def body(g_ref, v_ref, w_ref, b_ref, o_ref):
        w0, w1, w2, bv = w_ref[0:1, :], w_ref[1:2, :], w_ref[2:3, :], b_ref[...]
        row = lax.broadcasted_iota(jnp.int32, (rc, tf), 0)

        def chunk(c, carry):
            r0 = pl.multiple_of(c * rc, rc)
            cur = g_ref[pl.ds(r0, rc), :].astype(F32)
            before, after = _conv_neighbours(g_ref, c, r0, cur, row, nchunks, s, rc)
            gc = before * w0 + cur * w1 + after * w2 + bv
            val = v_ref[pl.ds(r0, rc), :].astype(F32)
            o_ref[pl.ds(r0, rc), :] = (gc * jax.nn.sigmoid(gc) * val).astype(o_ref.dtype)
            return carry

        lax.fori_loop(0, nchunks, chunk, 0)

    return _pc(
        body, name=name, grid=(nf,),
        in_specs=[pl.BlockSpec((s, tf), lambda j: (0, j)), pl.BlockSpec((s, tf), lambda j: (0, j + nf)),
                  pl.BlockSpec((3, tf), lambda j: (0, j)), pl.BlockSpec((1, tf), lambda j: (0, j))],
        out_specs=pl.BlockSpec((s, tf), lambda j: (0, j)),
        out_shape=jax.ShapeDtypeStruct((s, f), u.dtype),
        compiler_params=_cp("parallel"),
    )(u, u, w, b.reshape(1, f))


def _conv_bwd_call(u, w, b, da, name):
    s, f2 = u.shape
    f = f2 // 2
    tf = CONV_TF
    nf = f // tf
    rc = min(CONV_ROWS, s)
    nchunks = s // rc

    def body(g_ref, v_ref, da_ref, w_ref, b_ref, dg_ref, dv_ref, dw_ref, db_ref, dgc_s):
        w0, w1, w2, bv = w_ref[0:1, :], w_ref[1:2, :], w_ref[2:3, :], b_ref[...]
        row = lax.broadcasted_iota(jnp.int32, (rc, tf), 0)

        def chunk1(c, carry):
            a0, a1, a2, ab = carry
            r0 = pl.multiple_of(c * rc, rc)
            cur = g_ref[pl.ds(r0, rc), :].astype(F32)
            before, after = _conv_neighbours(g_ref, c, r0, cur, row, nchunks, s, rc)
            gc = before * w0 + cur * w1 + after * w2 + bv
            sig = jax.nn.sigmoid(gc)
            dav = da_ref[pl.ds(r0, rc), :].astype(F32)
            dv_ref[pl.ds(r0, rc), :] = (dav * (gc * sig)).astype(dv_ref.dtype)
            dgc = dav * v_ref[pl.ds(r0, rc), :].astype(F32) * (sig * (1.0 + gc * (1.0 - sig)))
            dgc_s[pl.ds(r0, rc), :] = dgc
            return (a0 + jnp.sum(dgc * before, axis=0, keepdims=True),
                    a1 + jnp.sum(dgc * cur, axis=0, keepdims=True),
                    a2 + jnp.sum(dgc * after, axis=0, keepdims=True),
                    ab + jnp.sum(dgc, axis=0, keepdims=True))

        z = jnp.zeros((1, tf), F32)
        a0, a1, a2, ab = lax.fori_loop(0, nchunks, chunk1, (z, z, z, z))
        dw_ref[0:1, :] = a0
        dw_ref[1:2, :] = a1
        dw_ref[2:3, :] = a2
        db_ref[...] = ab

        def chunk2(c, carry):
            r0 = pl.multiple_of(c * rc, rc)
            cur = dgc_s[pl.ds(r0, rc), :]
            before, after = _conv_neighbours(dgc_s, c, r0, cur, row, nchunks, s, rc)
            dg_ref[pl.ds(r0, rc), :] = (after * w0 + cur * w1 + before * w2).astype(dg_ref.dtype)
            return carry

        lax.fori_loop(0, nchunks, chunk2, 0)

    col = pl.BlockSpec((s, tf), lambda j: (0, j))
    return _pc(
        body, name=name, grid=(nf,),
        in_specs=[col, pl.BlockSpec((s, tf), lambda j: (0, j + nf)), col,
                  pl.BlockSpec((3, tf), lambda j: (0, j)), pl.BlockSpec((1, tf), lambda j: (0, j))],
        out_specs=[col, col, pl.BlockSpec((3, tf), lambda j: (0, j)), pl.BlockSpec((1, tf), lambda j: (0, j))],
        out_shape=[jax.ShapeDtypeStruct((s, f), u.dtype), jax.ShapeDtypeStruct((s, f), u.dtype),
                   jax.ShapeDtypeStruct((3, f), F32), jax.ShapeDtypeStruct((1, f), F32)],
        scratch_shapes=[pltpu.VMEM((s, tf), F32)],
        compiler_params=_cp("parallel"),
    )(u, u, da, w, b.reshape(1, f))


def _convgate(name):
    @jax.custom_vjp
    def f(u, w, b):
        return _conv_fwd_call(u, w, b, name + "_fwd")

    def fwd(u, w, b):
        return f(u, w, b), (u, w, b)

    def bwd(res, da):
        u, w, b = res
        dg, dv, dw, db = _conv_bwd_call(u, w, b, da, name + "_bwd")
        return jnp.concatenate([dg, dv], axis=-1), dw, db.reshape(b.shape)

    f.defvjp(fwd, bwd)
    return f


def _loss_call(y, tgt):
    r, d = y.shape
    tr = _row_tile(r, d)

    def body(y_ref, t_ref, dy_ref, part_ref):
        @pl.when(pl.program_id(0) == 0)
        def _():
            part_ref[...] = jnp.zeros_like(part_ref)

        diff = y_ref[...] - t_ref[...]
        dy_ref[...] = diff / d
        part_ref[...] += jnp.sum(diff * diff, axis=0, keepdims=True)

    blk = pl.BlockSpec((tr, d), lambda i: (i, 0))
    vec = pl.BlockSpec((1, d), lambda i: (0, 0))
    dy, part = _pc(
        body, name="loss_head", grid=(r // tr,), in_specs=[blk, blk], out_specs=[blk, vec],
        out_shape=[jax.ShapeDtypeStruct((r, d), F32), jax.ShapeDtypeStruct((1, d), F32)],
        compiler_params=_cp("arbitrary"),
    )(y, tgt)
    return 0.5 * (jnp.sum(part) / d), dy


def _adamw_call(w, m, v, g, name, summed):
    r, c = w.shape
    tr = _tile(r, max(2 * SUBLANES, (1 << 18) // c), 2 * SUBLANES)

    def body(g_ref, w_ref, m_ref, v_ref, go_ref, d_ref, mo_ref, vo_ref):
        if summed:
            gv = g_ref[0].astype(F32)
            for dev in range(1, N_DEV):
                gv = gv + g_ref[dev].astype(F32)
        else:
            gv = g_ref[...]
        mn = ADAM_B1 * m_ref[...] + (1.0 - ADAM_B1) * gv
        vn = ADAM_B2 * v_ref[...] + (1.0 - ADAM_B2) * (gv * gv)
        m_hat = mn / (1.0 - ADAM_B1 ** ADAM_STEP)
        v_hat = vn / (1.0 - ADAM_B2 ** ADAM_STEP)
        go_ref[...] = gv
        d_ref[...] = -ADAM_LR * (m_hat / (jnp.sqrt(v_hat) + ADAM_EPS) + ADAM_WD * w_ref[...])
        mo_ref[...] = mn
        vo_ref[...] = vn

    blk = pl.BlockSpec((tr, c), lambda i: (i, 0))
    gblk = pl.BlockSpec((N_DEV, tr, c), lambda i: (0, i, 0)) if summed else blk
    return _pc(
        body, name=name, grid=(r // tr,), in_specs=[gblk, blk, blk, blk], out_specs=[blk] * 4,
        out_shape=[jax.ShapeDtypeStruct((r, c), F32)] * 4, compiler_params=_cp("parallel"),
    )(g, w, m, v)


_ANY = pl.BlockSpec(memory_space=pl.ANY)
MESH = pl.DeviceIdType.MESH


def _window(ref, axis, idx, size):
    if axis == 0:
        return ref.at[pl.ds(idx * size, size), :]
    return ref.at[:, pl.ds(idx * size, size)]


def _allgather(shards, axes, name):
    n_p = len(shards)
    layers = [s.shape[0] for s in shards]
    out_shape = []
    for s, ax in zip(shards, axes):
        _, k, n = s.shape
        full = (k * N_DEV, n) if ax == 0 else (k, n * N_DEV)
        out_shape += [jax.ShapeDtypeStruct(full, s.dtype)] * s.shape[0]
    n_out = len(out_shape)

    def body(*refs):
        x_refs = refs[:n_p]
        flat = refs[n_p:n_p + n_out]
        send_sems, recv_sems, local_sems = refs[n_p + n_out:]
        outs, at = [], 0
        for cnt in layers:
            outs.append(flat[at:at + cnt])
            at += cnt
        x, y, c = lax.axis_index("x"), lax.axis_index("y"), lax.axis_index("c")
        me, sibling = (x, y, c), (x, y, 1 - c)
        chips = [(1 - x, y), (x, 1 - y), (1 - x, 1 - y)]

        def lin(d):
            return 4 * d[0] + 2 * d[1] + d[2]

        def copies(p, sem, block, to, from_shard):
            size = shards[p].shape[1 + axes[p]]
            res = []
            for l in range(layers[p]):
                dst = _window(outs[p][l], axes[p], lin(block), size)
                res.append(pltpu.make_async_remote_copy(
                    src_ref=x_refs[p].at[l] if from_shard else dst, dst_ref=dst,
                    send_sem=send_sems.at[p, sem], recv_sem=recv_sems.at[p, sem], device_id=to, device_id_type=MESH))
            return res

        def drained(p, sem):
            return pltpu.make_async_remote_copy(
                src_ref=x_refs[p], dst_ref=x_refs[p], send_sem=send_sems.at[p, sem], recv_sem=recv_sems.at[p, sem],
                device_id=me, device_id_type=MESH)

        for p in range(n_p):
            size = shards[p].shape[1 + axes[p]]
            for l in range(layers[p]):
                pltpu.make_async_copy(x_refs[p].at[l], _window(outs[p][l], axes[p], lin(me), size), local_sems.at[p]).start()
            for cp in copies(p, 0, me, sibling, True):
                cp.start()
            for j, chip in enumerate(chips):
                for cp in copies(p, 1 + j, me, (*chip, c), True):
                    cp.start()
        for p in range(n_p):
            for j, chip in enumerate(chips):
                drained(p, 1 + j).wait_recv()
                for cp in copies(p, 4 + j, (*chip, c), sibling, False):
                    cp.start()
        for p in range(n_p):
            for sem in (0, 4, 5, 6):
                drained(p, sem).wait_recv()
            for sem in range(7):
                drained(p, sem).wait_send()
            pltpu.make_async_copy(x_refs[p], x_refs[p], local_sems.at[p]).wait()

    flat = _pc(
        body, name=name, in_specs=[_ANY] * n_p, out_specs=[_ANY] * n_out, out_shape=out_shape,
        scratch_shapes=[pltpu.SemaphoreType.DMA((n_p, 7)), pltpu.SemaphoreType.DMA((n_p, 7)),
                        pltpu.SemaphoreType.DMA((n_p,))],
    )(*shards)
    res, at = [], 0
    for cnt in layers:
        res.append(list(flat[at:at + cnt]))
        at += cnt
    return res


def _alltoall(grads, axes, name):
    n_p = len(grads)
    layers = [len(g) for g in grads]
    n_in = sum(layers)
    blocks = []
    for g, ax in zip(grads, axes):
        kk, nn = g[0].shape
        blocks.append((kk // N_DEV, nn) if ax == 0 else (kk, nn // N_DEV))
    out_shape = [jax.ShapeDtypeStruct((N_DEV, cnt) + blk, g[0].dtype) for g, cnt, blk in zip(grads, layers, blocks)]

    def body(*refs):
        flat = refs[:n_in]
        lands = refs[n_in:n_in + n_p]
        send_sems, recv_sems, local_sems = refs[n_in + n_p:]
        g_refs, at = [], 0
        for cnt in layers:
            g_refs.append(flat[at:at + cnt])
            at += cnt
        x, y, c = lax.axis_index("x"), lax.axis_index("y"), lax.axis_index("c")
        me = 4 * x + 2 * y + c
        for p in range(n_p):
            size = blocks[p][axes[p]]
            for l in range(layers[p]):
                pltpu.make_async_copy(_window(g_refs[p][l], axes[p], me, size), lands[p].at[me, l], local_sems.at[p]).start()
            for rel in range(1, N_DEV):
                px = 1 - x if rel & 4 else x
                py = 1 - y if rel & 2 else y
                pc = 1 - c if rel & 1 else c
                for l in range(layers[p]):
                    pltpu.make_async_remote_copy(
                        src_ref=_window(g_refs[p][l], axes[p], 4 * px + 2 * py + pc, size), dst_ref=lands[p].at[me, l],
                        send_sem=send_sems.at[p, rel - 1], recv_sem=recv_sems.at[p, rel - 1],
                        device_id=(px, py, pc), device_id_type=MESH).start()
        for p in range(n_p):
            slab = lands[p].at[0]
            for rel in range(1, N_DEV):
                pltpu.make_async_remote_copy(
                    src_ref=slab, dst_ref=slab, send_sem=send_sems.at[p, rel - 1], recv_sem=recv_sems.at[p, rel - 1],
                    device_id=(x, y, c), device_id_type=MESH).wait()
            pltpu.make_async_copy(slab, slab, local_sems.at[p]).wait()

    return _pc(
        body, name=name, in_specs=[_ANY] * n_in, out_specs=[_ANY] * n_p, out_shape=out_shape,
        scratch_shapes=[pltpu.SemaphoreType.DMA((n_p, 7)), pltpu.SemaphoreType.DMA((n_p, 7)),
                        pltpu.SemaphoreType.DMA((n_p,))],
    )(*[g for gl in grads for g in gl])


def _gather_rows(a, name):
    r, c = a.shape
    return _allgather([a[None]], [0], name)[0][0].reshape(N_DEV, r, c)


def _conditioning(me):
    def forward(s16, w, b):
        nl, _, cols = w.shape
        part = jnp.concatenate([_matmul(s16, w[i], name="mod_fwd") for i in range(nl)], axis=0)
        full = _gather_rows(part, "mod_gather").reshape(N_DEV, nl, MOD_ROWS, cols)
        return full.transpose(1, 2, 0, 3).reshape(nl, MOD_ROWS, N_DEV * cols) + b[:, None, :]

    @jax.custom_vjp
    def f(s16, w, b):
        return forward(s16, w, b)

    def fwd(s16, w, b):
        return forward(s16, w, b), (s16, w)

    def bwd(res, dm):
        s16, w = res
        nl, dmodel, cols = w.shape
        width = dm.shape[-1]
        mine = lax.dynamic_slice_in_dim(dm, me, 1, axis=1)
        both = jnp.concatenate([mine, dm[:, N_DEV:N_DEV + 1]], axis=1).reshape(nl * 2, width)
        allrows = _gather_rows(both, "dmod_gather").reshape(N_DEV, nl, 2, width)
        total = jnp.concatenate([allrows[:, :, 0].transpose(1, 0, 2), jnp.sum(allrows[:, :, 1], axis=0)[:, None, :],
                                 jnp.zeros((nl, MOD_ROWS - N_DEV - 1, width), F32)], axis=1)
        db = jnp.sum(total, axis=1)
        my_cols = lax.dynamic_slice_in_dim(total, me * cols, cols, axis=2)
        dw = jnp.stack([_matmul(s16, my_cols[i], ta=True, name="mod_dw") for i in range(nl)])
        ds_part = _matmul(my_cols[0], w[0], tb=True, name="mod_ds")
        for i in range(1, nl):
            ds_part = ds_part + _matmul(my_cols[i], w[i], tb=True, name="mod_ds")
        ds = jnp.sum(_gather_rows(ds_part, "dcond_gather"), axis=0)
        return ds, dw, db

    f.defvjp(fwd, bwd)
    return f


def _rope_tables(s, rot_dim):
    t = jnp.arange(s, dtype=jnp.int32)
    rows, cols = t // GRID_W, t % GRID_W
    axis_dim = rot_dim // 2
    inv = jnp.power(ROPE_BASE, -jnp.arange(0, axis_dim, 2, dtype=F32) / axis_dim)
    ang_r = rows.astype(F32)[:, None] * inv
    ang_c = cols.astype(F32)[:, None] * inv
    ang = jnp.concatenate([ang_r, ang_r, ang_c, ang_c], axis=-1)
    return jnp.cos(ang), jnp.sin(ang)


def _rotate_half(z):
    z1, z2 = jnp.split(z, 2, axis=-1)
    return jnp.concatenate([-z2, z1], axis=-1)


def _rope(z, tables):
    if tables is None:
        return z
    cos, sin = tables
    half = z.shape[-1] // 2
    rot = jnp.concatenate([_rotate_half(z[..., :half]), _rotate_half(z[..., half:])], axis=-1)
    return (z * cos[:, None, :] + rot * sin[:, None, :]).astype(z.dtype)


def _head_norm(name, z, g):
    s, h, d = z.shape
    return _norm(name)(z.reshape(s * h, d), g).reshape(s, h, d)


def _mla_q(tag, hx, rope, w):
    s = hx.shape[0]
    cq = _norm(tag + "_cq_norm")(_mm(tag + "_dq")(hx, w["mla_w_dq"]), w["mla_g_dq"])
    q = _mm(tag + "_uq")(cq, w["mla_w_uq"]).reshape(s, -1, MLA_NOPE + MLA_ROPE)
    q_nope = _head_norm(tag + "_qn_norm", q[..., :MLA_NOPE], w["mla_g_q_nope"])
    q_pe = _rope(_head_norm(tag + "_qp_norm", q[..., MLA_NOPE:], w["mla_g_q_pe"]), rope)
    return jnp.concatenate([q_nope, q_pe], axis=-1)


def _mla_kv(tag, hx, rope, w):
    s = hx.shape[0]
    kv_a = _mm(tag + "_dkv")(hx, w["mla_w_dkv"])
    rank = kv_a.shape[-1] - MLA_ROPE
    c_kv = _norm(tag + "_ckv_norm")(kv_a[:, :rank], w["mla_g_dkv"])
    k_pe = _rope(_norm(tag + "_kp_norm")(kv_a[:, rank:], w["mla_g_k_pe"])[:, None, :], rope)
    kv = _mm(tag + "_ukv")(c_kv, w["mla_w_ukv"]).reshape(s, -1, MLA_NOPE + MLA_V)
    heads = kv.shape[1]
    k_nope = _head_norm(tag + "_kn_norm", kv[..., :MLA_NOPE], w["mla_g_k_nope"])
    k = jnp.concatenate([k_nope, jnp.broadcast_to(k_pe, (s, heads, MLA_ROPE))], axis=-1)
    return k, kv[..., MLA_NOPE:]


def _gqa_q(tag, hx, rope, w):
    s = hx.shape[0]
    q = _mm(tag + "_q")(hx, w["gqa_w_q"]).reshape(s, -1, GQA_HEAD_DIM)
    return _rope(_head_norm(tag + "_q_norm", q, w["gqa_g_q"]), rope)


def _gqa_kv(tag, hx, rope, w):
    s = hx.shape[0]
    kv = _mm(tag + "_kv")(hx, w["gqa_w_kv"]).reshape(s, 2, GQA_KV_HEADS, GQA_HEAD_DIM)
    k = _rope(_head_norm(tag + "_k_norm", kv[:, 0], w["gqa_g_k"]), rope)
    return k, kv[:, 1]


def _conv_ffn(tag, hx, w):
    u = _mm(tag + "_up", BF16)(hx, w["ffn_w_up"])
    a = _convgate(tag + "_conv")(u, w["ffn_conv_w"], w["ffn_conv_b"])
    return _mm(tag + "_down")(a, w["ffn_w_down"])


def _forward(leaves, ctx, silu_c_all, me, depth):
    x = leaves["x"]
    s, d = x.shape
    rope_mla = _rope_tables(s, MLA_ROPE)
    rope_gqa = _rope_tables(s, GQA_HEAD_DIM)
    silu_cc = jax.nn.silu(leaves["c_ctx"])
    s16 = jnp.concatenate([silu_c_all, silu_cc[None, :], jnp.zeros((MOD_ROWS - N_DEV - 1, d), F32)], axis=0)
    mods = _conditioning(me)(s16, leaves["w_mod"], leaves["b_mod"])
    for i in range(depth):
        last = i == depth - 1
        w = {k: v[i // 2] for k, v in leaves["mixer"][i % 2].items()}
        w.update({k: v[i] for k, v in leaves["ffn"].items()})
        mod = lax.dynamic_index_in_dim(mods[i], me, axis=0, keepdims=False)
        sh1, sc1, g1, sh2, sc2, g2 = jnp.split(mod, N_MOD)
        csh1, csc1, cg1, csh2, csc2, cg2 = jnp.split(mods[i, N_DEV], N_MOD)
        tag = f"l{i}"
        x, hx = _modulate(tag + "_mix_mod")(x, leaves["norm_mix"][i], sh1, sc1)
        ctx, hc = _modulate(tag + "c_mix_mod")(ctx, leaves["norm_mix"][i], csh1, csc1)
        if i % 2 == 0:
            q_fn, kv_fn, w_o = _mla_q, _mla_kv, w["mla_w_o"]
            rope, scale = rope_mla, 1.0 / float(MLA_NOPE + MLA_ROPE) ** 0.5
        else:
            q_fn, kv_fn, w_o = _gqa_q, _gqa_kv, w["gqa_w_o"]
            rope, scale = rope_gqa, 1.0 / float(GQA_HEAD_DIM) ** 0.5
        k_lat, v_lat = kv_fn(tag, hx, rope, w)
        k_ctx, v_ctx = kv_fn(tag + "c", hc, None, w)
        o = _attention(tag + "_attn", scale)(q_fn(tag, hx, rope, w), jnp.concatenate([k_lat, k_ctx], axis=0),
                                             jnp.concatenate([v_lat, v_ctx], axis=0))
        x = _gres(tag + "_mix_res")(x, g1, _mm(tag + "_o")(o, w_o))
        if not last:
            oc = _attention(tag + "c_attn", scale)(q_fn(tag + "c", hc, None, w), k_ctx, v_ctx)
            ctx = _gres(tag + "c_mix_res")(ctx, cg1, _mm(tag + "c_o")(oc, w_o))
        x, hx = _modulate(tag + "_ffn_mod")(x, leaves["norm_ffn"][i], sh2, sc2)
        x = _gres(tag + "_ffn_res")(x, g2, _conv_ffn(tag, hx, w))
        if not last:
            ctx, hc = _modulate(tag + "c_ffn_mod")(ctx, leaves["norm_ffn"][i], csh2, csc2)
            ctx = _gres(tag + "c_ffn_res")(ctx, cg2, _conv_ffn(tag + "c", hc, w))
    return x


_MLA_BIG = {"mla_w_dq": 0, "mla_w_uq": 1, "mla_w_dkv": 0, "mla_w_ukv": 1, "mla_w_o": 0}
_GQA_BIG = {"gqa_w_q": 0, "gqa_w_kv": 0, "gqa_w_o": 0}
_FFN_BIG = {"ffn_w_up": 1, "ffn_w_down": 0}
_BIG_GROUPS = [("mla", _MLA_BIG), ("gqa", _GQA_BIG), ("ffn_up", {"ffn_w_up": 1}), ("ffn_down", {"ffn_w_down": 0})]
_MLA_GAINS = ["mla_g_dq", "mla_g_q_nope", "mla_g_q_pe", "mla_g_dkv", "mla_g_k_pe", "mla_g_k_nope"]
_GQA_GAINS = ["gqa_g_q", "gqa_g_k"]
_SHARED = ["norm_mix", "norm_ffn"] + _MLA_GAINS + _GQA_GAINS + ["ffn_conv_b"]
_SUMMED = ["c_ctx", "b_mod"]

_NAMES = ["c_ctx", "w_mod", "b_mod", "norm_mix", "norm_ffn", "mla_w_dq", "mla_g_dq", "mla_w_uq", "mla_g_q_nope",
          "mla_g_q_pe", "mla_w_dkv", "mla_g_dkv", "mla_g_k_pe", "mla_w_ukv", "mla_g_k_nope", "mla_w_o", "gqa_w_q",
          "gqa_g_q", "gqa_w_kv", "gqa_g_k", "gqa_w_o", "ffn_w_up", "ffn_conv_w", "ffn_conv_b", "ffn_w_down"]


def _rows128(a):
    return a.reshape(-1, LANES)


def kernel(x, c, ctx, c_ctx, w_mod, b_mod, norm_mix, norm_ffn, mla_w_dq, mla_g_dq, mla_w_uq, mla_g_q_nope, mla_g_q_pe, mla_w_dkv, mla_g_dkv, mla_g_k_pe, mla_w_ukv, mla_g_k_nope, mla_w_o, gqa_w_q, gqa_g_q, gqa_w_kv, gqa_g_k, gqa_w_o, ffn_w_up, ffn_conv_w, ffn_conv_b, ffn_w_down, loss_target, m_c_ctx, m_w_mod, m_b_mod, m_norm_mix, m_norm_ffn, m_mla_w_dq, m_mla_g_dq, m_mla_w_uq, m_mla_g_q_nope, m_mla_g_q_pe, m_mla_w_dkv, m_mla_g_dkv, m_mla_g_k_pe, m_mla_w_ukv, m_mla_g_k_nope, m_mla_w_o, m_gqa_w_q, m_gqa_g_q, m_gqa_w_kv, m_gqa_g_k, m_gqa_w_o, m_ffn_w_up, m_ffn_conv_w, m_ffn_conv_b, m_ffn_w_down, v_c_ctx, v_w_mod, v_b_mod, v_norm_mix, v_norm_ffn, v_mla_w_dq, v_mla_g_dq, v_mla_w_uq, v_mla_g_q_nope, v_mla_g_q_pe, v_mla_w_dkv, v_mla_g_dkv, v_mla_g_k_pe, v_mla_w_ukv, v_mla_g_k_nope, v_mla_w_o, v_gqa_w_q, v_gqa_g_q, v_gqa_w_kv, v_gqa_g_k, v_gqa_w_o, v_ffn_w_up, v_ffn_conv_w, v_ffn_conv_b, v_ffn_w_down):
    args = locals()
    wts = {n: args[n] for n in _NAMES}
    mom = {n: args["m_" + n] for n in _NAMES}
    var = {n: args["v_" + n] for n in _NAMES}
    me = 4 * lax.axis_index("x") + 2 * lax.axis_index("y") + lax.axis_index("c")
    depth = w_mod.shape[0]
    d_model = x.shape[-1]
    d_ff = ffn_conv_b.shape[-1]

    taps = ffn_conv_w.reshape(-1)
    packed = jnp.concatenate([jax.nn.silu(c).reshape(-1), taps])
    packed = jnp.concatenate([packed, jnp.zeros((-packed.size % (SUBLANES * LANES),), F32)])
    got = _gather_rows(_rows128(packed), "cond_gather").reshape(N_DEV, -1)
    silu_c_all = got[:, :d_model]
    conv_w_full = got[:, d_model:d_model + taps.size].reshape(N_DEV, depth, 3, -1)
    conv_w_full = conv_w_full.transpose(1, 2, 0, 3).reshape(depth, 3, d_ff)

    full = {}
    for gname, group in _BIG_GROUPS:
        names = list(group)
        got_w = _allgather([wts[n].astype(BF16) for n in names], [group[n] for n in names], "gather_" + gname)
        full.update(dict(zip(names, got_w)))

    leaves = {
        "x": x[0], "c_ctx": c_ctx, "w_mod": w_mod, "b_mod": b_mod, "norm_mix": norm_mix, "norm_ffn": norm_ffn,
        "mixer": [
            {**{n: full[n] for n in _MLA_BIG}, **{n: wts[n] for n in _MLA_GAINS}},
            {**{n: full[n] for n in _GQA_BIG}, **{n: wts[n] for n in _GQA_GAINS}},
        ],
        "ffn": {"ffn_w_up": full["ffn_w_up"], "ffn_w_down": full["ffn_w_down"], "ffn_conv_w": conv_w_full,
                "ffn_conv_b": ffn_conv_b},
    }
    y, pullback = jax.vjp(lambda lv: _forward(lv, ctx[0], silu_c_all, me, depth), leaves)
    loss_part, dy = _loss_call(y, loss_target[0])
    (gl,) = pullback(dy)
    loss = lax.psum(loss_part, ("x", "y", "c"))

    grads, deltas, new_m, new_v = {}, {}, {}, {}

    def put(n, outs, shape):
        grads[n], deltas[n], new_m[n], new_v[n] = (o.reshape(shape) for o in outs)

    gfull = {**{n: gl["mixer"][0][n] for n in _MLA_BIG}, **{n: gl["mixer"][1][n] for n in _GQA_BIG},
             "ffn_w_up": gl["ffn"]["ffn_w_up"], "ffn_w_down": gl["ffn"]["ffn_w_down"]}
    for gname, group in _BIG_GROUPS:
        names = list(group)
        lands = _alltoall([gfull[n] for n in names], [group[n] for n in names], "exchange_" + gname)
        for n, land in zip(names, lands):
            shape = wts[n].shape
            cols = shape[-1]
            outs = _adamw_call(wts[n].reshape(-1, cols), mom[n].reshape(-1, cols), var[n].reshape(-1, cols),
                               land.reshape(N_DEV, -1, cols), "adamw_" + n, True)
            put(n, outs, shape)

    cols = w_mod.shape[-1]
    outs = _adamw_call(w_mod.reshape(-1, cols), m_w_mod.reshape(-1, cols), v_w_mod.reshape(-1, cols),
                       gl["w_mod"].reshape(-1, cols), "adamw_w_mod", False)
    put("w_mod", outs, w_mod.shape)

    share = {"norm_mix": gl["norm_mix"], "norm_ffn": gl["norm_ffn"], "ffn_conv_b": gl["ffn"]["ffn_conv_b"]}
    share.update({n: gl["mixer"][0][n] for n in _MLA_GAINS})
    share.update({n: gl["mixer"][1][n] for n in _GQA_GAINS})
    whole = {n: jnp.where(me == 0, gl[n], 0.0) for n in _SUMMED}
    order = _SUMMED + _SHARED
    parts = [whole[n] if n in whole else share[n] for n in order]
    sizes = [p.size for p in parts]
    taps_g = gl["ffn"]["ffn_conv_w"]
    pack_g = jnp.concatenate([p.reshape(-1) for p in parts] + [taps_g.reshape(-1)])
    pad = (-pack_g.size // LANES) % SUBLANES * LANES
    pack_g = jnp.concatenate([pack_g, jnp.zeros((pad,), F32)])
    land = _gather_rows(_rows128(pack_g), "smallgrad_gather")

    def pack(src):
        flat = jnp.concatenate([src[n].reshape(-1) for n in order] + [jnp.zeros((taps_g.size + pad,), F32)])
        return _rows128(flat)

    outs = _adamw_call(pack(wts), pack(mom), pack(var), land, "adamw_small", True)
    at = 0
    for n, size in zip(order, sizes):
        put(n, [o.reshape(-1)[at:at + size] for o in outs], wts[n].shape)
        at += size
    taps_sum = outs[0].reshape(-1)[at:at + taps_g.size].reshape(taps_g.shape)
    my_taps = lax.dynamic_slice_in_dim(taps_sum, me * ffn_conv_w.shape[-1], ffn_conv_w.shape[-1], axis=2)
    outs = _adamw_call(_rows128(ffn_conv_w), _rows128(m_ffn_conv_w), _rows128(v_ffn_conv_w), _rows128(my_taps),
                       "adamw_conv_w", False)
    put("ffn_conv_w", outs, ffn_conv_w.shape)

    return (loss, gl["x"][None], *[grads[n] for n in _NAMES], *[deltas[n] for n in _NAMES],
            *[new_m[n] for n in _NAMES], *[new_v[n] for n in _NAMES])
```

```python
import functools

import jax
import jax.numpy as jnp
from jax import lax
from jax.experimental import pallas as pl
from jax.experimental.pallas import tpu as pltpu

F32 = jnp.float32
BF16 = jnp.bfloat16

EPS = 1e-6
ROPE_BASE = 10000.0
GRID_W = 64
N_MOD = 6
MLA_NOPE = 128
MLA_ROPE = 64
MLA_V = 128
GQA_HEAD_DIM = 128
GQA_KV_HEADS = 4
ADAM_LR = 0.001
ADAM_B1 = 0.9
ADAM_B2 = 0.999
ADAM_EPS = 1e-08
ADAM_WD = 0.01
ADAM_STEP = 10
N_DEV = 8
MOD_ROWS = 16

VMEM_LIMIT = 48 * 1024 * 1024
MATMUL_VMEM = 40 * 1024 * 1024
LANES = 128
SUBLANES = 8


def _pc(body, **kw):
    return pl.pallas_call(body, **kw)


def _cp(*sem):
    return pltpu.CompilerParams(dimension_semantics=sem if sem else None, vmem_limit_bytes=VMEM_LIMIT)


def _tile(n, cap, mult=LANES):
    if n <= cap:
        return n
    t = (cap // mult) * mult
    while t >= mult:
        if n % t == 0:
            return t
        t -= mult
    return n


def _matmul(a, b, *, ta=False, tb=False, name, out_dtype=F32):
    assert not (ta and tb)
    kd, m = a.shape if ta else a.shape[::-1]
    n, kb = b.shape if tb else b.shape[::-1]
    assert kd == kb, (a.shape, b.shape, ta, tb)
    tm, tn = _tile(m, 1024), _tile(n, 1024)
    out_bytes = jnp.dtype(out_dtype).itemsize

    def vmem_need(t):
        return 2 * t * (tm * a.dtype.itemsize + tn * b.dtype.itemsize) + tm * tn * (2 * out_bytes + 4)

    tk = next(t for t in (_tile(kd, 2048), _tile(kd, 1024), _tile(kd, 512)) if vmem_need(t) <= MATMUL_VMEM or t <= 512)
    nk = kd // tk
    if ta:
        dims = (((0,), (0,)), ((), ()))
        a_spec = pl.BlockSpec((tk, tm), lambda i, j, k: (k, i))
    else:
        dims = (((1,), (1 if tb else 0,)), ((), ()))
        a_spec = pl.BlockSpec((tm, tk), lambda i, j, k: (i, k))
    if tb:
        b_spec = pl.BlockSpec((tn, tk), lambda i, j, k: (j, k))
    else:
        b_spec = pl.BlockSpec((tk, tn), lambda i, j, k: (k, j))

    def product(a_ref, b_ref):
        return lax.dot_general(a_ref[...].astype(BF16), b_ref[...].astype(BF16), dims, preferred_element_type=F32)

    def body_one(a_ref, b_ref, o_ref):
        o_ref[...] = product(a_ref, b_ref).astype(o_ref.dtype)

    def body(a_ref, b_ref, o_ref, acc_ref):
        k = pl.program_id(2)

        @pl.when(k == 0)
        def _():
            acc_ref[...] = jnp.zeros_like(acc_ref)

        acc_ref[...] += product(a_ref, b_ref)

        @pl.when(k == nk - 1)
        def _():
            o_ref[...] = acc_ref[...].astype(o_ref.dtype)

    return _pc(
        body_one if nk == 1 else body, name=name, grid=(m // tm, n // tn, nk),
        in_specs=[a_spec, b_spec],
        out_specs=pl.BlockSpec((tm, tn), lambda i, j, k: (i, j)),
        out_shape=jax.ShapeDtypeStruct((m, n), out_dtype),
        scratch_shapes=[] if nk == 1 else [pltpu.VMEM((tm, tn), F32)],
        compiler_params=_cp("parallel", "parallel", "arbitrary"),
    )(a, b)


def _mm(name, out_dtype=F32):
    @jax.custom_vjp
    def f(a, w):
        return _matmul(a, w, name=name + "_fwd", out_dtype=out_dtype)

    def fwd(a, w):
        return f(a, w), (a, w)

    def bwd(res, dc):
        a, w = res
        return (_matmul(dc, w, tb=True, name=name + "_da", out_dtype=a.dtype),
                _matmul(a, dc, ta=True, name=name + "_dw", out_dtype=w.dtype))

    f.defvjp(fwd, bwd)
    return f


def _row_tile(r, d):
    return _tile(r, max(SUBLANES, min(4096, (2 << 20) // (4 * d))), SUBLANES)


def _norm_fwd_call(x, g, sc, sh, name, out_dtype=F32):
    r, d = x.shape
    tr = _row_tile(r, d)
    mod = sc is not None

    def body(*refs):
        if mod:
            x_ref, g_ref, sc_ref, sh_ref, y_ref = refs
        else:
            x_ref, g_ref, y_ref = refs
        xv = x_ref[...]
        y = xv * lax.rsqrt(jnp.mean(xv * xv, axis=-1, keepdims=True) + EPS) * g_ref[...]
        if mod:
            y = y * (1.0 + sc_ref[...]) + sh_ref[...]
        y_ref[...] = y.astype(y_ref.dtype)

    vec = pl.BlockSpec((1, d), lambda i: (0, 0))
    args = [x, g.reshape(1, d)] + ([sc.reshape(1, d), sh.reshape(1, d)] if mod else [])
    return _pc(
        body, name=name, grid=(r // tr,),
        in_specs=[pl.BlockSpec((tr, d), lambda i: (i, 0))] + [vec] * (len(args) - 1),
        out_specs=pl.BlockSpec((tr, d), lambda i: (i, 0)),
        out_shape=jax.ShapeDtypeStruct((r, d), out_dtype),
        compiler_params=_cp("parallel"),
    )(*args)


def _norm_bwd_call(x, g, sc, dy, name, add=None):
    r, d = x.shape
    tr = _row_tile(r, d)
    mod = sc is not None

    def body(*refs):
        add_ref = None
        if add is not None:
            add_ref, refs = refs[0], refs[1:]
        if mod:
            x_ref, g_ref, sc_ref, dy_ref, dx_ref, dg_ref, dsc_ref, dsh_ref = refs
        else:
            x_ref, g_ref, dy_ref, dx_ref, dg_ref = refs

        @pl.when(pl.program_id(0) == 0)
        def _():
            dg_ref[...] = jnp.zeros_like(dg_ref)
            if mod:
                dsc_ref[...] = jnp.zeros_like(dsc_ref)
                dsh_ref[...] = jnp.zeros_like(dsh_ref)

        xv = x_ref[...]
        gv = g_ref[...]
        dyv = dy_ref[...].astype(F32)
        rs = lax.rsqrt(jnp.mean(xv * xv, axis=-1, keepdims=True) + EPS)
        xh = xv * rs
        if mod:
            dsc_ref[...] += jnp.sum(dyv * (xh * gv), axis=0, keepdims=True)
            dsh_ref[...] += jnp.sum(dyv, axis=0, keepdims=True)
            t = dyv * (1.0 + sc_ref[...])
        else:
            t = dyv
        dg_ref[...] += jnp.sum(t * xh, axis=0, keepdims=True)
        dxh = t * gv
        dx = rs * (dxh - xh * jnp.mean(dxh * xh, axis=-1, keepdims=True))
        dx_ref[...] = dx if add_ref is None else add_ref[...] + dx

    vec = pl.BlockSpec((1, d), lambda i: (0, 0))
    blk = pl.BlockSpec((tr, d), lambda i: (i, 0))
    args = [x, g.reshape(1, d)] + ([sc.reshape(1, d)] if mod else []) + [dy]
    n_vec_out = 3 if mod else 1
    in_specs = [blk] + [vec] * (len(args) - 2) + [blk]
    if add is not None:
        args, in_specs = [add] + args, [blk] + in_specs
    outs = _pc(
        body, name=name, grid=(r // tr,),
        in_specs=in_specs,
        out_specs=[blk] + [vec] * n_vec_out,
        out_shape=[jax.ShapeDtypeStruct((r, d), F32)] + [jax.ShapeDtypeStruct((1, d), F32)] * n_vec_out,
        compiler_params=_cp("arbitrary"),
    )(*args)
    return outs


def _norm(name):
    @jax.custom_vjp
    def f(x, g):
        return _norm_fwd_call(x, g, None, None, name + "_fwd", out_dtype=BF16)

    def fwd(x, g):
        return f(x, g), (x, g)

    def bwd(res, dy):
        x, g = res
        dx, dg = _norm_bwd_call(x, g, None, dy, name + "_bwd")
        return dx, dg.reshape(g.shape)

    f.defvjp(fwd, bwd)
    return f


def _modulate(name):
    @jax.custom_vjp
    def f(x, g, sh, sc):
        return x, _norm_fwd_call(x, g, sc, sh, name + "_fwd", out_dtype=BF16)

    def fwd(x, g, sh, sc):
        return f(x, g, sh, sc), (x, g, sc)

    def bwd(res, cts):
        x, g, sc = res
        dx_res, dy = cts
        dx, dg, dsc, dsh = _norm_bwd_call(x, g, sc, dy, name + "_bwd", add=dx_res)
        return dx, dg.reshape(g.shape), dsh.reshape(g.shape), dsc.reshape(g.shape)

    f.defvjp(fwd, bwd)
    return f


def _gres(name):
    def fwd_call(x, gate, y):
        r, d = x.shape
        tr = _row_tile(r, d)

        def body(x_ref, g_ref, y_ref, o_ref):
            o_ref[...] = x_ref[...] + g_ref[...] * y_ref[...]

        blk = pl.BlockSpec((tr, d), lambda i: (i, 0))
        return _pc(
            body, name=name + "_fwd", grid=(r // tr,),
            in_specs=[blk, pl.BlockSpec((1, d), lambda i: (0, 0)), blk], out_specs=blk,
            out_shape=jax.ShapeDtypeStruct((r, d), F32), compiler_params=_cp("parallel"),
        )(x, gate.reshape(1, d), y)

    def bwd_call(do, gate, y):
        r, d = do.shape
        tr = _row_tile(r, d)

        def body(do_ref, g_ref, y_ref, dy_ref, dg_ref):
            @pl.when(pl.program_id(0) == 0)
            def _():
                dg_ref[...] = jnp.zeros_like(dg_ref)

            dov = do_ref[...]
            dy_ref[...] = g_ref[...] * dov
            dg_ref[...] += jnp.sum(dov * y_ref[...], axis=0, keepdims=True)

        blk = pl.BlockSpec((tr, d), lambda i: (i, 0))
        vec = pl.BlockSpec((1, d), lambda i: (0, 0))
        return _pc(
            body, name=name + "_bwd", grid=(r // tr,),
            in_specs=[blk, vec, blk], out_specs=[blk, vec],
            out_shape=[jax.ShapeDtypeStruct((r, d), F32), jax.ShapeDtypeStruct((1, d), F32)],
            compiler_params=_cp("arbitrary"),
        )(do, gate.reshape(1, d), y)

    @jax.custom_vjp
    def f(x, gate, y):
        return fwd_call(x, gate, y)

    def fwd(x, gate, y):
        return f(x, gate, y), (gate, y)

    def bwd(res, do):
        gate, y = res
        dy, dg = bwd_call(do, gate, y)
        return do, dg.reshape(gate.shape), dy

    f.defvjp(fwd, bwd)
    return f


_NT = (((1,), (1,)), ((), ()))
_NN = (((1,), (0,)), ((), ()))
_TN = (((0,), (0,)), ((), ()))
ATT_TQ = 2048
ATT_BTQ = 1024
ATT_SUB = 256
ATT_TK = 2816
LOG2E = 1.4426950408889634


def _per_head(flat, rows, width, where):
    if flat:
        return pl.BlockSpec((rows, width), where)
    return pl.BlockSpec((1, rows, width), lambda *g: (where(*g)[1], where(*g)[0], 0))


def _tile_of(ref):
    return ref if len(ref.shape) == 2 else ref.at[0]


def _head_dims(z, heads):
    if z.ndim == 2:
        return True, z.shape[0], z.shape[1] // heads
    return False, z.shape[1], z.shape[2]


def _attn_fwd_call(q, k, v, heads, scale, name):
    h, hkv = heads
    fq, sq, dq = _head_dims(q, h)
    fk, t, _ = _head_dims(k, hkv)
    dv = v.shape[1] // hkv
    dvx = dv + LANES
    grp = h // hkv
    tq, tk = _tile(sq, ATT_TQ), _tile(t, ATT_TK)
    sub = min(ATT_SUB, tq)
    nk = t // tk
    c = scale * LOG2E

    def body(q_ref, k_ref, v_ref, o_ref, lse_ref, m_s, acc_s):
        j = pl.program_id(2)

        @pl.when(j == 0)
        def _():
            m_s[...] = jnp.full_like(m_s, -jnp.inf)
            acc_s[...] = jnp.zeros_like(acc_s)

        q_t, kv = _tile_of(q_ref), _tile_of(k_ref)[...]
        one = (lax.broadcasted_iota(jnp.int32, (tk, LANES), 1) == 0).astype(BF16)
        vv = jnp.concatenate([v_ref[...], one], axis=1)
        for r in range(tq // sub):
            rows = pl.ds(r * sub, sub)
            s = lax.dot_general(q_t[rows, :], kv, _NT, preferred_element_type=F32) * c
            m_prev = m_s[rows, :]
            m_new = jnp.maximum(m_prev, jnp.max(s, axis=-1, keepdims=True))
            p = jnp.exp2(s - m_new)
            acc_s[rows, :] = jnp.exp2(m_prev - m_new) * acc_s[rows, :] + lax.dot_general(
                p.astype(BF16), vv, _NN, preferred_element_type=F32)
            m_s[rows, :] = m_new

        @pl.when(j == nk - 1)
        def _():
            acc = acc_s[...]
            l = acc[:, dv:dv + 1]
            o_ref[...] = (acc[:, :dv] / l).astype(o_ref.dtype)
            lse_ref[0] = m_s[...] + jnp.log(l) * LOG2E

    return _pc(
        body, name=name, grid=(h, sq // tq, nk),
        in_specs=[_per_head(fq, tq, dq, lambda hh, i, j: (i, hh)),
                  _per_head(fk, tk, dq, lambda hh, i, j: (j, hh // grp)),
                  pl.BlockSpec((tk, dv), lambda hh, i, j: (j, hh // grp))],
        out_specs=[pl.BlockSpec((tq, dv), lambda hh, i, j: (i, hh)),
                   pl.BlockSpec((1, tq, 1), lambda hh, i, j: (hh, i, 0))],
        out_shape=[jax.ShapeDtypeStruct((sq, h * dv), BF16), jax.ShapeDtypeStruct((h, sq, 1), F32)],
        scratch_shapes=[pltpu.VMEM((tq, 1), F32), pltpu.VMEM((tq, dvx), F32)],
        compiler_params=_cp("parallel", "parallel", "arbitrary"),
    )(q, k, v)


def _attn_bwd_call(q, k, v, o, do, lse, heads, scale, name):
    h, hkv = heads
    fq, sq, dq = _head_dims(q, h)
    fk, t, _ = _head_dims(k, hkv)
    dv = v.shape[1] // hkv
    grp = h // hkv
    tq, tk = _tile(sq, ATT_BTQ), _tile(t, ATT_TK)
    sub = min(ATT_SUB, tq)
    nq = sq // tq
    c = scale * LOG2E

    def body(q_ref, k_ref, v_ref, o_ref, do_ref, lse_ref, dq_ref, dk_ref, dv_ref, dk_s, dv_s):
        j = pl.program_id(1)
        i = pl.program_id(2)

        @pl.when(i == 0)
        def _():
            dk_s[...] = jnp.zeros_like(dk_s)
            dv_s[...] = jnp.zeros_like(dv_s)

        q_t, dq_t = _tile_of(q_ref), _tile_of(dq_ref)
        kv, vv = _tile_of(k_ref)[...], v_ref[...]
        for r in range(tq // sub):
            rows = pl.ds(r * sub, sub)
            qv, dov = q_t[rows, :], do_ref[rows, :]
            s = lax.dot_general(qv, kv, _NT, preferred_element_type=F32) * c
            p = jnp.exp2(s - lse_ref[0, rows, :])
            delta = jnp.sum(dov.astype(F32) * o_ref[rows, :].astype(F32), axis=-1, keepdims=True)
            dv_s[...] += lax.dot_general(p.astype(BF16), dov, _TN, preferred_element_type=F32)
            dp = lax.dot_general(dov, vv, _NT, preferred_element_type=F32)
            ds = (p * (dp - delta) * scale).astype(BF16)
            dk_s[...] += lax.dot_general(ds, qv, _TN, preferred_element_type=F32)
            dq_blk = lax.dot_general(ds, kv, _NN, preferred_element_type=F32)
            out_rows = pl.ds(pl.multiple_of(i * tq + r * sub, sub), sub)

            @pl.when(j == 0)
            def _():
                dq_t[out_rows, :] = dq_blk

            @pl.when(j > 0)
            def _():
                dq_t[out_rows, :] += dq_blk

        @pl.when(i == nq - 1)
        def _():
            _tile_of(dk_ref)[...] = dk_s[...]
            dv_ref[...] = dv_s[...]

    def shape_like(flat, rows, width):
        return (rows, h * width) if flat else (h, rows, width)

    return _pc(
        body, name=name, grid=(h, t // tk, nq),
        in_specs=[_per_head(fq, tq, dq, lambda hh, j, i: (i, hh)),
                  _per_head(fk, tk, dq, lambda hh, j, i: (j, hh // grp)),
                  pl.BlockSpec((tk, dv), lambda hh, j, i: (j, hh // grp)),
                  pl.BlockSpec((tq, dv), lambda hh, j, i: (i, hh)),
                  pl.BlockSpec((tq, dv), lambda hh, j, i: (i, hh)),
                  pl.BlockSpec((1, tq, 1), lambda hh, j, i: (hh, i, 0))],
        out_specs=[_per_head(fq, sq, dq, lambda hh, j, i: (0, hh)),
                   _per_head(fk, tk, dq, lambda hh, j, i: (j, hh)),
                   pl.BlockSpec((tk, dv), lambda hh, j, i: (j, hh))],
        out_shape=[jax.ShapeDtypeStruct(shape_like(fq, sq, dq), F32), jax.ShapeDtypeStruct(shape_like(fk, t, dq), F32),
                   jax.ShapeDtypeStruct((t, h * dv), F32)],
        scratch_shapes=[pltpu.VMEM((tk, dq), F32), pltpu.VMEM((tk, dv), F32)],
        compiler_params=_cp("parallel", "arbitrary", "arbitrary"),
    )(q, k, v, o, do, lse)


def _attention(name, scale):
    def per_head(z):
        rows, hh, d = z.shape
        return z.reshape(rows, hh * d).astype(BF16) if d % LANES == 0 else z.transpose(1, 0, 2).astype(BF16)

    def rows_first(g, hh):
        return g.reshape(g.shape[0], hh, -1) if g.ndim == 2 else g.transpose(1, 0, 2)

    @jax.custom_vjp
    def f(q, k, v):
        return fwd(q, k, v)[0]

    def fwd(q, k, v):
        heads = (q.shape[1], k.shape[1])
        qx, kx, vx = per_head(q), per_head(k), v.reshape(v.shape[0], -1).astype(BF16)
        o, lse = _attn_fwd_call(qx, kx, vx, heads, scale, name + "_fwd")
        like = tuple(jnp.zeros((0,), z.dtype) for z in (q, k, v))
        return o, (qx, kx, vx, o, lse, like)

    def bwd(res, do):
        qx, kx, vx, o, lse, like = res
        h, t = lse.shape[0], vx.shape[0]
        hkv = vx.shape[1] // (o.shape[1] // h)
        dq, dk, dvv = _attn_bwd_call(qx, kx, vx, o, do, lse, (h, hkv), scale, name + "_bwd")
        dq, dk, dvv = rows_first(dq, h), rows_first(dk, h), rows_first(dvv, h)
        if h != hkv:
            dk = dk.reshape(t, hkv, h // hkv, -1).sum(axis=2)
            dvv = dvv.reshape(t, hkv, h // hkv, -1).sum(axis=2)
        return tuple(d.astype(z.dtype) for d, z in zip((dq, dk, dvv), like))

    f.defvjp(fwd, bwd)
    return f


CONV_TF = 128
CONV_ROWS = 256


def _conv_neighbours(ref, c, r0, cur, row, nchunks, s, rc):
    halo = SUBLANES * (4 // ref.dtype.itemsize)
    prev = ref[pl.ds(pl.multiple_of(jnp.maximum(r0 - halo, 0), halo), halo), :].astype(F32)
    nxt = ref[pl.ds(pl.multiple_of(jnp.minimum(r0 + rc, s - halo), halo), halo), :].astype(F32)
    prow = jnp.where(c > 0, prev[halo - 1:halo, :], 0.0)
    nrow = jnp.where(c < nchunks - 1, nxt[0:1, :], 0.0)
    before = jnp.where(row == 0, prow, pltpu.roll(cur, 1, 0))
    after = jnp.where(row == rc - 1, nrow, pltpu.roll(cur, rc - 1, 0))
    return before, after


def _conv_fwd_call(u, w, b, name):
    s, f2 = u.shape
    f = f2 // 2
    tf = CONV_TF
    nf = f // tf
    rc = min(CONV_ROWS, s)
    nchunks = s // rc

    def body(g_ref, v_ref, w_ref, b_ref, o_ref):
        w0, w1, w2, bv = w_ref[0:1, :], w_ref[1:2, :], w_ref[2:3, :], b_ref[...]
        row = lax.broadcasted_iota(jnp.int32, (rc, tf), 0)

        def chunk(c, carry):
            r0 = pl.multiple_of(c * rc, rc)
            cur = g_ref[pl.ds(r0, rc), :].astype(F32)
            before, after = _conv_neighbours(g_ref, c, r0, cur, row, nchunks, s, rc)
            gc = before * w0 + cur * w1 + after * w2 + bv
            val = v_ref[pl.ds(r0, rc), :].astype(F32)
            o_ref[pl.ds(r0, rc), :] = (gc * jax.nn.sigmoid(gc) * val).astype(o_ref.dtype)
            return carry

        lax.fori_loop(0, nchunks, chunk, 0)

    return _pc(
        body, name=name, grid=(nf,),
        in_specs=[pl.BlockSpec((s, tf), lambda j: (0, j)), pl.BlockSpec((s, tf), lambda j: (0, j + nf)),
                  pl.BlockSpec((3, tf), lambda j: (0, j)), pl.BlockSpec((1, tf), lambda j: (0, j))],
        out_specs=pl.BlockSpec((s, tf), lambda j: (0, j)),
        out_shape=jax.ShapeDtypeStruct((s, f), u.dtype),
        compiler_params=_cp("parallel"),
    )(u, u, w, b.reshape(1, f))


def _conv_bwd_call(u, w, b, da, name):
    s, f2 = u.shape
    f = f2 // 2
    tf = CONV_TF
    nf = f // tf
    rc = min(CONV_ROWS, s)
    nchunks = s // rc

    def body(g_ref, v_ref, da_ref, w_ref, b_ref, dg_ref, dv_ref, dw_ref, db_ref, dgc_s):
        w0, w1, w2, bv = w_ref[0:1, :], w_ref[1:2, :], w_ref[2:3, :], b_ref[...]
        row = lax.broadcasted_iota(jnp.int32, (rc, tf), 0)

        def chunk1(c, carry):
            a0, a1, a2, ab = carry
            r0 = pl.multiple_of(c * rc, rc)
            cur = g_ref[pl.ds(r0, rc), :].astype(F32)
            before, after = _conv_neighbours(g_ref, c, r0, cur, row, nchunks, s, rc)
            gc = before * w0 + cur * w1 + after * w2 + bv
            sig = jax.nn.sigmoid(gc)
            dav = da_ref[pl.ds(r0, rc), :].astype(F32)
            dv_ref[pl.ds(r0, rc), :] = (dav * (gc * sig)).astype(dv_ref.dtype)
            dgc = dav * v_ref[pl.ds(r0, rc), :].astype(F32) * (sig * (1.0 + gc * (1.0 - sig)))
            dgc_s[pl.ds(r0, rc), :] = dgc
            return (a0 + jnp.sum(dgc * before, axis=0, keepdims=True),
                    a1 + jnp.sum(dgc * cur, axis=0, keepdims=True),
                    a2 + jnp.sum(dgc * after, axis=0, keepdims=True),
                    ab + jnp.sum(dgc, axis=0, keepdims=True))

        z = jnp.zeros((1, tf), F32)
        a0, a1, a2, ab = lax.fori_loop(0, nchunks, chunk1, (z, z, z, z))
        dw_ref[0:1, :] = a0
        dw_ref[1:2, :] = a1
        dw_ref[2:3, :] = a2
        db_ref[...] = ab

        def chunk2(c, carry):
            r0 = pl.multiple_of(c * rc, rc)
            cur = dgc_s[pl.ds(r0, rc), :]
            before, after = _conv_neighbours(dgc_s, c, r0, cur, row, nchunks, s, rc)
            dg_ref[pl.ds(r0, rc), :] = (after * w0 + cur * w1 + before * w2).astype(dg_ref.dtype)
            return carry

        lax.fori_loop(0, nchunks, chunk2, 0)

    col = pl.BlockSpec((s, tf), lambda j: (0, j))
    return _pc(
        body, name=name, grid=(nf,),
        in_specs=[col, pl.BlockSpec((s, tf), lambda j: (0, j + nf)), col,
                  pl.BlockSpec((3, tf), lambda j: (0, j)), pl.BlockSpec((1, tf), lambda j: (0, j))],
        out_specs=[col, col, pl.BlockSpec((3, tf), lambda j: (0, j)), pl.BlockSpec((1, tf), lambda j: (0, j))],
        out_shape=[jax.ShapeDtypeStruct((s, f), u.dtype), jax.ShapeDtypeStruct((s, f), u.dtype),
                   jax.ShapeDtypeStruct((3, f), F32), jax.ShapeDtypeStruct((1, f), F32)],
        scratch_shapes=[pltpu.VMEM((s, tf), F32)],
        compiler_params=_cp("parallel"),
    )(u, u, da, w, b.reshape(1, f))


def _convgate(name):
    @jax.custom_vjp
    def f(u, w, b):
        return _conv_fwd_call(u, w, b, name + "_fwd")

    def fwd(u, w, b):
        return f(u, w, b), (u, w, b)

    def bwd(res, da):
        u, w, b = res
        dg, dv, dw, db = _conv_bwd_call(u, w, b, da, name + "_bwd")
        return jnp.concatenate([dg, dv], axis=-1), dw, db.reshape(b.shape)

    f.defvjp(fwd, bwd)
    return f


def _loss_call(y, tgt):
    r, d = y.shape
    tr = _row_tile(r, d)

    def body(y_ref, t_ref, dy_ref, part_ref):
        @pl.when(pl.program_id(0) == 0)
        def _():
            part_ref[...] = jnp.zeros_like(part_ref)

        diff = y_ref[...] - t_ref[...]
        dy_ref[...] = diff / d
        part_ref[...] += jnp.sum(diff * diff, axis=0, keepdims=True)

    blk = pl.BlockSpec((tr, d), lambda i: (i, 0))
    vec = pl.BlockSpec((1, d), lambda i: (0, 0))
    dy, part = _pc(
        body, name="loss_head", grid=(r // tr,), in_specs=[blk, blk], out_specs=[blk, vec],
        out_shape=[jax.ShapeDtypeStruct((r, d), F32), jax.ShapeDtypeStruct((1, d), F32)],
        compiler_params=_cp("arbitrary"),
    )(y, tgt)
    return 0.5 * (jnp.sum(part) / d), dy


def _adamw_call(w, m, v, g, name, summed):
    r, c = w.shape
    tr = _tile(r, max(2 * SUBLANES, (1 << 18) // c), 2 * SUBLANES)

    def body(g_ref, w_ref, m_ref, v_ref, go_ref, d_ref, mo_ref, vo_ref):
        if summed:
            gv = g_ref[0].astype(F32)
            for dev in range(1, N_DEV):
                gv = gv + g_ref[dev].astype(F32)
        else:
            gv = g_ref[...]
        mn = ADAM_B1 * m_ref[...] + (1.0 - ADAM_B1) * gv
        vn = ADAM_B2 * v_ref[...] + (1.0 - ADAM_B2) * (gv * gv)
        m_hat = mn / (1.0 - ADAM_B1 ** ADAM_STEP)
        v_hat = vn / (1.0 - ADAM_B2 ** ADAM_STEP)
        go_ref[...] = gv
        d_ref[...] = -ADAM_LR * (m_hat / (jnp.sqrt(v_hat) + ADAM_EPS) + ADAM_WD * w_ref[...])
        mo_ref[...] = mn
        vo_ref[...] = vn

    blk = pl.BlockSpec((tr, c), lambda i: (i, 0))
    gblk = pl.BlockSpec((N_DEV, tr, c), lambda i: (0, i, 0)) if summed else blk
    return _pc(
        body, name=name, grid=(r // tr,), in_specs=[gblk, blk, blk, blk], out_specs=[blk] * 4,
        out_shape=[jax.ShapeDtypeStruct((r, c), F32)] * 4, compiler_params=_cp("parallel"),
    )(g, w, m, v)


_ANY = pl.BlockSpec(memory_space=pl.ANY)
MESH = pl.DeviceIdType.MESH


def _window(ref, axis, idx, size):
    if axis == 0:
        return ref.at[pl.ds(idx * size, size), :]
    return ref.at[:, pl.ds(idx * size, size)]


def _allgather(shards, axes, name):
    n_p = len(shards)
    layers = [s.shape[0] for s in shards]
    out_shape = []
    for s, ax in zip(shards, axes):
        _, k, n = s.shape
        full = (k * N_DEV, n) if ax == 0 else (k, n * N_DEV)
        out_shape += [jax.ShapeDtypeStruct(full, s.dtype)] * s.shape[0]
    n_out = len(out_shape)

    def body(*refs):
        x_refs = refs[:n_p]
        flat = refs[n_p:n_p + n_out]
        send_sems, recv_sems, local_sems = refs[n_p + n_out:]
        outs, at = [], 0
        for cnt in layers:
            outs.append(flat[at:at + cnt])
            at += cnt
        x, y, c = lax.axis_index("x"), lax.axis_index("y"), lax.axis_index("c")
        me, sibling = (x, y, c), (x, y, 1 - c)
        chips = [(1 - x, y), (x, 1 - y), (1 - x, 1 - y)]

        def lin(d):
            return 4 * d[0] + 2 * d[1] + d[2]

        def copies(p, sem, block, to, from_shard):
            size = shards[p].shape[1 + axes[p]]
            res = []
            for l in range(layers[p]):
                dst = _window(outs[p][l], axes[p], lin(block), size)
                res.append(pltpu.make_async_remote_copy(
                    src_ref=x_refs[p].at[l] if from_shard else dst, dst_ref=dst,
                    send_sem=send_sems.at[p, sem], recv_sem=recv_sems.at[p, sem], device_id=to, device_id_type=MESH))
            return res

        def drained(p, sem):
            return pltpu.make_async_remote_copy(
                src_ref=x_refs[p], dst_ref=x_refs[p], send_sem=send_sems.at[p, sem], recv_sem=recv_sems.at[p, sem],
                device_id=me, device_id_type=MESH)

        for p in range(n_p):
            size = shards[p].shape[1 + axes[p]]
            for l in range(layers[p]):
                pltpu.make_async_copy(x_refs[p].at[l], _window(outs[p][l], axes[p], lin(me), size), local_sems.at[p]).start()
            for cp in copies(p, 0, me, sibling, True):
                cp.start()
            for j, chip in enumerate(chips):
                for cp in copies(p, 1 + j, me, (*chip, c), True):
                    cp.start()
        for p in range(n_p):
            for j, chip in enumerate(chips):
                drained(p, 1 + j).wait_recv()
                for cp in copies(p, 4 + j, (*chip, c), sibling, False):
                    cp.start()
        for p in range(n_p):
            for sem in (0, 4, 5, 6):
                drained(p, sem).wait_recv()
            for sem in range(7):
                drained(p, sem).wait_send()
            pltpu.make_async_copy(x_refs[p], x_refs[p], local_sems.at[p]).wait()

    flat = _pc(
        body, name=name, in_specs=[_ANY] * n_p, out_specs=[_ANY] * n_out, out_shape=out_shape,
        scratch_shapes=[pltpu.SemaphoreType.DMA((n_p, 7)), pltpu.SemaphoreType.DMA((n_p, 7)),
                        pltpu.SemaphoreType.DMA((n_p,))],
    )(*shards)
    res, at = [], 0
    for cnt in layers:
        res.append(list(flat[at:at + cnt]))
        at += cnt
    return res


def _alltoall(grads, axes, name):
    n_p = len(grads)
    layers = [len(g) for g in grads]
    n_in = sum(layers)
    blocks = []
    for g, ax in zip(grads, axes):
        kk, nn = g[0].shape
        blocks.append((kk // N_DEV, nn) if ax == 0 else (kk, nn // N_DEV))
    out_shape = [jax.ShapeDtypeStruct((N_DEV, cnt) + blk, g[0].dtype) for g, cnt, blk in zip(grads, layers, blocks)]

    def body(*refs):
        flat = refs[:n_in]
        lands = refs[n_in:n_in + n_p]
        send_sems, recv_sems, local_sems = refs[n_in + n_p:]
        g_refs, at = [], 0
        for cnt in layers:
            g_refs.append(flat[at:at + cnt])
            at += cnt
        x, y, c = lax.axis_index("x"), lax.axis_index("y"), lax.axis_index("c")
        me = 4 * x + 2 * y + c
        for p in range(n_p):
            size = blocks[p][axes[p]]
            for l in range(layers[p]):
                pltpu.make_async_copy(_window(g_refs[p][l], axes[p], me, size), lands[p].at[me, l], local_sems.at[p]).start()
            for rel in range(1, N_DEV):
                px = 1 - x if rel & 4 else x
                py = 1 - y if rel & 2 else y
                pc = 1 - c if rel & 1 else c
                for l in range(layers[p]):
                    pltpu.make_async_remote_copy(
                        src_ref=_window(g_refs[p][l], axes[p], 4 * px + 2 * py + pc, size), dst_ref=lands[p].at[me, l],
                        send_sem=send_sems.at[p, rel - 1], recv_sem=recv_sems.at[p, rel - 1],
                        device_id=(px, py, pc), device_id_type=MESH).start()
        for p in range(n_p):
            slab = lands[p].at[0]
            for rel in range(1, N_DEV):
                pltpu.make_async_remote_copy(
                    src_ref=slab, dst_ref=slab, send_sem=send_sems.at[p, rel - 1], recv_sem=recv_sems.at[p, rel - 1],
                    device_id=(x, y, c), device_id_type=MESH).wait()
            pltpu.make_async_copy(slab, slab, local_sems.at[p]).wait()

    return _pc(
        body, name=name, in_specs=[_ANY] * n_in, out_specs=[_ANY] * n_p, out_shape=out_shape,
        scratch_shapes=[pltpu.SemaphoreType.DMA((n_p, 7)), pltpu.SemaphoreType.DMA((n_p, 7)),
                        pltpu.SemaphoreType.DMA((n_p,))],
    )(*[g for gl in grads for g in gl])


def _gather_rows(a, name):
    r, c = a.shape
    return _allgather([a[None]], [0], name)[0][0].reshape(N_DEV, r, c)


def _conditioning(me):
    def forward(s16, w, b):
        nl, _, cols = w.shape
        part = jnp.concatenate([_matmul(s16, w[i], name="mod_fwd") for i in range(nl)], axis=0)
        full = _gather_rows(part, "mod_gather").reshape(N_DEV, nl, MOD_ROWS, cols)
        return full.transpose(1, 2, 0, 3).reshape(nl, MOD_ROWS, N_DEV * cols) + b[:, None, :]

    @jax.custom_vjp
    def f(s16, w, b):
        return forward(s16, w, b)

    def fwd(s16, w, b):
        return forward(s16, w, b), (s16, w)

    def bwd(res, dm):
        s16, w = res
        nl, dmodel, cols = w.shape
        width = dm.shape[-1]
        mine = lax.dynamic_slice_in_dim(dm, me, 1, axis=1)
        both = jnp.concatenate([mine, dm[:, N_DEV:N_DEV + 1]], axis=1).reshape(nl * 2, width)
        allrows = _gather_rows(both, "dmod_gather").reshape(N_DEV, nl, 2, width)
        total = jnp.concatenate([allrows[:, :, 0].transpose(1, 0, 2), jnp.sum(allrows[:, :, 1], axis=0)[:, None, :],
                                 jnp.zeros((nl, MOD_ROWS - N_DEV - 1, width), F32)], axis=1)
        db = jnp.sum(total, axis=1)
        my_cols = lax.dynamic_slice_in_dim(total, me * cols, cols, axis=2)
        dw = jnp.stack([_matmul(s16, my_cols[i], ta=True, name="mod_dw") for i in range(nl)])
        ds_part = _matmul(my_cols[0], w[0], tb=True, name="mod_ds")
        for i in range(1, nl):
            ds_part = ds_part + _matmul(my_cols[i], w[i], tb=True, name="mod_ds")
        ds = jnp.sum(_gather_rows(ds_part, "dcond_gather"), axis=0)
        return ds, dw, db

    f.defvjp(fwd, bwd)
    return f


def _rope_tables(s, rot_dim):
    t = jnp.arange(s, dtype=jnp.int32)
    rows, cols = t // GRID_W, t % GRID_W
    axis_dim = rot_dim // 2
    inv = jnp.power(ROPE_BASE, -jnp.arange(0, axis_dim, 2, dtype=F32) / axis_dim)
    ang_r = rows.astype(F32)[:, None] * inv
    ang_c = cols.astype(F32)[:, None] * inv
    ang = jnp.concatenate([ang_r, ang_r, ang_c, ang_c], axis=-1)
    return jnp.cos(ang), jnp.sin(ang)


def _rotate_half(z):
    z1, z2 = jnp.split(z, 2, axis=-1)
    return jnp.concatenate([-z2, z1], axis=-1)


def _rope(z, tables):
    if tables is None:
        return z
    cos, sin = tables
    half = z.shape[-1] // 2
    rot = jnp.concatenate([_rotate_half(z[..., :half]), _rotate_half(z[..., half:])], axis=-1)
    return (z * cos[:, None, :] + rot * sin[:, None, :]).astype(z.dtype)


def _head_norm(name, z, g):
    s, h, d = z.shape
    return _norm(name)(z.reshape(s * h, d), g).reshape(s, h, d)


def _mla_q(tag, hx, rope, w):
    s = hx.shape[0]
    cq = _norm(tag + "_cq_norm")(_mm(tag + "_dq")(hx, w["mla_w_dq"]), w["mla_g_dq"])
    q = _mm(tag + "_uq")(cq, w["mla_w_uq"]).reshape(s, -1, MLA_NOPE + MLA_ROPE)
    q_nope = _head_norm(tag + "_qn_norm", q[..., :MLA_NOPE], w["mla_g_q_nope"])
    q_pe = _rope(_head_norm(tag + "_qp_norm", q[..., MLA_NOPE:], w["mla_g_q_pe"]), rope)
    return jnp.concatenate([q_nope, q_pe], axis=-1)


def _mla_kv(tag, hx, rope, w):
    s = hx.shape[0]
    kv_a = _mm(tag + "_dkv")(hx, w["mla_w_dkv"])
    rank = kv_a.shape[-1] - MLA_ROPE
    c_kv = _norm(tag + "_ckv_norm")(kv_a[:, :rank], w["mla_g_dkv"])
    k_pe = _rope(_norm(tag + "_kp_norm")(kv_a[:, rank:], w["mla_g_k_pe"])[:, None, :], rope)
    kv = _mm(tag + "_ukv")(c_kv, w["mla_w_ukv"]).reshape(s, -1, MLA_NOPE + MLA_V)
    heads = kv.shape[1]
    k_nope = _head_norm(tag + "_kn_norm", kv[..., :MLA_NOPE], w["mla_g_k_nope"])
    k = jnp.concatenate([k_nope, jnp.broadcast_to(k_pe, (s, heads, MLA_ROPE))], axis=-1)
    return k, kv[..., MLA_NOPE:]


def _gqa_q(tag, hx, rope, w):
    s = hx.shape[0]
    q = _mm(tag + "_q")(hx, w["gqa_w_q"]).reshape(s, -1, GQA_HEAD_DIM)
    return _rope(_head_norm(tag + "_q_norm", q, w["gqa_g_q"]), rope)


def _gqa_kv(tag, hx, rope, w):
    s = hx.shape[0]
    kv = _mm(tag + "_kv")(hx, w["gqa_w_kv"]).reshape(s, 2, GQA_KV_HEADS, GQA_HEAD_DIM)
    k = _rope(_head_norm(tag + "_k_norm", kv[:, 0], w["gqa_g_k"]), rope)
    return k, kv[:, 1]


def _conv_ffn(tag, hx, w):
    u = _mm(tag + "_up", BF16)(hx, w["ffn_w_up"])
    a = _convgate(tag + "_conv")(u, w["ffn_conv_w"], w["ffn_conv_b"])
    return _mm(tag + "_down")(a, w["ffn_w_down"])


def _forward(leaves, ctx, silu_c_all, me, depth):
    x = leaves["x"]
    s, d = x.shape
    rope_mla = _rope_tables(s, MLA_ROPE)
    rope_gqa = _rope_tables(s, GQA_HEAD_DIM)
    silu_cc = jax.nn.silu(leaves["c_ctx"])
    s16 = jnp.concatenate([silu_c_all, silu_cc[None, :], jnp.zeros((MOD_ROWS - N_DEV - 1, d), F32)], axis=0)
    mods = _conditioning(me)(s16, leaves["w_mod"], leaves["b_mod"])
    for i in range(depth):
        last = i == depth - 1
        w = {k: v[i // 2] for k, v in leaves["mixer"][i % 2].items()}
        w.update({k: v[i] for k, v in leaves["ffn"].items()})
        mod = lax.dynamic_index_in_dim(mods[i], me, axis=0, keepdims=False)
        sh1, sc1, g1, sh2, sc2, g2 = jnp.split(mod, N_MOD)
        csh1, csc1, cg1, csh2, csc2, cg2 = jnp.split(mods[i, N_DEV], N_MOD)
        tag = f"l{i}"
        x, hx = _modulate(tag + "_mix_mod")(x, leaves["norm_mix"][i], sh1, sc1)
        ctx, hc = _modulate(tag + "c_mix_mod")(ctx, leaves["norm_mix"][i], csh1, csc1)
        if i % 2 == 0:
            q_fn, kv_fn, w_o = _mla_q, _mla_kv, w["mla_w_o"]
            rope, scale = rope_mla, 1.0 / float(MLA_NOPE + MLA_ROPE) ** 0.5
        else:
            q_fn, kv_fn, w_o = _gqa_q, _gqa_kv, w["gqa_w_o"]
            rope, scale = rope_gqa, 1.0 / float(GQA_HEAD_DIM) ** 0.5
        k_lat, v_lat = kv_fn(tag, hx, rope, w)
        k_ctx, v_ctx = kv_fn(tag + "c", hc, None, w)
        o = _attention(tag + "_attn", scale)(q_fn(tag, hx, rope, w), jnp.concatenate([k_lat, k_ctx], axis=0),
                                             jnp.concatenate([v_lat, v_ctx], axis=0))
        x = _gres(tag + "_mix_res")(x, g1, _mm(tag + "_o")(o, w_o))
        if not last:
            oc = _attention(tag + "c_attn", scale)(q_fn(tag + "c", hc, None, w), k_ctx, v_ctx)
            ctx = _gres(tag + "c_mix_res")(ctx, cg1, _mm(tag + "c_o")(oc, w_o))
        x, hx = _modulate(tag + "_ffn_mod")(x, leaves["norm_ffn"][i], sh2, sc2)
        x = _gres(tag + "_ffn_res")(x, g2, _conv_ffn(tag, hx, w))
        if not last:
            ctx, hc = _modulate(tag + "c_ffn_mod")(ctx, leaves["norm_ffn"][i], csh2, csc2)
            ctx = _gres(tag + "c_ffn_res")(ctx, cg2, _conv_ffn(tag + "c", hc, w))
    return x


_MLA_BIG = {"mla_w_dq": 0, "mla_w_uq": 1, "mla_w_dkv": 0, "mla_w_ukv": 1, "mla_w_o": 0}
_GQA_BIG = {"gqa_w_q": 0, "gqa_w_kv": 0, "gqa_w_o": 0}
_FFN_BIG = {"ffn_w_up": 1, "ffn_w_down": 0}
_BIG_GROUPS = [("mla", _MLA_BIG), ("gqa", _GQA_BIG), ("ffn_up", {"ffn_w_up": 1}), ("ffn_down", {"ffn_w_down": 0})]
_MLA_GAINS = ["mla_g_dq", "mla_g_q_nope", "mla_g_q_pe", "mla_g_dkv", "mla_g_k_pe", "mla_g_k_nope"]
_GQA_GAINS = ["gqa_g_q", "gqa_g_k"]
_SHARED = ["norm_mix", "norm_ffn"] + _MLA_GAINS + _GQA_GAINS + ["ffn_conv_b"]
_SUMMED = ["c_ctx", "b_mod"]

_NAMES = ["c_ctx", "w_mod", "b_mod", "norm_mix", "norm_ffn", "mla_w_dq", "mla_g_dq", "mla_w_uq", "mla_g_q_nope",
          "mla_g_q_pe", "mla_w_dkv", "mla_g_dkv", "mla_g_k_pe", "mla_w_ukv", "mla_g_k_nope", "mla_w_o", "gqa_w_q",
          "gqa_g_q", "gqa_w_kv", "gqa_g_k", "gqa_w_o", "ffn_w_up", "ffn_conv_w", "ffn_conv_b", "ffn_w_down"]


def _rows128(a):
    return a.reshape(-1, LANES)


def kernel(x, c, ctx, c_ctx, w_mod, b_mod, norm_mix, norm_ffn, mla_w_dq, mla_g_dq, mla_w_uq, mla_g_q_nope, mla_g_q_pe, mla_w_dkv, mla_g_dkv, mla_g_k_pe, mla_w_ukv, mla_g_k_nope, mla_w_o, gqa_w_q, gqa_g_q, gqa_w_kv, gqa_g_k, gqa_w_o, ffn_w_up, ffn_conv_w, ffn_conv_b, ffn_w_down, loss_target, m_c_ctx, m_w_mod, m_b_mod, m_norm_mix, m_norm_ffn, m_mla_w_dq, m_mla_g_dq, m_mla_w_uq, m_mla_g_q_nope, m_mla_g_q_pe, m_mla_w_dkv, m_mla_g_dkv, m_mla_g_k_pe, m_mla_w_ukv, m_mla_g_k_nope, m_mla_w_o, m_gqa_w_q, m_gqa_g_q, m_gqa_w_kv, m_gqa_g_k, m_gqa_w_o, m_ffn_w_up, m_ffn_conv_w, m_ffn_conv_b, m_ffn_w_down, v_c_ctx, v_w_mod, v_b_mod, v_norm_mix, v_norm_ffn, v_mla_w_dq, v_mla_g_dq, v_mla_w_uq, v_mla_g_q_nope, v_mla_g_q_pe, v_mla_w_dkv, v_mla_g_dkv, v_mla_g_k_pe, v_mla_w_ukv, v_mla_g_k_nope, v_mla_w_o, v_gqa_w_q, v_gqa_g_q, v_gqa_w_kv, v_gqa_g_k, v_gqa_w_o, v_ffn_w_up, v_ffn_conv_w, v_ffn_conv_b, v_ffn_w_down):
    args = locals()
    wts = {n: args[n] for n in _NAMES}
    mom = {n: args["m_" + n] for n in _NAMES}
    var = {n: args["v_" + n] for n in _NAMES}
    me = 4 * lax.axis_index("x") + 2 * lax.axis_index("y") + lax.axis_index("c")
    depth = w_mod.shape[0]
    d_model = x.shape[-1]
    d_ff = ffn_conv_b.shape[-1]

    taps = ffn_conv_w.reshape(-1)
    packed = jnp.concatenate([jax.nn.silu(c).reshape(-1), taps])
    packed = jnp.concatenate([packed, jnp.zeros((-packed.size % (SUBLANES * LANES),), F32)])
    got = _gather_rows(_rows128(packed), "cond_gather").reshape(N_DEV, -1)
    silu_c_all = got[:, :d_model]
    conv_w_full = got[:, d_model:d_model + taps.size].reshape(N_DEV, depth, 3, -1)
    conv_w_full = conv_w_full.transpose(1, 2, 0, 3).reshape(depth, 3, d_ff)

    full = {}
    for gname, group in _BIG_GROUPS:
        names = list(group)
        got_w = _allgather([wts[n].astype(BF16) for n in names], [group[n] for n in names], "gather_" + gname)
        full.update(dict(zip(names, got_w)))

    leaves = {
        "x": x[0], "c_ctx": c_ctx, "w_mod": w_mod, "b_mod": b_mod, "norm_mix": norm_mix, "norm_ffn": norm_ffn,
        "mixer": [
            {**{n: full[n] for n in _MLA_BIG}, **{n: wts[n] for n in _MLA_GAINS}},
            {**{n: full[n] for n in _GQA_BIG}, **{n: wts[n] for n in _GQA_GAINS}},
        ],
        "ffn": {"ffn_w_up": full["ffn_w_up"], "ffn_w_down": full["ffn_w_down"], "ffn_conv_w": conv_w_full,
                "ffn_conv_b": ffn_conv_b},
    }
    y, pullback = jax.vjp(lambda lv: _forward(lv, ctx[0], silu_c_all, me, depth), leaves)
    loss_part, dy = _loss_call(y, loss_target[0])
    (gl,) = pullback(dy)
    loss = lax.psum(loss_part, ("x", "y", "c"))

    grads, deltas, new_m, new_v = {}, {}, {}, {}

    def put(n, outs, shape):
        grads[n], deltas[n], new_m[n], new_v[n] = (o.reshape(shape) for o in outs)

    gfull = {**{n: gl["mixer"][0][n] for n in _MLA_BIG}, **{n: gl["mixer"][1][n] for n in _GQA_BIG},
             "ffn_w_up": gl["ffn"]["ffn_w_up"], "ffn_w_down": gl["ffn"]["ffn_w_down"]}
    for gname, group in _BIG_GROUPS:
        names = list(group)
        lands = _alltoall([gfull[n] for n in names], [group[n] for n in names], "exchange_" + gname)
        for n, land in zip(names, lands):
            shape = wts[n].shape
            cols = shape[-1]
            outs = _adamw_call(wts[n].reshape(-1, cols), mom[n].reshape(-1, cols), var[n].reshape(-1, cols),
                               land.reshape(N_DEV, -1, cols), "adamw_" + n, True)
            put(n, outs, shape)

    cols = w_mod.shape[-1]
    outs = _adamw_call(w_mod.reshape(-1, cols), m_w_mod.reshape(-1, cols), v_w_mod.reshape(-1, cols),
                       gl["w_mod"].reshape(-1, cols), "adamw_w_mod", False)
    put("w_mod", outs, w_mod.shape)

    share = {"norm_mix": gl["norm_mix"], "norm_ffn": gl["norm_ffn"], "ffn_conv_b": gl["ffn"]["ffn_conv_b"]}
    share.update({n: gl["mixer"][0][n] for n in _MLA_GAINS})
    share.update({n: gl["mixer"][1][n] for n in _GQA_GAINS})
    whole = {n: jnp.where(me == 0, gl[n], 0.0) for n in _SUMMED}
    order = _SUMMED + _SHARED
    parts = [whole[n] if n in whole else share[n] for n in order]
    sizes = [p.size for p in parts]
    taps_g = gl["ffn"]["ffn_conv_w"]
    pack_g = jnp.concatenate([p.reshape(-1) for p in parts] + [taps_g.reshape(-1)])
    pad = (-pack_g.size // LANES) % SUBLANES * LANES
    pack_g = jnp.concatenate([pack_g, jnp.zeros((pad,), F32)])
    land = _gather_rows(_rows128(pack_g), "smallgrad_gather")

    def pack(src):
        flat = jnp.concatenate([src[n].reshape(-1) for n in order] + [jnp.zeros((taps_g.size + pad,), F32)])
        return _rows128(flat)

    outs = _adamw_call(pack(wts), pack(mom), pack(var), land, "adamw_small", True)
    at = 0
    for n, size in zip(order, sizes):
        put(n, [o.reshape(-1)[at:at + size] for o in outs], wts[n].shape)
        at += size
    taps_sum = outs[0].reshape(-1)[at:at + taps_g.size].reshape(taps_g.shape)
    my_taps = lax.dynamic_slice_in_dim(taps_sum, me * ffn_conv_w.shape[-1], ffn_conv_w.shape[-1], axis=2)
    outs = _adamw_call(_rows128(ffn_conv_w), _rows128(m_ffn_conv_w), _rows128(v_ffn_conv_w), _rows128(my_taps),
                       "adamw_conv_w", False)
    put("ffn_conv_w", outs, ffn_conv_w.shape)

    return (loss, gl["x"][None], *[grads[n] for n in _NAMES], *[deltas[n] for n in _NAMES],
            *[new_m[n] for n in _NAMES], *[new_v[n] for n in _NAMES])
```

```python
import functools

import jax
import jax.numpy as jnp
from jax import lax
from jax.experimental import pallas as pl
from jax.experimental.pallas import tpu as pltpu

F32 = jnp.float32
BF16 = jnp.bfloat16

EPS = 1e-6
ROPE_BASE = 10000.0
GRID_W = 64
N_MOD = 6
MLA_NOPE = 128
MLA_ROPE = 64
MLA_V = 128
GQA_HEAD_DIM = 128
GQA_KV_HEADS = 4
ADAM_LR = 0.001
ADAM_B1 = 0.9
ADAM_B2 = 0.999
ADAM_EPS = 1e-08
ADAM_WD = 0.01
ADAM_STEP = 10
N_DEV = 8
MOD_ROWS = 16

VMEM_LIMIT = 48 * 1024 * 1024
MATMUL_VMEM = 40 * 1024 * 1024
LANES = 128
SUBLANES = 8


def _pc(body, **kw):
    return pl.pallas_call(body, **kw)


def _cp(*sem):
    return pltpu.CompilerParams(dimension_semantics=sem if sem else None, vmem_limit_bytes=VMEM_LIMIT)


def _tile(n, cap, mult=LANES):
    if n <= cap:
        return n
    t = (cap // mult) * mult
    while t >= mult:
        if n % t == 0:
            return t
        t -= mult
    return n


def _matmul(a, b, *, ta=False, tb=False, name, out_dtype=F32):
    assert not (ta and tb)
    kd, m = a.shape if ta else a.shape[::-1]
    n, kb = b.shape if tb else b.shape[::-1]
    assert kd == kb, (a.shape, b.shape, ta, tb)
    tm, tn = _tile(m, 1024), _tile(n, 1024)
    out_bytes = jnp.dtype(out_dtype).itemsize

    def vmem_need(t):
        return 2 * t * (tm * a.dtype.itemsize + tn * b.dtype.itemsize) + tm * tn * (2 * out_bytes + 4)

    tk = next(t for t in (_tile(kd, 2048), _tile(kd, 1024), _tile(kd, 512)) if vmem_need(t) <= MATMUL_VMEM or t <= 512)
    nk = kd // tk
    if ta:
        dims = (((0,), (0,)), ((), ()))
        a_spec = pl.BlockSpec((tk, tm), lambda i, j, k: (k, i))
    else:
        dims = (((1,), (1 if tb else 0,)), ((), ()))
        a_spec = pl.BlockSpec((tm, tk), lambda i, j, k: (i, k))
    if tb:
        b_spec = pl.BlockSpec((tn, tk), lambda i, j, k: (j, k))
    else:
        b_spec = pl.BlockSpec((tk, tn), lambda i, j, k: (k, j))

    def product(a_ref, b_ref):
        return lax.dot_general(a_ref[...].astype(BF16), b_ref[...].astype(BF16), dims, preferred_element_type=F32)

    def body_one(a_ref, b_ref, o_ref):
        o_ref[...] = product(a_ref, b_ref).astype(o_ref.dtype)

    def body(a_ref, b_ref, o_ref, acc_ref):
        k = pl.program_id(2)

        @pl.when(k == 0)
        def _():
            acc_ref[...] = jnp.zeros_like(acc_ref)

        acc_ref[...] += product(a_ref, b_ref)

        @pl.when(k == nk - 1)
        def _():
            o_ref[...] = acc_ref[...].astype(o_ref.dtype)

    return _pc(
        body_one if nk == 1 else body, name=name, grid=(m // tm, n // tn, nk),
        in_specs=[a_spec, b_spec],
        out_specs=pl.BlockSpec((tm, tn), lambda i, j, k: (i, j)),
        out_shape=jax.ShapeDtypeStruct((m, n), out_dtype),
        scratch_shapes=[] if nk == 1 else [pltpu.VMEM((tm, tn), F32)],
        compiler_params=_cp("parallel", "parallel", "arbitrary"),
    )(a, b)


def _mm(name, out_dtype=F32):
    @jax.custom_vjp
    def f(a, w):
        return _matmul(a, w, name=name + "_fwd", out_dtype=out_dtype)

    def fwd(a, w):
        return f(a, w), (a, w)

    def bwd(res, dc):
        a, w = res
        return (_matmul(dc, w, tb=True, name=name + "_da", out_dtype=a.dtype),
                _matmul(a, dc, ta=True, name=name + "_dw", out_dtype=w.dtype))

    f.defvjp(fwd, bwd)
    return f


def _row_tile(r, d):
    return _tile(r, max(SUBLANES, min(4096, (2 << 20) // (4 * d))), SUBLANES)


def _norm_fwd_call(x, g, sc, sh, name, out_dtype=F32):
    r, d = x.shape
    tr = _row_tile(r, d)
    mod = sc is not None

    def body(*refs):
        if mod:
            x_ref, g_ref, sc_ref, sh_ref, y_ref = refs
        else:
            x_ref, g_ref, y_ref = refs
        xv = x_ref[...]
        y = xv * lax.rsqrt(jnp.mean(xv * xv, axis=-1, keepdims=True) + EPS) * g_ref[...]
        if mod:
            y = y * (1.0 + sc_ref[...]) + sh_ref[...]
        y_ref[...] = y.astype(y_ref.dtype)

    vec = pl.BlockSpec((1, d), lambda i: (0, 0))
    args = [x, g.reshape(1, d)] + ([sc.reshape(1, d), sh.reshape(1, d)] if mod else [])
    return _pc(
        body, name=name, grid=(r // tr,),
        in_specs=[pl.BlockSpec((tr, d), lambda i: (i, 0))] + [vec] * (len(args) - 1),
        out_specs=pl.BlockSpec((tr, d), lambda i: (i, 0)),
        out_shape=jax.ShapeDtypeStruct((r, d), out_dtype),
        compiler_params=_cp("parallel"),
    )(*args)


def _norm_bwd_call(x, g, sc, dy, name, add=None):
    r, d = x.shape
    tr = _row_tile(r, d)
    mod = sc is not None

    def body(*refs):
        add_ref = None
        if add is not None:
            add_ref, refs = refs[0], refs[1:]
        if mod:
            x_ref, g_ref, sc_ref, dy_ref, dx_ref, dg_ref, dsc_ref, dsh_ref = refs
        else:
            x_ref, g_ref, dy_ref, dx_ref, dg_ref = refs

        @pl.when(pl.program_id(0) == 0)
        def _():
            dg_ref[...] = jnp.zeros_like(dg_ref)
            if mod:
                dsc_ref[...] = jnp.zeros_like(dsc_ref)
                dsh_ref[...] = jnp.zeros_like(dsh_ref)

        xv = x_ref[...]
        gv = g_ref[...]
        dyv = dy_ref[...].astype(F32)
        rs = lax.rsqrt(jnp.mean(xv * xv, axis=-1, keepdims=True) + EPS)
        xh = xv * rs
        if mod:
            dsc_ref[...] += jnp.sum(dyv * (xh * gv), axis=0, keepdims=True)
            dsh_ref[...] += jnp.sum(dyv, axis=0, keepdims=True)
            t = dyv * (1.0 + sc_ref[...])
        else:
            t = dyv
        dg_ref[...] += jnp.sum(t * xh, axis=0, keepdims=True)
        dxh = t * gv
        dx = rs * (dxh - xh * jnp.mean(dxh * xh, axis=-1, keepdims=True))
        dx_ref[...] = dx if add_ref is None else add_ref[...] + dx

    vec = pl.BlockSpec((1, d), lambda i: (0, 0))
    blk = pl.BlockSpec((tr, d), lambda i: (i, 0))
    args = [x, g.reshape(1, d)] + ([sc.reshape(1, d)] if mod else []) + [dy]
    n_vec_out = 3 if mod else 1
    in_specs = [blk] + [vec] * (len(args) - 2) + [blk]
    if add is not None:
        args, in_specs = [add] + args, [blk] + in_specs
    outs = _pc(
        body, name=name, grid=(r // tr,),
        in_specs=in_specs,
        out_specs=[blk] + [vec] * n_vec_out,
        out_shape=[jax.ShapeDtypeStruct((r, d), F32)] + [jax.ShapeDtypeStruct((1, d), F32)] * n_vec_out,
        compiler_params=_cp("arbitrary"),
    )(*args)
    return outs


def _norm(name):
    @jax.custom_vjp
    def f(x, g):
        return _norm_fwd_call(x, g, None, None, name + "_fwd", out_dtype=BF16)

    def fwd(x, g):
        return f(x, g), (x, g)

    def bwd(res, dy):
        x, g = res
        dx, dg = _norm_bwd_call(x, g, None, dy, name + "_bwd")
        return dx, dg.reshape(g.shape)

    f.defvjp(fwd, bwd)
    return f


def _modulate(name):
    @jax.custom_vjp
    def f(x, g, sh, sc):
        return x, _norm_fwd_call(x, g, sc, sh, name + "_fwd", out_dtype=BF16)

    def fwd(x, g, sh, sc):
        return f(x, g, sh, sc), (x, g, sc)

    def bwd(res, cts):
        x, g, sc = res
        dx_res, dy = cts
        dx, dg, dsc, dsh = _norm_bwd_call(x, g, sc, dy, name + "_bwd", add=dx_res)
        return dx, dg.reshape(g.shape), dsh.reshape(g.shape), dsc.reshape(g.shape)

    f.defvjp(fwd, bwd)
    return f


def _gres(name):
    def fwd_call(x, gate, y):
        r, d = x.shape
        tr = _row_tile(r, d)

        def body(x_ref, g_ref, y_ref, o_ref):
            o_ref[...] = x_ref[...] + g_ref[...] * y_ref[...]

        blk = pl.BlockSpec((tr, d), lambda i: (i, 0))
        return _pc(
            body, name=name + "_fwd", grid=(r // tr,),
            in_specs=[blk, pl.BlockSpec((1, d), lambda i: (0, 0)), blk], out_specs=blk,
            out_shape=jax.ShapeDtypeStruct((r, d), F32), compiler_params=_cp("parallel"),
        )(x, gate.reshape(1, d), y)

    def bwd_call(do, gate, y):
        r, d = do.shape
        tr = _row_tile(r, d)

        def body(do_ref, g_ref, y_ref, dy_ref, dg_ref):
            @pl.when(pl.program_id(0) == 0)
            def _():
                dg_ref[...] = jnp.zeros_like(dg_ref)

            dov = do_ref[...]
            dy_ref[...] = g_ref[...] * dov
            dg_ref[...] += jnp.sum(dov * y_ref[...], axis=0, keepdims=True)

        blk = pl.BlockSpec((tr, d), lambda i: (i, 0))
        vec = pl.BlockSpec((1, d), lambda i: (0, 0))
        return _pc(
            body, name=name + "_bwd", grid=(r // tr,),
            in_specs=[blk, vec, blk], out_specs=[blk, vec],
            out_shape=[jax.ShapeDtypeStruct((r, d), F32), jax.ShapeDtypeStruct((1, d), F32)],
            compiler_params=_cp("arbitrary"),
        )(do, gate.reshape(1, d), y)

    @jax.custom_vjp
    def f(x, gate, y):
        return fwd_call(x, gate, y)

    def fwd(x, gate, y):
        return f(x, gate, y), (gate, y)

    def bwd(res, do):
        gate, y = res
        dy, dg = bwd_call(do, gate, y)
        return do, dg.reshape(gate.shape), dy

    f.defvjp(fwd, bwd)
    return f


_NT = (((1,), (1,)), ((), ()))
_NN = (((1,), (0,)), ((), ()))
_TN = (((0,), (0,)), ((), ()))
ATT_TQ = 2048
ATT_BTQ = 1024
ATT_SUB = 256
ATT_TK = 2816
LOG2E = 1.4426950408889634


def _per_head(flat, rows, width, where):
    if flat:
        return pl.BlockSpec((rows, width), where)
    return pl.BlockSpec((1, rows, width), lambda *g: (where(*g)[1], where(*g)[0], 0))


def _tile_of(ref):
    return ref if len(ref.shape) == 2 else ref.at[0]


def _head_dims(z, heads):
    if z.ndim == 2:
        return True, z.shape[0], z.shape[1] // heads
    return False, z.shape[1], z.shape[2]


def _attn_fwd_call(q, k, v, heads, scale, name):
    h, hkv = heads
    fq, sq, dq = _head_dims(q, h)
    fk, t, _ = _head_dims(k, hkv)
    dv = v.shape[1] // hkv
    dvx = dv + LANES
    grp = h // hkv
    tq, tk = _tile(sq, ATT_TQ), _tile(t, ATT_TK)
    sub = min(ATT_SUB, tq)
    nk = t // tk
    c = scale * LOG2E

    def body(q_ref, k_ref, v_ref, o_ref, lse_ref, m_s, acc_s):
        j = pl.program_id(2)

        @pl.when(j == 0)
        def _():
            m_s[...] = jnp.full_like(m_s, -jnp.inf)
            acc_s[...] = jnp.zeros_like(acc_s)

        q_t, kv = _tile_of(q_ref), _tile_of(k_ref)[...]
        one = (lax.broadcasted_iota(jnp.int32, (tk, LANES), 1) == 0).astype(BF16)
        vv = jnp.concatenate([v_ref[...], one], axis=1)
        for r in range(tq // sub):
            rows = pl.ds(r * sub, sub)
            s = lax.dot_general(q_t[rows, :], kv, _NT, preferred_element_type=F32) * c
            m_prev = m_s[rows, :]
            m_new = jnp.maximum(m_prev, jnp.max(s, axis=-1, keepdims=True))
            p = jnp.exp2(s - m_new)
            acc_s[rows, :] = jnp.exp2(m_prev - m_new) * acc_s[rows, :] + lax.dot_general(
                p.astype(BF16), vv, _NN, preferred_element_type=F32)
            m_s[rows, :] = m_new

        @pl.when(j == nk - 1)
        def _():
            acc = acc_s[...]
            l = acc[:, dv:dv + 1]
            o_ref[...] = (acc[:, :dv] / l).astype(o_ref.dtype)
            lse_ref[0] = m_s[...] + jnp.log(l) * LOG2E

    return _pc(
        body, name=name, grid=(h, sq // tq, nk),
        in_specs=[_per_head(fq, tq, dq, lambda hh, i, j: (i, hh)),
                  _per_head(fk, tk, dq, lambda hh, i, j: (j, hh // grp)),
                  pl.BlockSpec((tk, dv), lambda hh, i, j: (j, hh // grp))],
        out_specs=[pl.BlockSpec((tq, dv), lambda hh, i, j: (i, hh)),
                   pl.BlockSpec((1, tq, 1), lambda hh, i, j: (hh, i, 0))],
        out_shape=[jax.ShapeDtypeStruct((sq, h * dv), BF16), jax.ShapeDtypeStruct((h, sq, 1), F32)],
        scratch_shapes=[pltpu.VMEM((tq, 1), F32), pltpu.VMEM((tq, dvx), F32)],
        compiler_params=_cp("parallel", "parallel", "arbitrary"),
    )(q, k, v)


def _attn_bwd_call(q, k, v, o, do, lse, heads, scale, name):
    h, hkv = heads
    fq, sq, dq = _head_dims(q, h)
    fk, t, _ = _head_dims(k, hkv)
    dv = v.shape[1] // hkv
    grp = h // hkv
    tq, tk = _tile(sq, ATT_BTQ), _tile(t, ATT_TK)
    sub = min(ATT_SUB, tq)
    nq = sq // tq
    c = scale * LOG2E

    def body(q_ref, k_ref, v_ref, o_ref, do_ref, lse_ref, dq_ref, dk_ref, dv_ref, dk_s, dv_s):
        j = pl.program_id(1)
        i = pl.program_id(2)

        @pl.when(i == 0)
        def _():
            dk_s[...] = jnp.zeros_like(dk_s)
            dv_s[...] = jnp.zeros_like(dv_s)

        q_t, dq_t = _tile_of(q_ref), _tile_of(dq_ref)
        kv, vv = _tile_of(k_ref)[...], v_ref[...]
        for r in range(tq // sub):
            rows = pl.ds(r * sub, sub)
            qv, dov = q_t[rows, :], do_ref[rows, :]
            s = lax.dot_general(qv, kv, _NT, preferred_element_type=F32) * c
            p = jnp.exp2(s - lse_ref[0, rows, :])
            delta = jnp.sum(dov.astype(F32) * o_ref[rows, :].astype(F32), axis=-1, keepdims=True)
            dv_s[...] += lax.dot_general(p.astype(BF16), dov, _TN, preferred_element_type=F32)
            dp = lax.dot_general(dov, vv, _NT, preferred_element_type=F32)
            ds = (p * (dp - delta) * scale).astype(BF16)
            dk_s[...] += lax.dot_general(ds, qv, _TN, preferred_element_type=F32)
            dq_blk = lax.dot_general(ds, kv, _NN, preferred_element_type=F32)
            out_rows = pl.ds(pl.multiple_of(i * tq + r * sub, sub), sub)

            @pl.when(j == 0)
            def _():
                dq_t[out_rows, :] = dq_blk

            @pl.when(j > 0)
            def _():
                dq_t[out_rows, :] += dq_blk

        @pl.when(i == nq - 1)
        def _():
            _tile_of(dk_ref)[...] = dk_s[...]
            dv_ref[...] = dv_s[...]

    def shape_like(flat, rows, width):
        return (rows, h * width) if flat else (h, rows, width)

    return _pc(
        body, name=name, grid=(h, t // tk, nq),
        in_specs=[_per_head(fq, tq, dq, lambda hh, j, i: (i, hh)),
                  _per_head(fk, tk, dq, lambda hh, j, i: (j, hh // grp)),
                  pl.BlockSpec((tk, dv), lambda hh, j, i: (j, hh // grp)),
                  pl.BlockSpec((tq, dv), lambda hh, j, i: (i, hh)),
                  pl.BlockSpec((tq, dv), lambda hh, j, i: (i, hh)),
                  pl.BlockSpec((1, tq, 1), lambda hh, j, i: (hh, i, 0))],
        out_specs=[_per_head(fq, sq, dq, lambda hh, j, i: (0, hh)),
                   _per_head(fk, tk, dq, lambda hh, j, i: (j, hh)),
                   pl.BlockSpec((tk, dv), lambda hh, j, i: (j, hh))],
        out_shape=[jax.ShapeDtypeStruct(shape_like(fq, sq, dq), F32), jax.ShapeDtypeStruct(shape_like(fk, t, dq), F32),
                   jax.ShapeDtypeStruct((t, h * dv), F32)],
        scratch_shapes=[pltpu.VMEM((tk, dq), F32), pltpu.VMEM((tk, dv), F32)],
        compiler_params=_cp("parallel", "arbitrary", "arbitrary"),
    )(q, k, v, o, do, lse)


def _attention(name, scale):
    def per_head(z):
        rows, hh, d = z.shape
        return z.reshape(rows, hh * d).astype(BF16) if d % LANES == 0 else z.transpose(1, 0, 2).astype(BF16)

    def rows_first(g, hh):
        return g.reshape(g.shape[0], hh, -1) if g.ndim == 2 else g.transpose(1, 0, 2)

    @jax.custom_vjp
    def f(q, k, v):
        return fwd(q, k, v)[0]

    def fwd(q, k, v):
        heads = (q.shape[1], k.shape[1])
        qx, kx, vx = per_head(q), per_head(k), v.reshape(v.shape[0], -1).astype(BF16)
        o, lse = _attn_fwd_call(qx, kx, vx, heads, scale, name + "_fwd")
        like = tuple(jnp.zeros((0,), z.dtype) for z in (q, k, v))
        return o, (qx, kx, vx, o, lse, like)

    def bwd(res, do):
        qx, kx, vx, o, lse, like = res
        h, t = lse.shape[0], vx.shape[0]
        hkv = vx.shape[1] // (o.shape[1] // h)
        dq, dk, dvv = _attn_bwd_call(qx, kx, vx, o, do, lse, (h, hkv), scale, name + "_bwd")
        dq, dk, dvv = rows_first(dq, h), rows_first(dk, h), rows_first(dvv, h)
        if h != hkv:
            dk = dk.reshape(t, hkv, h // hkv, -1).sum(axis=2)
            dvv = dvv.reshape(t, hkv, h // hkv, -1).sum(axis=2)
        return tuple(d.astype(z.dtype) for d, z in zip((dq, dk, dvv), like))

    f.defvjp(fwd, bwd)
    return f


CONV_TF = 128
CONV_ROWS = 256


def _conv_neighbours(ref, c, r0, cur, row, nchunks, s, rc):
    halo = SUBLANES * (4 // ref.dtype.itemsize)
    prev = ref[pl.ds(pl.multiple_of(jnp.maximum(r0 - halo, 0), halo), halo), :].astype(F32)
    nxt = ref[pl.ds(pl.multiple_of(jnp.minimum(r0 + rc, s - halo), halo), halo), :].astype(F32)
    prow = jnp.where(c > 0, prev[halo - 1:halo, :], 0.0)
    nrow = jnp.where(c < nchunks - 1, nxt[0:1, :], 0.0)
    before = jnp.where(row == 0, prow, pltpu.roll(cur, 1, 0))
    after = jnp.where(row == rc - 1, nrow, pltpu.roll(cur, rc - 1, 0))
    return before, after


def _conv_fwd_call(u, w, b, name):
    s, f2 = u.shape
    f = f2 // 2
    tf = CONV_TF
    nf = f // tf
    rc = min(CONV_ROWS, s)
    nchunks = s // rc

    def body(g_ref, v_ref, w_ref, b_ref, o_ref):
        w0, w1, w2, bv = w_ref[0:1, :], w_ref[1:2, :], w_ref[2:3, :], b_ref[...]
        row = lax.broadcasted_iota(jnp.int32, (rc, tf), 0)

        def chunk(c, carry):
            r0 = pl.multiple_of(c * rc, rc)
            cur = g_ref[pl.ds(r0, rc), :].astype(F32)
            before, after = _conv_neighbours(g_ref, c, r0, cur, row, nchunks, s, rc)
            gc = before * w0 + cur * w1 + after * w2 + bv
            val = v_ref[pl.ds(r0, rc), :].astype(F32)
            o_ref[pl.ds(r0, rc), :] = (gc * jax.nn.sigmoid(gc) * val).astype(o_ref.dtype)
            return carry

        lax.fori_loop(0, nchunks, chunk, 0)

    return _pc(
        body, name=name, grid=(nf,),
        in_specs=[pl.BlockSpec((s, tf), lambda j: (0, j)), pl.BlockSpec((s, tf), lambda j: (0, j + nf)),
                  pl.BlockSpec((3, tf), lambda j: (0, j)), pl.BlockSpec((1, tf), lambda j: (0, j))],
        out_specs=pl.BlockSpec((s, tf), lambda j: (0, j)),
        out_shape=jax.ShapeDtypeStruct((s, f), u.dtype),
        compiler_params=_cp("parallel"),
    )(u, u, w, b.reshape(1, f))


def _conv_bwd_call(u, w, b, da, name):
    s, f2 = u.shape
    f = f2 // 2
    tf = CONV_TF
    nf = f // tf
    rc = min(CONV_ROWS, s)
    nchunks = s // rc

    def body(g_ref, v_ref, da_ref, w_ref, b_ref, dg_ref, dv_ref, dw_ref, db_ref, dgc_s):
        w0, w1, w2, bv = w_ref[0:1, :], w_ref[1:2, :], w_ref[2:3, :], b_ref[...]
        row = lax.broadcasted_iota(jnp.int32, (rc, tf), 0)

        def chunk1(c, carry):
            a0, a1, a2, ab = carry
            r0 = pl.multiple_of(c * rc, rc)
            cur = g_ref[pl.ds(r0, rc), :].astype(F32)
            before, after = _conv_neighbours(g_ref, c, r0, cur, row, nchunks, s, rc)
            gc = before * w0 + cur * w1 + after * w2 + bv
            sig = jax.nn.sigmoid(gc)
            dav = da_ref[pl.ds(r0, rc), :].astype(F32)
            dv_ref[pl.ds(r0, rc), :] = (dav * (gc * sig)).astype(dv_ref.dtype)
            dgc = dav * v_ref[pl.ds(r0, rc), :].astype(F32) * (sig * (1.0 + gc * (1.0 - sig)))
            dgc_s[pl.ds(r0, rc), :] = dgc
            return (a0 + jnp.sum(dgc * before, axis=0, keepdims=True),
                    a1 + jnp.sum(dgc * cur, axis=0, keepdims=True),
                    a2 + jnp.sum(dgc * after, axis=0, keepdims=True),
                    ab + jnp.sum(dgc, axis=0, keepdims=True))

        z = jnp.zeros((1, tf), F32)
        a0, a1, a2, ab = lax.fori_loop(0, nchunks, chunk1, (z, z, z, z))
        dw_ref[0:1, :] = a0
        dw_ref[1:2, :] = a1
        dw_ref[2:3, :] = a2
        db_ref[...] = ab

        def chunk2(c, carry):
            r0 = pl.multiple_of(c * rc, rc)
            cur = dgc_s[pl.ds(r0, rc), :]
            before, after = _conv_neighbours(dgc_s, c, r0, cur, row, nchunks, s, rc)
            dg_ref[pl.ds(r0, rc), :] = (after * w0 + cur * w1 + before * w2).astype(dg_ref.dtype)
            return carry

        lax.fori_loop(0, nchunks, chunk2, 0)

    col = pl.BlockSpec((s, tf), lambda j: (0, j))
    return _pc(
        body, name=name, grid=(nf,),
        in_specs=[col, pl.BlockSpec((s, tf), lambda j: (0, j + nf)), col,
                  pl.BlockSpec((3, tf), lambda j: (0, j)), pl.BlockSpec((1, tf), lambda j: (0, j))],
        out_specs=[col, col, pl.BlockSpec((3, tf), lambda j: (0, j)), pl.BlockSpec((1, tf), lambda j: (0, j))],
        out_shape=[jax.ShapeDtypeStruct((s, f), u.dtype), jax.ShapeDtypeStruct((s, f), u.dtype),
                   jax.ShapeDtypeStruct((3, f), F32), jax.ShapeDtypeStruct((1, f), F32)],
        scratch_shapes=[pltpu.VMEM((s, tf), F32)],
        compiler_params=_cp("parallel"),
    )(u, u, da, w, b.reshape(1, f))


def _convgate(name):
    @jax.custom_vjp
    def f(u, w, b):
        return _conv_fwd_call(u, w, b, name + "_fwd")

    def fwd(u, w, b):
        return f(u, w, b), (u, w, b)

    def bwd(res, da):
        u, w, b = res
        dg, dv, dw, db = _conv_bwd_call(u, w, b, da, name + "_bwd")
        return jnp.concatenate([dg, dv], axis=-1), dw, db.reshape(b.shape)

    f.defvjp(fwd, bwd)
    return f


def _loss_call(y, tgt):
    r, d = y.shape
    tr = _row_tile(r, d)

    def body(y_ref, t_ref, dy_ref, part_ref):
        @pl.when(pl.program_id(0) == 0)
        def _():
            part_ref[...] = jnp.zeros_like(part_ref)

        diff = y_ref[...] - t_ref[...]
        dy_ref[...] = diff / d
        part_ref[...] += jnp.sum(diff * diff, axis=0, keepdims=True)

    blk = pl.BlockSpec((tr, d), lambda i: (i, 0))
    vec = pl.BlockSpec((1, d), lambda i: (0, 0))
    dy, part = _pc(
        body, name="loss_head", grid=(r // tr,), in_specs=[blk, blk], out_specs=[blk, vec],
        out_shape=[jax.ShapeDtypeStruct((r, d), F32), jax.ShapeDtypeStruct((1, d), F32)],
        compiler_params=_cp("arbitrary"),
    )(y, tgt)
    return 0.5 * (jnp.sum(part) / d), dy


def _adamw_call(w, m, v, g, name, summed):
    r, c = w.shape
    tr = _tile(r, max(2 * SUBLANES, (1 << 18) // c), 2 * SUBLANES)
    slabs = g.shape[0] if summed else 1

    def body(g_ref, w_ref, m_ref, v_ref, go_ref, d_ref, mo_ref, vo_ref):
        if summed:
            gv = g_ref[0].astype(F32)
            for slab in range(1, slabs):
                gv = gv + g_ref[slab].astype(F32)
        else:
            gv = g_ref[...]
        mn = ADAM_B1 * m_ref[...] + (1.0 - ADAM_B1) * gv
        vn = ADAM_B2 * v_ref[...] + (1.0 - ADAM_B2) * (gv * gv)
        m_hat = mn / (1.0 - ADAM_B1 ** ADAM_STEP)
        v_hat = vn / (1.0 - ADAM_B2 ** ADAM_STEP)
        go_ref[...] = gv
        d_ref[...] = -ADAM_LR * (m_hat / (jnp.sqrt(v_hat) + ADAM_EPS) + ADAM_WD * w_ref[...])
        mo_ref[...] = mn
        vo_ref[...] = vn

    blk = pl.BlockSpec((tr, c), lambda i: (i, 0))
    gblk = pl.BlockSpec((slabs, tr, c), lambda i: (0, i, 0)) if summed else blk
    return _pc(
        body, name=name, grid=(r // tr,), in_specs=[gblk, blk, blk, blk], out_specs=[blk] * 4,
        out_shape=[jax.ShapeDtypeStruct((r, c), F32)] * 4, compiler_params=_cp("parallel"),
    )(g, w, m, v)


_ANY = pl.BlockSpec(memory_space=pl.ANY)
MESH = pl.DeviceIdType.MESH


def _window(ref, axis, idx, size):
    if axis == 0:
        return ref.at[pl.ds(idx * size, size), :]
    return ref.at[:, pl.ds(idx * size, size)]


def _allgather(shards, axes, name):
    n_p = len(shards)
    layers = [s.shape[0] for s in shards]
    out_shape = []
    for s, ax in zip(shards, axes):
        _, k, n = s.shape
        full = (k * N_DEV, n) if ax == 0 else (k, n * N_DEV)
        out_shape += [jax.ShapeDtypeStruct(full, s.dtype)] * s.shape[0]
    n_out = len(out_shape)

    def body(*refs):
        x_refs = refs[:n_p]
        flat = refs[n_p:n_p + n_out]
        send_sems, recv_sems, local_sems = refs[n_p + n_out:]
        outs, at = [], 0
        for cnt in layers:
            outs.append(flat[at:at + cnt])
            at += cnt
        x, y, c = lax.axis_index("x"), lax.axis_index("y"), lax.axis_index("c")
        me, sibling = (x, y, c), (x, y, 1 - c)
        chips = [(1 - x, y), (x, 1 - y), (1 - x, 1 - y)]

        def lin(d):
            return 4 * d[0] + 2 * d[1] + d[2]

        def copies(p, sem, block, to, from_shard):
            size = shards[p].shape[1 + axes[p]]
            res = []
            for l in range(layers[p]):
                dst = _window(outs[p][l], axes[p], lin(block), size)
                res.append(pltpu.make_async_remote_copy(
                    src_ref=x_refs[p].at[l] if from_shard else dst, dst_ref=dst,
                    send_sem=send_sems.at[p, sem], recv_sem=recv_sems.at[p, sem], device_id=to, device_id_type=MESH))
            return res

        def drained(p, sem):
            return pltpu.make_async_remote_copy(
                src_ref=x_refs[p], dst_ref=x_refs[p], send_sem=send_sems.at[p, sem], recv_sem=recv_sems.at[p, sem],
                device_id=me, device_id_type=MESH)

        for p in range(n_p):
            size = shards[p].shape[1 + axes[p]]
            for l in range(layers[p]):
                pltpu.make_async_copy(x_refs[p].at[l], _window(outs[p][l], axes[p], lin(me), size), local_sems.at[p]).start()
            for cp in copies(p, 0, me, sibling, True):
                cp.start()
            for j, chip in enumerate(chips):
                for cp in copies(p, 1 + j, me, (*chip, c), True):
                    cp.start()
        for p in range(n_p):
            for j, chip in enumerate(chips):
                drained(p, 1 + j).wait_recv()
                for cp in copies(p, 4 + j, (*chip, c), sibling, False):
                    cp.start()
        for p in range(n_p):
            for sem in (0, 4, 5, 6):
                drained(p, sem).wait_recv()
            for sem in range(7):
                drained(p, sem).wait_send()
            pltpu.make_async_copy(x_refs[p], x_refs[p], local_sems.at[p]).wait()

    flat = _pc(
        body, name=name, in_specs=[_ANY] * n_p, out_specs=[_ANY] * n_out, out_shape=out_shape,
        scratch_shapes=[pltpu.SemaphoreType.DMA((n_p, 7)), pltpu.SemaphoreType.DMA((n_p, 7)),
                        pltpu.SemaphoreType.DMA((n_p,))],
    )(*shards)
    res, at = [], 0
    for cnt in layers:
        res.append(list(flat[at:at + cnt]))
        at += cnt
    return res


N_CHIP = 4


def _pair_exchange(grads, axes, name):
    n_p = len(grads)
    layers = [len(g) for g in grads]
    n_in = sum(layers)
    blocks = []
    for g, ax in zip(grads, axes):
        kk, nn = g[0].shape
        blocks.append((kk // N_DEV, nn) if ax == 0 else (kk, nn // N_DEV))
    out_shape = [jax.ShapeDtypeStruct((N_CHIP, cnt) + blk, g[0].dtype) for g, cnt, blk in zip(grads, layers, blocks)]

    def body(*refs):
        flat = refs[:n_in]
        mine = refs[n_in:n_in + n_p]
        theirs = refs[n_in + n_p:n_in + 2 * n_p]
        send_sems, recv_sems, local_sems = refs[n_in + 2 * n_p:]
        g_refs, at = [], 0
        for cnt in layers:
            g_refs.append(flat[at:at + cnt])
            at += cnt
        x, y, c = lax.axis_index("x"), lax.axis_index("y"), lax.axis_index("c")
        for p in range(n_p):
            size = blocks[p][axes[p]]
            for q in range(N_CHIP):
                for l in range(layers[p]):
                    pltpu.make_async_copy(_window(g_refs[p][l], axes[p], 2 * q + c, size), mine[p].at[q, l],
                                          local_sems.at[p]).start()
                    pltpu.make_async_remote_copy(
                        src_ref=_window(g_refs[p][l], axes[p], 2 * q + (1 - c), size), dst_ref=theirs[p].at[q, l],
                        send_sem=send_sems.at[p], recv_sem=recv_sems.at[p],
                        device_id=(x, y, 1 - c), device_id_type=MESH).start()
        for p in range(n_p):
            pltpu.make_async_remote_copy(src_ref=theirs[p], dst_ref=theirs[p], send_sem=send_sems.at[p],
                                         recv_sem=recv_sems.at[p], device_id=(x, y, c), device_id_type=MESH).wait()
            pltpu.make_async_copy(theirs[p], theirs[p], local_sems.at[p]).wait()

    outs = _pc(
        body, name=name, in_specs=[_ANY] * n_in, out_specs=[_ANY] * (2 * n_p), out_shape=out_shape + out_shape,
        scratch_shapes=[pltpu.SemaphoreType.DMA((n_p,)), pltpu.SemaphoreType.DMA((n_p,)), pltpu.SemaphoreType.DMA((n_p,))],
    )(*[g for gl in grads for g in gl])
    return outs[:n_p], outs[n_p:]


def _add_pairs(a, b, name):
    cols = a.shape[-1]
    a2, b2 = a.reshape(-1, cols), b.reshape(-1, cols)
    r = a2.shape[0]
    tr = _tile(r, max(2 * SUBLANES, (1 << 20) // cols), 2 * SUBLANES)

    def body(a_ref, b_ref, o_ref):
        o_ref[...] = (a_ref[...].astype(F32) + b_ref[...].astype(F32)).astype(o_ref.dtype)

    blk = pl.BlockSpec((tr, cols), lambda i: (i, 0))
    return _pc(body, name=name, grid=(r // tr,), in_specs=[blk, blk], out_specs=blk,
               out_shape=jax.ShapeDtypeStruct((r, cols), a.dtype), compiler_params=_cp("parallel"))(a2, b2).reshape(a.shape)


def _chip_exchange(parts, name):
    n_p = len(parts)

    def body(*refs):
        srcs = refs[:n_p]
        lands = refs[n_p:2 * n_p]
        send_sems, recv_sems, local_sems = refs[2 * n_p:]
        x, y, c = lax.axis_index("x"), lax.axis_index("y"), lax.axis_index("c")
        my_chip = 2 * x + y
        for p in range(n_p):
            pltpu.make_async_copy(srcs[p].at[my_chip], lands[p].at[my_chip], local_sems.at[p]).start()
            for rel in range(1, N_CHIP):
                px = 1 - x if rel & 2 else x
                py = 1 - y if rel & 1 else y
                pltpu.make_async_remote_copy(
                    src_ref=srcs[p].at[2 * px + py], dst_ref=lands[p].at[my_chip],
                    send_sem=send_sems.at[p, rel - 1], recv_sem=recv_sems.at[p, rel - 1],
                    device_id=(px, py, c), device_id_type=MESH).start()
        for p in range(n_p):
            slab = lands[p].at[0]
            for rel in range(1, N_CHIP):
                pltpu.make_async_remote_copy(
                    src_ref=slab, dst_ref=slab, send_sem=send_sems.at[p, rel - 1], recv_sem=recv_sems.at[p, rel - 1],
                    device_id=(x, y, c), device_id_type=MESH).wait()
            pltpu.make_async_copy(slab, slab, local_sems.at[p]).wait()

    return _pc(
        body, name=name, in_specs=[_ANY] * n_p, out_specs=[_ANY] * n_p,
        out_shape=[jax.ShapeDtypeStruct(z.shape, z.dtype) for z in parts],
        scratch_shapes=[pltpu.SemaphoreType.DMA((n_p, N_CHIP - 1)), pltpu.SemaphoreType.DMA((n_p, N_CHIP - 1)),
                        pltpu.SemaphoreType.DMA((n_p,))],
    )(*parts)


def _gather_rows(a, name):
    r, c = a.shape
    return _allgather([a[None]], [0], name)[0][0].reshape(N_DEV, r, c)


def _conditioning(me):
    def forward(s16, w, b):
        nl, _, cols = w.shape
        part = jnp.concatenate([_matmul(s16, w[i], name="mod_fwd") for i in range(nl)], axis=0)
        full = _gather_rows(part, "mod_gather").reshape(N_DEV, nl, MOD_ROWS, cols)
        return full.transpose(1, 2, 0, 3).reshape(nl, MOD_ROWS, N_DEV * cols) + b[:, None, :]

    @jax.custom_vjp
    def f(s16, w, b):
        return forward(s16, w, b)

    def fwd(s16, w, b):
        return forward(s16, w, b), (s16, w)

    def bwd(res, dm):
        s16, w = res
        nl, dmodel, cols = w.shape
        width = dm.shape[-1]
        mine = lax.dynamic_slice_in_dim(dm, me, 1, axis=1)
        both = jnp.concatenate([mine, dm[:, N_DEV:N_DEV + 1]], axis=1).reshape(nl * 2, width)
        allrows = _gather_rows(both, "dmod_gather").reshape(N_DEV, nl, 2, width)
        total = jnp.concatenate([allrows[:, :, 0].transpose(1, 0, 2), jnp.sum(allrows[:, :, 1], axis=0)[:, None, :],
                                 jnp.zeros((nl, MOD_ROWS - N_DEV - 1, width), F32)], axis=1)
        db = jnp.sum(total, axis=1)
        my_cols = lax.dynamic_slice_in_dim(total, me * cols, cols, axis=2)
        dw = jnp.stack([_matmul(s16, my_cols[i], ta=True, name="mod_dw") for i in range(nl)])
        ds_part = _matmul(my_cols[0], w[0], tb=True, name="mod_ds")
        for i in range(1, nl):
            ds_part = ds_part + _matmul(my_cols[i], w[i], tb=True, name="mod_ds")
        ds = jnp.sum(_gather_rows(ds_part, "dcond_gather"), axis=0)
        return ds, dw, db

    f.defvjp(fwd, bwd)
    return f


def _rope_tables(s, rot_dim):
    t = jnp.arange(s, dtype=jnp.int32)
    rows, cols = t // GRID_W, t % GRID_W
    axis_dim = rot_dim // 2
    inv = jnp.power(ROPE_BASE, -jnp.arange(0, axis_dim, 2, dtype=F32) / axis_dim)
    ang_r = rows.astype(F32)[:, None] * inv
    ang_c = cols.astype(F32)[:, None] * inv
    ang = jnp.concatenate([ang_r, ang_r, ang_c, ang_c], axis=-1)
    return jnp.cos(ang), jnp.sin(ang)


def _rotate_half(z):
    z1, z2 = jnp.split(z, 2, axis=-1)
    return jnp.concatenate([-z2, z1], axis=-1)


def _rope(z, tables):
    if tables is None:
        return z
    cos, sin = tables
    half = z.shape[-1] // 2
    rot = jnp.concatenate([_rotate_half(z[..., :half]), _rotate_half(z[..., half:])], axis=-1)
    return (z * cos[:, None, :] + rot * sin[:, None, :]).astype(z.dtype)


def _head_norm(name, z, g):
    s, h, d = z.shape
    return _norm(name)(z.reshape(s * h, d), g).reshape(s, h, d)


def _mla_q(tag, hx, rope, w):
    s = hx.shape[0]
    cq = _norm(tag + "_cq_norm")(_mm(tag + "_dq")(hx, w["mla_w_dq"]), w["mla_g_dq"])
    q = _mm(tag + "_uq")(cq, w["mla_w_uq"]).reshape(s, -1, MLA_NOPE + MLA_ROPE)
    q_nope = _head_norm(tag + "_qn_norm", q[..., :MLA_NOPE], w["mla_g_q_nope"])
    q_pe = _rope(_head_norm(tag + "_qp_norm", q[..., MLA_NOPE:], w["mla_g_q_pe"]), rope)
    return jnp.concatenate([q_nope, q_pe], axis=-1)


def _mla_kv(tag, hx, rope, w):
    s = hx.shape[0]
    kv_a = _mm(tag + "_dkv")(hx, w["mla_w_dkv"])
    rank = kv_a.shape[-1] - MLA_ROPE
    c_kv = _norm(tag + "_ckv_norm")(kv_a[:, :rank], w["mla_g_dkv"])
    k_pe = _rope(_norm(tag + "_kp_norm")(kv_a[:, rank:], w["mla_g_k_pe"])[:, None, :], rope)
    kv = _mm(tag + "_ukv")(c_kv, w["mla_w_ukv"]).reshape(s, -1, MLA_NOPE + MLA_V)
    heads = kv.shape[1]
    k_nope = _head_norm(tag + "_kn_norm", kv[..., :MLA_NOPE], w["mla_g_k_nope"])
    k = jnp.concatenate([k_nope, jnp.broadcast_to(k_pe, (s, heads, MLA_ROPE))], axis=-1)
    return k, kv[..., MLA_NOPE:]


def _gqa_q(tag, hx, rope, w):
    s = hx.shape[0]
    q = _mm(tag + "_q")(hx, w["gqa_w_q"]).reshape(s, -1, GQA_HEAD_DIM)
    return _rope(_head_norm(tag + "_q_norm", q, w["gqa_g_q"]), rope)


def _gqa_kv(tag, hx, rope, w):
    s = hx.shape[0]
    kv = _mm(tag + "_kv")(hx, w["gqa_w_kv"]).reshape(s, 2, GQA_KV_HEADS, GQA_HEAD_DIM)
    k = _rope(_head_norm(tag + "_k_norm", kv[:, 0], w["gqa_g_k"]), rope)
    return k, kv[:, 1]


def _conv_ffn(tag, hx, w):
    u = _mm(tag + "_up", BF16)(hx, w["ffn_w_up"])
    a = _convgate(tag + "_conv")(u, w["ffn_conv_w"], w["ffn_conv_b"])
    return _mm(tag + "_down")(a, w["ffn_w_down"])


def _forward(leaves, ctx, silu_c_all, me, depth):
    x = leaves["x"]
    s, d = x.shape
    rope_mla = _rope_tables(s, MLA_ROPE)
    rope_gqa = _rope_tables(s, GQA_HEAD_DIM)
    silu_cc = jax.nn.silu(leaves["c_ctx"])
    s16 = jnp.concatenate([silu_c_all, silu_cc[None, :], jnp.zeros((MOD_ROWS - N_DEV - 1, d), F32)], axis=0)
    mods = _conditioning(me)(s16, leaves["w_mod"], leaves["b_mod"])
    for i in range(depth):
        last = i == depth - 1
        w = {k: v[i // 2] for k, v in leaves["mixer"][i % 2].items()}
        w.update({k: v[i] for k, v in leaves["ffn"].items()})
        mod = lax.dynamic_index_in_dim(mods[i], me, axis=0, keepdims=False)
        sh1, sc1, g1, sh2, sc2, g2 = jnp.split(mod, N_MOD)
        csh1, csc1, cg1, csh2, csc2, cg2 = jnp.split(mods[i, N_DEV], N_MOD)
        tag = f"l{i}"
        x, hx = _modulate(tag + "_mix_mod")(x, leaves["norm_mix"][i], sh1, sc1)
        ctx, hc = _modulate(tag + "c_mix_mod")(ctx, leaves["norm_mix"][i], csh1, csc1)
        if i % 2 == 0:
            q_fn, kv_fn, w_o = _mla_q, _mla_kv, w["mla_w_o"]
            rope, scale = rope_mla, 1.0 / float(MLA_NOPE + MLA_ROPE) ** 0.5
        else:
            q_fn, kv_fn, w_o = _gqa_q, _gqa_kv, w["gqa_w_o"]
            rope, scale = rope_gqa, 1.0 / float(GQA_HEAD_DIM) ** 0.5
        k_lat, v_lat = kv_fn(tag, hx, rope, w)
        k_ctx, v_ctx = kv_fn(tag + "c", hc, None, w)
        o = _attention(tag + "_attn", scale)(q_fn(tag, hx, rope, w), jnp.concatenate([k_lat, k_ctx], axis=0),
                                             jnp.concatenate([v_lat, v_ctx], axis=0))
        x = _gres(tag + "_mix_res")(x, g1, _mm(tag + "_o")(o, w_o))
        if not last:
            oc = _attention(tag + "c_attn", scale)(q_fn(tag + "c", hc, None, w), k_ctx, v_ctx)
            ctx = _gres(tag + "c_mix_res")(ctx, cg1, _mm(tag + "c_o")(oc, w_o))
        x, hx = _modulate(tag + "_ffn_mod")(x, leaves["norm_ffn"][i], sh2, sc2)
        x = _gres(tag + "_ffn_res")(x, g2, _conv_ffn(tag, hx, w))
        if not last:
            ctx, hc = _modulate(tag + "c_ffn_mod")(ctx, leaves["norm_ffn"][i], csh2, csc2)
            ctx = _gres(tag + "c_ffn_res")(ctx, cg2, _conv_ffn(tag + "c", hc, w))
    return x


_MLA_BIG = {"mla_w_dq": 0, "mla_w_uq": 1, "mla_w_dkv": 0, "mla_w_ukv": 1, "mla_w_o": 0}
_GQA_BIG = {"gqa_w_q": 0, "gqa_w_kv": 0, "gqa_w_o": 0}
_FFN_BIG = {"ffn_w_up": 1, "ffn_w_down": 0}
_BIG_GROUPS = [("mla", _MLA_BIG), ("gqa", _GQA_BIG), ("ffn_up", {"ffn_w_up": 1}), ("ffn_down", {"ffn_w_down": 0})]
_MLA_GAINS = ["mla_g_dq", "mla_g_q_nope", "mla_g_q_pe", "mla_g_dkv", "mla_g_k_pe", "mla_g_k_nope"]
_GQA_GAINS = ["gqa_g_q", "gqa_g_k"]
_SHARED = ["norm_mix", "norm_ffn"] + _MLA_GAINS + _GQA_GAINS + ["ffn_conv_b"]
_SUMMED = ["c_ctx", "b_mod"]

_NAMES = ["c_ctx", "w_mod", "b_mod", "norm_mix", "norm_ffn", "mla_w_dq", "mla_g_dq", "mla_w_uq", "mla_g_q_nope",
          "mla_g_q_pe", "mla_w_dkv", "mla_g_dkv", "mla_g_k_pe", "mla_w_ukv", "mla_g_k_nope", "mla_w_o", "gqa_w_q",
          "gqa_g_q", "gqa_w_kv", "gqa_g_k", "gqa_w_o", "ffn_w_up", "ffn_conv_w", "ffn_conv_b", "ffn_w_down"]


def _rows128(a):
    return a.reshape(-1, LANES)


def kernel(x, c, ctx, c_ctx, w_mod, b_mod, norm_mix, norm_ffn, mla_w_dq, mla_g_dq, mla_w_uq, mla_g_q_nope, mla_g_q_pe, mla_w_dkv, mla_g_dkv, mla_g_k_pe, mla_w_ukv, mla_g_k_nope, mla_w_o, gqa_w_q, gqa_g_q, gqa_w_kv, gqa_g_k, gqa_w_o, ffn_w_up, ffn_conv_w, ffn_conv_b, ffn_w_down, loss_target, m_c_ctx, m_w_mod, m_b_mod, m_norm_mix, m_norm_ffn, m_mla_w_dq, m_mla_g_dq, m_mla_w_uq, m_mla_g_q_nope, m_mla_g_q_pe, m_mla_w_dkv, m_mla_g_dkv, m_mla_g_k_pe, m_mla_w_ukv, m_mla_g_k_nope, m_mla_w_o, m_gqa_w_q, m_gqa_g_q, m_gqa_w_kv, m_gqa_g_k, m_gqa_w_o, m_ffn_w_up, m_ffn_conv_w, m_ffn_conv_b, m_ffn_w_down, v_c_ctx, v_w_mod, v_b_mod, v_norm_mix, v_norm_ffn, v_mla_w_dq, v_mla_g_dq, v_mla_w_uq, v_mla_g_q_nope, v_mla_g_q_pe, v_mla_w_dkv, v_mla_g_dkv, v_mla_g_k_pe, v_mla_w_ukv, v_mla_g_k_nope, v_mla_w_o, v_gqa_w_q, v_gqa_g_q, v_gqa_w_kv, v_gqa_g_k, v_gqa_w_o, v_ffn_w_up, v_ffn_conv_w, v_ffn_conv_b, v_ffn_w_down):
    args = locals()
    wts = {n: args[n] for n in _NAMES}
    mom = {n: args["m_" + n] for n in _NAMES}
    var = {n: args["v_" + n] for n in _NAMES}
    me = 4 * lax.axis_index("x") + 2 * lax.axis_index("y") + lax.axis_index("c")
    depth = w_mod.shape[0]
    d_model = x.shape[-1]
    d_ff = ffn_conv_b.shape[-1]

    taps = ffn_conv_w.reshape(-1)
    packed = jnp.concatenate([jax.nn.silu(c).reshape(-1), taps])
    packed = jnp.concatenate([packed, jnp.zeros((-packed.size % (SUBLANES * LANES),), F32)])
    got = _gather_rows(_rows128(packed), "cond_gather").reshape(N_DEV, -1)
    silu_c_all = got[:, :d_model]
    conv_w_full = got[:, d_model:d_model + taps.size].reshape(N_DEV, depth, 3, -1)
    conv_w_full = conv_w_full.transpose(1, 2, 0, 3).reshape(depth, 3, d_ff)

    full = {}
    for gname, group in _BIG_GROUPS:
        names = list(group)
        got_w = _allgather([wts[n].astype(BF16) for n in names], [group[n] for n in names], "gather_" + gname)
        full.update(dict(zip(names, got_w)))

    leaves = {
        "x": x[0], "c_ctx": c_ctx, "w_mod": w_mod, "b_mod": b_mod, "norm_mix": norm_mix, "norm_ffn": norm_ffn,
        "mixer": [
            {**{n: full[n] for n in _MLA_BIG}, **{n: wts[n] for n in _MLA_GAINS}},
            {**{n: full[n] for n in _GQA_BIG}, **{n: wts[n] for n in _GQA_GAINS}},
        ],
        "ffn": {"ffn_w_up": full["ffn_w_up"], "ffn_w_down": full["ffn_w_down"], "ffn_conv_w": conv_w_full,
                "ffn_conv_b": ffn_conv_b},
    }
    y, pullback = jax.vjp(lambda lv: _forward(lv, ctx[0], silu_c_all, me, depth), leaves)
    loss_part, dy = _loss_call(y, loss_target[0])
    (gl,) = pullback(dy)
    loss = lax.psum(loss_part, ("x", "y", "c"))

    grads, deltas, new_m, new_v = {}, {}, {}, {}

    def put(n, outs, shape):
        grads[n], deltas[n], new_m[n], new_v[n] = (o.reshape(shape) for o in outs)

    gfull = {**{n: gl["mixer"][0][n] for n in _MLA_BIG}, **{n: gl["mixer"][1][n] for n in _GQA_BIG},
             "ffn_w_up": gl["ffn"]["ffn_w_up"], "ffn_w_down": gl["ffn"]["ffn_w_down"]}
    for gname, group in _BIG_GROUPS:
        names = list(group)
        mine, theirs = _pair_exchange([gfull[n] for n in names], [group[n] for n in names], "pair_" + gname)
        sums = [_add_pairs(a, b, "pairsum_" + n) for n, a, b in zip(names, mine, theirs)]
        lands = _chip_exchange(sums, "exchange_" + gname)
        for n, land in zip(names, lands):
            shape = wts[n].shape
            cols = shape[-1]
            outs = _adamw_call(wts[n].reshape(-1, cols), mom[n].reshape(-1, cols), var[n].reshape(-1, cols),
                               land.reshape(N_CHIP, -1, cols), "adamw_" + n, True)
            put(n, outs, shape)

    cols = w_mod.shape[-1]
    outs = _adamw_call(w_mod.reshape(-1, cols), m_w_mod.reshape(-1, cols), v_w_mod.reshape(-1, cols),
                       gl["w_mod"].reshape(-1, cols), "adamw_w_mod", False)
    put("w_mod", outs, w_mod.shape)

    share = {"norm_mix": gl["norm_mix"], "norm_ffn": gl["norm_ffn"], "ffn_conv_b": gl["ffn"]["ffn_conv_b"]}
    share.update({n: gl["mixer"][0][n] for n in _MLA_GAINS})
    share.update({n: gl["mixer"][1][n] for n in _GQA_GAINS})
    whole = {n: jnp.where(me == 0, gl[n], 0.0) for n in _SUMMED}
    order = _SUMMED + _SHARED
    parts = [whole[n] if n in whole else share[n] for n in order]
    sizes = [p.size for p in parts]
    taps_g = gl["ffn"]["ffn_conv_w"]
    pack_g = jnp.concatenate([p.reshape(-1) for p in parts] + [taps_g.reshape(-1)])
    pad = (-pack_g.size // LANES) % SUBLANES * LANES
    pack_g = jnp.concatenate([pack_g, jnp.zeros((pad,), F32)])
    land = _gather_rows(_rows128(pack_g), "smallgrad_gather")

    def pack(src):
        flat = jnp.concatenate([src[n].reshape(-1) for n in order] + [jnp.zeros((taps_g.size + pad,), F32)])
        return _rows128(flat)

    outs = _adamw_call(pack(wts), pack(mom), pack(var), land, "adamw_small", True)
    at = 0
    for n, size in zip(order, sizes):
        put(n, [o.reshape(-1)[at:at + size] for o in outs], wts[n].shape)
        at += size
    taps_sum = outs[0].reshape(-1)[at:at + taps_g.size].reshape(taps_g.shape)
    my_taps = lax.dynamic_slice_in_dim(taps_sum, me * ffn_conv_w.shape[-1], ffn_conv_w.shape[-1], axis=2)
    outs = _adamw_call(_rows128(ffn_conv_w), _rows128(m_ffn_conv_w), _rows128(v_ffn_conv_w), _rows128(my_taps),
                       "adamw_conv_w", False)
    put("ffn_conv_w", outs, ffn_conv_w.shape)

    return (loss, gl["x"][None], *[grads[n] for n in _NAMES], *[deltas[n] for n in _NAMES],
            *[new_m[n] for n in _NAMES], *[new_v[n] for n in _NAMES])
```

```python
import functools

import jax
import jax.numpy as jnp
from jax import lax
from jax.experimental import pallas as pl
from jax.experimental.pallas import tpu as pltpu

F32 = jnp.float32
BF16 = jnp.bfloat16

EPS = 1e-6
ROPE_BASE = 10000.0
GRID_W = 64
N_MOD = 6
MLA_NOPE = 128
MLA_ROPE = 64
MLA_V = 128
GQA_HEAD_DIM = 128
GQA_KV_HEADS = 4
ADAM_LR = 0.001
ADAM_B1 = 0.9
ADAM_B2 = 0.999
ADAM_EPS = 1e-08
ADAM_WD = 0.01
ADAM_STEP = 10
N_DEV = 8
MOD_ROWS = 16

VMEM_LIMIT = 48 * 1024 * 1024
MATMUL_VMEM = 40 * 1024 * 1024
LANES = 128
SUBLANES = 8


def _pc(body, **kw):
    return pl.pallas_call(body, **kw)


def _cp(*sem):
    return pltpu.CompilerParams(dimension_semantics=sem if sem else None, vmem_limit_bytes=VMEM_LIMIT)


def _tile(n, cap, mult=LANES):
    if n <= cap:
        return n
    t = (cap // mult) * mult
    while t >= mult:
        if n % t == 0:
            return t
        t -= mult
    return n


def _matmul(a, b, *, ta=False, tb=False, name, out_dtype=F32):
    assert not (ta and tb)
    kd, m = a.shape if ta else a.shape[::-1]
    n, kb = b.shape if tb else b.shape[::-1]
    assert kd == kb, (a.shape, b.shape, ta, tb)
    tm, tn = _tile(m, 1024), _tile(n, 1024)
    out_bytes = jnp.dtype(out_dtype).itemsize

    def vmem_need(t):
        return 2 * t * (tm * a.dtype.itemsize + tn * b.dtype.itemsize) + tm * tn * (2 * out_bytes + 4)

    tk = next(t for t in (_tile(kd, 2048), _tile(kd, 1024), _tile(kd, 512)) if vmem_need(t) <= MATMUL_VMEM or t <= 512)
    nk = kd // tk
    if ta:
        dims = (((0,), (0,)), ((), ()))
        a_spec = pl.BlockSpec((tk, tm), lambda i, j, k: (k, i))
    else:
        dims = (((1,), (1 if tb else 0,)), ((), ()))
        a_spec = pl.BlockSpec((tm, tk), lambda i, j, k: (i, k))
    if tb:
        b_spec = pl.BlockSpec((tn, tk), lambda i, j, k: (j, k))
    else:
        b_spec = pl.BlockSpec((tk, tn), lambda i, j, k: (k, j))

    def product(a_ref, b_ref):
        return lax.dot_general(a_ref[...].astype(BF16), b_ref[...].astype(BF16), dims, preferred_element_type=F32)

    def body_one(a_ref, b_ref, o_ref):
        o_ref[...] = product(a_ref, b_ref).astype(o_ref.dtype)

    def body(a_ref, b_ref, o_ref, acc_ref):
        k = pl.program_id(2)

        @pl.when(k == 0)
        def _():
            acc_ref[...] = jnp.zeros_like(acc_ref)

        acc_ref[...] += product(a_ref, b_ref)

        @pl.when(k == nk - 1)
        def _():
            o_ref[...] = acc_ref[...].astype(o_ref.dtype)

    return _pc(
        body_one if nk == 1 else body, name=name, grid=(m // tm, n // tn, nk),
        in_specs=[a_spec, b_spec],
        out_specs=pl.BlockSpec((tm, tn), lambda i, j, k: (i, j)),
        out_shape=jax.ShapeDtypeStruct((m, n), out_dtype),
        scratch_shapes=[] if nk == 1 else [pltpu.VMEM((tm, tn), F32)],
        compiler_params=_cp("parallel", "parallel", "arbitrary"),
    )(a, b)


def _mm(name, out_dtype=F32):
    @jax.custom_vjp
    def f(a, w):
        return _matmul(a, w, name=name + "_fwd", out_dtype=out_dtype)

    def fwd(a, w):
        return f(a, w), (a, w)

    def bwd(res, dc):
        a, w = res
        return (_matmul(dc, w, tb=True, name=name + "_da", out_dtype=a.dtype),
                _matmul(a, dc, ta=True, name=name + "_dw", out_dtype=w.dtype))

    f.defvjp(fwd, bwd)
    return f


def _row_tile(r, d):
    return _tile(r, max(SUBLANES, min(4096, (2 << 20) // (4 * d))), SUBLANES)


def _norm_fwd_call(x, g, sc, sh, name, out_dtype=F32):
    r, d = x.shape
    tr = _row_tile(r, d)
    mod = sc is not None

    def body(*refs):
        if mod:
            x_ref, g_ref, sc_ref, sh_ref, y_ref = refs
        else:
            x_ref, g_ref, y_ref = refs
        xv = x_ref[...]
        y = xv * lax.rsqrt(jnp.mean(xv * xv, axis=-1, keepdims=True) + EPS) * g_ref[...]
        if mod:
            y = y * (1.0 + sc_ref[...]) + sh_ref[...]
        y_ref[...] = y.astype(y_ref.dtype)

    vec = pl.BlockSpec((1, d), lambda i: (0, 0))
    args = [x, g.reshape(1, d)] + ([sc.reshape(1, d), sh.reshape(1, d)] if mod else [])
    return _pc(
        body, name=name, grid=(r // tr,),
        in_specs=[pl.BlockSpec((tr, d), lambda i: (i, 0))] + [vec] * (len(args) - 1),
        out_specs=pl.BlockSpec((tr, d), lambda i: (i, 0)),
        out_shape=jax.ShapeDtypeStruct((r, d), out_dtype),
        compiler_params=_cp("parallel"),
    )(*args)


def _norm_bwd_call(x, g, sc, dy, name, add=None):
    r, d = x.shape
    tr = _row_tile(r, d)
    mod = sc is not None

    def body(*refs):
        add_ref = None
        if add is not None:
            add_ref, refs = refs[0], refs[1:]
        if mod:
            x_ref, g_ref, sc_ref, dy_ref, dx_ref, dg_ref, dsc_ref, dsh_ref = refs
        else:
            x_ref, g_ref, dy_ref, dx_ref, dg_ref = refs

        @pl.when(pl.program_id(0) == 0)
        def _():
            dg_ref[...] = jnp.zeros_like(dg_ref)
            if mod:
                dsc_ref[...] = jnp.zeros_like(dsc_ref)
                dsh_ref[...] = jnp.zeros_like(dsh_ref)

        xv = x_ref[...]
        gv = g_ref[...]
        dyv = dy_ref[...].astype(F32)
        rs = lax.rsqrt(jnp.mean(xv * xv, axis=-1, keepdims=True) + EPS)
        xh = xv * rs
        if mod:
            dsc_ref[...] += jnp.sum(dyv * (xh * gv), axis=0, keepdims=True)
            dsh_ref[...] += jnp.sum(dyv, axis=0, keepdims=True)
            t = dyv * (1.0 + sc_ref[...])
        else:
            t = dyv
        dg_ref[...] += jnp.sum(t * xh, axis=0, keepdims=True)
        dxh = t * gv
        dx = rs * (dxh - xh * jnp.mean(dxh * xh, axis=-1, keepdims=True))
        dx_ref[...] = dx if add_ref is None else add_ref[...] + dx

    vec = pl.BlockSpec((1, d), lambda i: (0, 0))
    blk = pl.BlockSpec((tr, d), lambda i: (i, 0))
    args = [x, g.reshape(1, d)] + ([sc.reshape(1, d)] if mod else []) + [dy]
    n_vec_out = 3 if mod else 1
    in_specs = [blk] + [vec] * (len(args) - 2) + [blk]
    if add is not None:
        args, in_specs = [add] + args, [blk] + in_specs
    outs = _pc(
        body, name=name, grid=(r // tr,),
        in_specs=in_specs,
        out_specs=[blk] + [vec] * n_vec_out,
        out_shape=[jax.ShapeDtypeStruct((r, d), F32)] + [jax.ShapeDtypeStruct((1, d), F32)] * n_vec_out,
        compiler_params=_cp("arbitrary"),
    )(*args)
    return outs


def _norm(name):
    @jax.custom_vjp
    def f(x, g):
        return _norm_fwd_call(x, g, None, None, name + "_fwd", out_dtype=BF16)

    def fwd(x, g):
        return f(x, g), (x, g)

    def bwd(res, dy):
        x, g = res
        dx, dg = _norm_bwd_call(x, g, None, dy, name + "_bwd")
        return dx, dg.reshape(g.shape)

    f.defvjp(fwd, bwd)
    return f


def _modulate(name):
    @jax.custom_vjp
    def f(x, g, sh, sc):
        return x, _norm_fwd_call(x, g, sc, sh, name + "_fwd", out_dtype=BF16)

    def fwd(x, g, sh, sc):
        return f(x, g, sh, sc), (x, g, sc)

    def bwd(res, cts):
        x, g, sc = res
        dx_res, dy = cts
        dx, dg, dsc, dsh = _norm_bwd_call(x, g, sc, dy, name + "_bwd", add=dx_res)
        return dx, dg.reshape(g.shape), dsh.reshape(g.shape), dsc.reshape(g.shape)

    f.defvjp(fwd, bwd)
    return f


def _gres(name):
    def fwd_call(x, gate, y):
        r, d = x.shape
        tr = _row_tile(r, d)

        def body(x_ref, g_ref, y_ref, o_ref):
            o_ref[...] = x_ref[...] + g_ref[...] * y_ref[...]

        blk = pl.BlockSpec((tr, d), lambda i: (i, 0))
        return _pc(
            body, name=name + "_fwd", grid=(r // tr,),
            in_specs=[blk, pl.BlockSpec((1, d), lambda i: (0, 0)), blk], out_specs=blk,
            out_shape=jax.ShapeDtypeStruct((r, d), F32), compiler_params=_cp("parallel"),
        )(x, gate.reshape(1, d), y)

    def bwd_call(do, gate, y):
        r, d = do.shape
        tr = _row_tile(r, d)

        def body(do_ref, g_ref, y_ref, dy_ref, dg_ref):
            @pl.when(pl.program_id(0) == 0)
            def _():
                dg_ref[...] = jnp.zeros_like(dg_ref)

            dov = do_ref[...]
            dy_ref[...] = g_ref[...] * dov
            dg_ref[...] += jnp.sum(dov * y_ref[...], axis=0, keepdims=True)

        blk = pl.BlockSpec((tr, d), lambda i: (i, 0))
        vec = pl.BlockSpec((1, d), lambda i: (0, 0))
        return _pc(
            body, name=name + "_bwd", grid=(r // tr,),
            in_specs=[blk, vec, blk], out_specs=[blk, vec],
            out_shape=[jax.ShapeDtypeStruct((r, d), F32), jax.ShapeDtypeStruct((1, d), F32)],
            compiler_params=_cp("arbitrary"),
        )(do, gate.reshape(1, d), y)

    @jax.custom_vjp
    def f(x, gate, y):
        return fwd_call(x, gate, y)

    def fwd(x, gate, y):
        return f(x, gate, y), (gate, y)

    def bwd(res, do):
        gate, y = res
        dy, dg = bwd_call(do, gate, y)
        return do, dg.reshape(gate.shape), dy

    f.defvjp(fwd, bwd)
    return f


_NT = (((1,), (1,)), ((), ()))
_NN = (((1,), (0,)), ((), ()))
_TN = (((0,), (0,)), ((), ()))
ATT_TQ = 2048
ATT_BTQ = 1024
ATT_SUB = 256
ATT_TK = 2816
LOG2E = 1.4426950408889634


def _per_head(flat, rows, width, where):
    if flat:
        return pl.BlockSpec((rows, width), where)
    return pl.BlockSpec((1, rows, width), lambda *g: (where(*g)[1], where(*g)[0], 0))


def _tile_of(ref):
    return ref if len(ref.shape) == 2 else ref.at[0]


def _head_dims(z, heads):
    if z.ndim == 2:
        return True, z.shape[0], z.shape[1] // heads
    return False, z.shape[1], z.shape[2]


def _attn_fwd_call(q, k, v, heads, scale, name):
    h, hkv = heads
    fq, sq, dq = _head_dims(q, h)
    fk, t, _ = _head_dims(k, hkv)
    dv = v.shape[1] // hkv
    dvx = dv + LANES
    grp = h // hkv
    tq, tk = _tile(sq, ATT_TQ), _tile(t, ATT_TK)
    sub = min(ATT_SUB, tq)
    nk = t // tk
    c = scale * LOG2E

    def body(q_ref, k_ref, v_ref, o_ref, lse_ref, m_s, acc_s):
        j = pl.program_id(2)

        @pl.when(j == 0)
        def _():
            m_s[...] = jnp.full_like(m_s, -jnp.inf)
            acc_s[...] = jnp.zeros_like(acc_s)

        q_t, kv = _tile_of(q_ref), _tile_of(k_ref)[...]
        one = (lax.broadcasted_iota(jnp.int32, (tk, LANES), 1) == 0).astype(BF16)
        vv = jnp.concatenate([v_ref[...], one], axis=1)
        for r in range(tq // sub):
            rows = pl.ds(r * sub, sub)
            s = lax.dot_general(q_t[rows, :], kv, _NT, preferred_element_type=F32) * c
            m_prev = m_s[rows, :]
            m_new = jnp.maximum(m_prev, jnp.max(s, axis=-1, keepdims=True))
            p = jnp.exp2(s - m_new)
            acc_s[rows, :] = jnp.exp2(m_prev - m_new) * acc_s[rows, :] + lax.dot_general(
                p.astype(BF16), vv, _NN, preferred_element_type=F32)
            m_s[rows, :] = m_new

        @pl.when(j == nk - 1)
        def _():
            acc = acc_s[...]
            l = acc[:, dv:dv + 1]
            o_ref[...] = (acc[:, :dv] / l).astype(o_ref.dtype)
            lse_ref[0] = m_s[...] + jnp.log(l) * LOG2E

    return _pc(
        body, name=name, grid=(h, sq // tq, nk),
        in_specs=[_per_head(fq, tq, dq, lambda hh, i, j: (i, hh)),
                  _per_head(fk, tk, dq, lambda hh, i, j: (j, hh // grp)),
                  pl.BlockSpec((tk, dv), lambda hh, i, j: (j, hh // grp))],
        out_specs=[pl.BlockSpec((tq, dv), lambda hh, i, j: (i, hh)),
                   pl.BlockSpec((1, tq, 1), lambda hh, i, j: (hh, i, 0))],
        out_shape=[jax.ShapeDtypeStruct((sq, h * dv), BF16), jax.ShapeDtypeStruct((h, sq, 1), F32)],
        scratch_shapes=[pltpu.VMEM((tq, 1), F32), pltpu.VMEM((tq, dvx), F32)],
        compiler_params=_cp("parallel", "parallel", "arbitrary"),
    )(q, k, v)


def _attn_bwd_call(q, k, v, o, do, lse, heads, scale, name):
    h, hkv = heads
    fq, sq, dq = _head_dims(q, h)
    fk, t, _ = _head_dims(k, hkv)
    dv = v.shape[1] // hkv
    grp = h // hkv
    tq, tk = _tile(sq, ATT_BTQ), _tile(t, ATT_TK)
    sub = min(ATT_SUB, tq)
    nq = sq // tq
    c = scale * LOG2E

    def body(q_ref, k_ref, v_ref, o_ref, do_ref, lse_ref, dq_ref, dk_ref, dv_ref, dk_s, dv_s):
        j = pl.program_id(1)
        i = pl.program_id(2)

        @pl.when(i == 0)
        def _():
            dk_s[...] = jnp.zeros_like(dk_s)
            dv_s[...] = jnp.zeros_like(dv_s)

        q_t, dq_t = _tile_of(q_ref), _tile_of(dq_ref)
        kv, vv = _tile_of(k_ref)[...], v_ref[...]
        for r in range(tq // sub):
            rows = pl.ds(r * sub, sub)
            qv, dov = q_t[rows, :], do_ref[rows, :]
            s = lax.dot_general(qv, kv, _NT, preferred_element_type=F32) * c
            p = jnp.exp2(s - lse_ref[0, rows, :])
            delta = jnp.sum(dov.astype(F32) * o_ref[rows, :].astype(F32), axis=-1, keepdims=True)
            dv_s[...] += lax.dot_general(p.astype(BF16), dov, _TN, preferred_element_type=F32)
            dp = lax.dot_general(dov, vv, _NT, preferred_element_type=F32)
            ds = (p * (dp - delta) * scale).astype(BF16)
            dk_s[...] += lax.dot_general(ds, qv, _TN, preferred_element_type=F32)
            dq_blk = lax.dot_general(ds, kv, _NN, preferred_element_type=F32)
            out_rows = pl.ds(pl.multiple_of(i * tq + r * sub, sub), sub)

            @pl.when(j == 0)
            def _():
                dq_t[out_rows, :] = dq_blk

            @pl.when(j > 0)
            def _():
                dq_t[out_rows, :] += dq_blk

        @pl.when(i == nq - 1)
        def _():
            _tile_of(dk_ref)[...] = dk_s[...]
            dv_ref[...] = dv_s[...]

    def shape_like(flat, rows, width):
        return (rows, h * width) if flat else (h, rows, width)

    return _pc(
        body, name=name, grid=(h, t // tk, nq),
        in_specs=[_per_head(fq, tq, dq, lambda hh, j, i: (i, hh)),
                  _per_head(fk, tk, dq, lambda hh, j, i: (j, hh // grp)),
                  pl.BlockSpec((tk, dv), lambda hh, j, i: (j, hh // grp)),
                  pl.BlockSpec((tq, dv), lambda hh, j, i: (i, hh)),
                  pl.BlockSpec((tq, dv), lambda hh, j, i: (i, hh)),
                  pl.BlockSpec((1, tq, 1), lambda hh, j, i: (hh, i, 0))],
        out_specs=[_per_head(fq, sq, dq, lambda hh, j, i: (0, hh)),
                   _per_head(fk, tk, dq, lambda hh, j, i: (j, hh)),
                   pl.BlockSpec((tk, dv), lambda hh, j, i: (j, hh))],
        out_shape=[jax.ShapeDtypeStruct(shape_like(fq, sq, dq), F32), jax.ShapeDtypeStruct(shape_like(fk, t, dq), F32),
                   jax.ShapeDtypeStruct((t, h * dv), F32)],
        scratch_shapes=[pltpu.VMEM((tk, dq), F32), pltpu.VMEM((tk, dv), F32)],
        compiler_params=_cp("parallel", "arbitrary", "arbitrary"),
    )(q, k, v, o, do, lse)


def _attention(name, scale):
    def per_head(z):
        rows, hh, d = z.shape
        return z.reshape(rows, hh * d).astype(BF16) if d % LANES == 0 else z.transpose(1, 0, 2).astype(BF16)

    def rows_first(g, hh):
        return g.reshape(g.shape[0], hh, -1) if g.ndim == 2 else g.transpose(1, 0, 2)

    @jax.custom_vjp
    def f(q, k, v):
        return fwd(q, k, v)[0]

    def fwd(q, k, v):
        heads = (q.shape[1], k.shape[1])
        qx, kx, vx = per_head(q), per_head(k), v.reshape(v.shape[0], -1).astype(BF16)
        o, lse = _attn_fwd_call(qx, kx, vx, heads, scale, name + "_fwd")
        like = tuple(jnp.zeros((0,), z.dtype) for z in (q, k, v))
        return o, (qx, kx, vx, o, lse, like)

    def bwd(res, do):
        qx, kx, vx, o, lse, like = res
        h, t = lse.shape[0], vx.shape[0]
        hkv = vx.shape[1] // (o.shape[1] // h)
        dq, dk, dvv = _attn_bwd_call(qx, kx, vx, o, do, lse, (h, hkv), scale, name + "_bwd")
        dq, dk, dvv = rows_first(dq, h), rows_first(dk, h), rows_first(dvv, h)
        if h != hkv:
            dk = dk.reshape(t, hkv, h // hkv, -1).sum(axis=2)
            dvv = dvv.reshape(t, hkv, h // hkv, -1).sum(axis=2)
        return tuple(d.astype(z.dtype) for d, z in zip((dq, dk, dvv), like))

    f.defvjp(fwd, bwd)
    return f


CONV_TF = 128
CONV_ROWS = 256


def _conv_neighbours(ref, c, r0, cur, row, nchunks, s, rc):
    halo = SUBLANES * (4 // ref.dtype.itemsize)
    prev = ref[pl.ds(pl.multiple_of(jnp.maximum(r0 - halo, 0), halo), halo), :].astype(F32)
    nxt = ref[pl.ds(pl.multiple_of(jnp.minimum(r0 + rc, s - halo), halo), halo), :].astype(F32)
    prow = jnp.where(c > 0, prev[halo - 1:halo, :], 0.0)
    nrow = jnp.where(c < nchunks - 1, nxt[0:1, :], 0.0)
    before = jnp.where(row == 0, prow, pltpu.roll(cur, 1, 0))
    after = jnp.where(row == rc - 1, nrow, pltpu.roll(cur, rc - 1, 0))
    return before, after


def _conv_fwd_call(u, w, b, name):
    s, f2 = u.shape
    f = f2 // 2
    tf = CONV_TF
    nf = f // tf
    rc = min(CONV_ROWS, s)
    nchunks = s // rc

    def body(g_ref, v_ref, w_ref, b_ref, o_ref):
        w0, w1, w2, bv = w_ref[0:1, :], w_ref[1:2, :], w_ref[2:3, :], b_ref[...]
        row = lax.broadcasted_iota(jnp.int32, (rc, tf), 0)

        def chunk(c, carry):
            r0 = pl.multiple_of(c * rc, rc)
            cur = g_ref[pl.ds(r0, rc), :].astype(F32)
            before, after = _conv_neighbours(g_ref, c, r0, cur, row, nchunks, s, rc)
            gc = before * w0 + cur * w1 + after * w2 + bv
            val = v_ref[pl.ds(r0, rc), :].astype(F32)
            o_ref[pl.ds(r0, rc), :] = (gc * jax.nn.sigmoid(gc) * val).astype(o_ref.dtype)
            return carry

        lax.fori_loop(0, nchunks, chunk, 0)

    return _pc(
        body, name=name, grid=(nf,),
        in_specs=[pl.BlockSpec((s, tf), lambda j: (0, j)), pl.BlockSpec((s, tf), lambda j: (0, j + nf)),
                  pl.BlockSpec((3, tf), lambda j: (0, j)), pl.BlockSpec((1, tf), lambda j: (0, j))],
        out_specs=pl.BlockSpec((s, tf), lambda j: (0, j)),
        out_shape=jax.ShapeDtypeStruct((s, f), u.dtype),
        compiler_params=_cp("parallel"),
    )(u, u, w, b.reshape(1, f))


def _conv_bwd_call(u, w, b, da, name):
    s, f2 = u.shape
    f = f2 // 2
    tf = CONV_TF
    nf = f // tf
    rc = min(CONV_ROWS, s)
    nchunks = s // rc

    def body(g_ref, v_ref, da_ref, w_ref, b_ref, dg_ref, dv_ref, dw_ref, db_ref, dgc_s):
        w0, w1, w2, bv = w_ref[0:1, :], w_ref[1:2, :], w_ref[2:3, :], b_ref[...]
        row = lax.broadcasted_iota(jnp.int32, (rc, tf), 0)

        def chunk1(c, carry):
            a0, a1, a2, ab = carry
            r0 = pl.multiple_of(c * rc, rc)
            cur = g_ref[pl.ds(r0, rc), :].astype(F32)
            before, after = _conv_neighbours(g_ref, c, r0, cur, row, nchunks, s, rc)
            gc = before * w0 + cur * w1 + after * w2 + bv
            sig = jax.nn.sigmoid(gc)
            dav = da_ref[pl.ds(r0, rc), :].astype(F32)
            dv_ref[pl.ds(r0, rc), :] = (dav * (gc * sig)).astype(dv_ref.dtype)
            dgc = dav * v_ref[pl.ds(r0, rc), :].astype(F32) * (sig * (1.0 + gc * (1.0 - sig)))
            dgc_s[pl.ds(r0, rc), :] = dgc
            return (a0 + jnp.sum(dgc * before, axis=0, keepdims=True),
                    a1 + jnp.sum(dgc * cur, axis=0, keepdims=True),
                    a2 + jnp.sum(dgc * after, axis=0, keepdims=True),
                    ab + jnp.sum(dgc, axis=0, keepdims=True))

        z = jnp.zeros((1, tf), F32)
        a0, a1, a2, ab = lax.fori_loop(0, nchunks, chunk1, (z, z, z, z))
        dw_ref[0:1, :] = a0
        dw_ref[1:2, :] = a1
        dw_ref[2:3, :] = a2
        db_ref[...] = ab

        def chunk2(c, carry):
            r0 = pl.multiple_of(c * rc, rc)
            cur = dgc_s[pl.ds(r0, rc), :]
            before, after = _conv_neighbours(dgc_s, c, r0, cur, row, nchunks, s, rc)
            dg_ref[pl.ds(r0, rc), :] = (after * w0 + cur * w1 + before * w2).astype(dg_ref.dtype)
            return carry

        lax.fori_loop(0, nchunks, chunk2, 0)

    col = pl.BlockSpec((s, tf), lambda j: (0, j))
    return _pc(
        body, name=name, grid=(nf,),
        in_specs=[col, pl.BlockSpec((s, tf), lambda j: (0, j + nf)), col,
                  pl.BlockSpec((3, tf), lambda j: (0, j)), pl.BlockSpec((1, tf), lambda j: (0, j))],
        out_specs=[col, col, pl.BlockSpec((3, tf), lambda j: (0, j)), pl.BlockSpec((1, tf), lambda j: (0, j))],
        out_shape=[jax.ShapeDtypeStruct((s, f), u.dtype), jax.ShapeDtypeStruct((s, f), u.dtype),
                   jax.ShapeDtypeStruct((3, f), F32), jax.ShapeDtypeStruct((1, f), F32)],
        scratch_shapes=[pltpu.VMEM((s, tf), F32)],
        compiler_params=_cp("parallel"),
    )(u, u, da, w, b.reshape(1, f))


def _convgate(name):
    @jax.custom_vjp
    def f(u, w, b):
        return _conv_fwd_call(u, w, b, name + "_fwd")

    def fwd(u, w, b):
        return f(u, w, b), (u, w, b)

    def bwd(res, da):
        u, w, b = res
        dg, dv, dw, db = _conv_bwd_call(u, w, b, da, name + "_bwd")
        return jnp.concatenate([dg, dv], axis=-1), dw, db.reshape(b.shape)

    f.defvjp(fwd, bwd)
    return f


def _loss_call(y, tgt):
    r, d = y.shape
    tr = _row_tile(r, d)

    def body(y_ref, t_ref, dy_ref, part_ref):
        @pl.when(pl.program_id(0) == 0)
        def _():
            part_ref[...] = jnp.zeros_like(part_ref)

        diff = y_ref[...] - t_ref[...]
        dy_ref[...] = diff / d
        part_ref[...] += jnp.sum(diff * diff, axis=0, keepdims=True)

    blk = pl.BlockSpec((tr, d), lambda i: (i, 0))
    vec = pl.BlockSpec((1, d), lambda i: (0, 0))
    dy, part = _pc(
        body, name="loss_head", grid=(r // tr,), in_specs=[blk, blk], out_specs=[blk, vec],
        out_shape=[jax.ShapeDtypeStruct((r, d), F32), jax.ShapeDtypeStruct((1, d), F32)],
        compiler_params=_cp("arbitrary"),
    )(y, tgt)
    return 0.5 * (jnp.sum(part) / d), dy


def _adamw_call(w, m, v, g, name, summed):
    r, c = w.shape
    tr = _tile(r, max(2 * SUBLANES, (1 << 18) // c), 2 * SUBLANES)
    slabs = g.shape[0] if summed else 1

    def body(g_ref, w_ref, m_ref, v_ref, go_ref, d_ref, mo_ref, vo_ref):
        if summed:
            gv = g_ref[0].astype(F32)
            for slab in range(1, slabs):
                gv = gv + g_ref[slab].astype(F32)
        else:
            gv = g_ref[...]
        mn = ADAM_B1 * m_ref[...] + (1.0 - ADAM_B1) * gv
        vn = ADAM_B2 * v_ref[...] + (1.0 - ADAM_B2) * (gv * gv)
        m_hat = mn / (1.0 - ADAM_B1 ** ADAM_STEP)
        v_hat = vn / (1.0 - ADAM_B2 ** ADAM_STEP)
        go_ref[...] = gv
        d_ref[...] = -ADAM_LR * (m_hat / (jnp.sqrt(v_hat) + ADAM_EPS) + ADAM_WD * w_ref[...])
        mo_ref[...] = mn
        vo_ref[...] = vn

    blk = pl.BlockSpec((tr, c), lambda i: (i, 0))
    gblk = pl.BlockSpec((slabs, tr, c), lambda i: (0, i, 0)) if summed else blk
    return _pc(
        body, name=name, grid=(r // tr,), in_specs=[gblk, blk, blk, blk], out_specs=[blk] * 4,
        out_shape=[jax.ShapeDtypeStruct((r, c), F32)] * 4, compiler_params=_cp("parallel"),
    )(g, w, m, v)


_ANY = pl.BlockSpec(memory_space=pl.ANY)
MESH = pl.DeviceIdType.MESH


def _window(ref, axis, idx, size):
    if axis == 0:
        return ref.at[pl.ds(idx * size, size), :]
    return ref.at[:, pl.ds(idx * size, size)]


def _allgather(shards, axes, name):
    n_p = len(shards)
    layers = [s.shape[0] for s in shards]
    out_shape = []
    for s, ax in zip(shards, axes):
        _, k, n = s.shape
        full = (k * N_DEV, n) if ax == 0 else (k, n * N_DEV)
        out_shape += [jax.ShapeDtypeStruct(full, s.dtype)] * s.shape[0]
    n_out = len(out_shape)

    def body(*refs):
        x_refs = refs[:n_p]
        flat = refs[n_p:n_p + n_out]
        send_sems, recv_sems, local_sems = refs[n_p + n_out:]
        outs, at = [], 0
        for cnt in layers:
            outs.append(flat[at:at + cnt])
            at += cnt
        x, y, c = lax.axis_index("x"), lax.axis_index("y"), lax.axis_index("c")
        me, sibling = (x, y, c), (x, y, 1 - c)
        chips = [(1 - x, y), (x, 1 - y), (1 - x, 1 - y)]

        def lin(d):
            return 4 * d[0] + 2 * d[1] + d[2]

        def copies(p, sem, block, to, from_shard):
            size = shards[p].shape[1 + axes[p]]
            res = []
            for l in range(layers[p]):
                dst = _window(outs[p][l], axes[p], lin(block), size)
                res.append(pltpu.make_async_remote_copy(
                    src_ref=x_refs[p].at[l] if from_shard else dst, dst_ref=dst,
                    send_sem=send_sems.at[p, sem], recv_sem=recv_sems.at[p, sem], device_id=to, device_id_type=MESH))
            return res

        def drained(p, sem):
            return pltpu.make_async_remote_copy(
                src_ref=x_refs[p], dst_ref=x_refs[p], send_sem=send_sems.at[p, sem], recv_sem=recv_sems.at[p, sem],
                device_id=me, device_id_type=MESH)

        for p in range(n_p):
            size = shards[p].shape[1 + axes[p]]
            for l in range(layers[p]):
                pltpu.make_async_copy(x_refs[p].at[l], _window(outs[p][l], axes[p], lin(me), size), local_sems.at[p]).start()
            for cp in copies(p, 0, me, sibling, True):
                cp.start()
            for j, chip in enumerate(chips):
                for cp in copies(p, 1 + j, me, (*chip, c), True):
                    cp.start()
        for p in range(n_p):
            for j, chip in enumerate(chips):
                drained(p, 1 + j).wait_recv()
                for cp in copies(p, 4 + j, (*chip, c), sibling, False):
                    cp.start()
        for p in range(n_p):
            for sem in (0, 4, 5, 6):
                drained(p, sem).wait_recv()
            for sem in range(7):
                drained(p, sem).wait_send()
            pltpu.make_async_copy(x_refs[p], x_refs[p], local_sems.at[p]).wait()

    flat = _pc(
        body, name=name, in_specs=[_ANY] * n_p, out_specs=[_ANY] * n_out, out_shape=out_shape,
        scratch_shapes=[pltpu.SemaphoreType.DMA((n_p, 7)), pltpu.SemaphoreType.DMA((n_p, 7)),
                        pltpu.SemaphoreType.DMA((n_p,))],
    )(*shards)
    res, at = [], 0
    for cnt in layers:
        res.append(list(flat[at:at + cnt]))
        at += cnt
    return res


N_CHIP = 4


def _pair_exchange(grads, axes, name):
    n_p = len(grads)
    layers = [len(g) for g in grads]
    n_in = sum(layers)
    blocks = []
    for g, ax in zip(grads, axes):
        kk, nn = g[0].shape
        blocks.append((kk // N_DEV, nn) if ax == 0 else (kk, nn // N_DEV))
    out_shape = [jax.ShapeDtypeStruct((N_CHIP, cnt) + blk, g[0].dtype) for g, cnt, blk in zip(grads, layers, blocks)]

    def body(*refs):
        flat = refs[:n_in]
        theirs = refs[n_in:n_in + n_p]
        send_sems, recv_sems = refs[n_in + n_p:]
        g_refs, at = [], 0
        for cnt in layers:
            g_refs.append(flat[at:at + cnt])
            at += cnt
        x, y, c = lax.axis_index("x"), lax.axis_index("y"), lax.axis_index("c")
        for p in range(n_p):
            size = blocks[p][axes[p]]
            for q in range(N_CHIP):
                for l in range(layers[p]):
                    pltpu.make_async_remote_copy(
                        src_ref=_window(g_refs[p][l], axes[p], 2 * q + (1 - c), size), dst_ref=theirs[p].at[q, l],
                        send_sem=send_sems.at[p], recv_sem=recv_sems.at[p],
                        device_id=(x, y, 1 - c), device_id_type=MESH).start()
        for p in range(n_p):
            pltpu.make_async_remote_copy(src_ref=theirs[p], dst_ref=theirs[p], send_sem=send_sems.at[p],
                                         recv_sem=recv_sems.at[p], device_id=(x, y, c), device_id_type=MESH).wait()

    return _pc(
        body, name=name, in_specs=[_ANY] * n_in, out_specs=[_ANY] * n_p, out_shape=out_shape,
        scratch_shapes=[pltpu.SemaphoreType.DMA((n_p,)), pltpu.SemaphoreType.DMA((n_p,))],
    )(*[g for gl in grads for g in gl])


def _own_blocks(layers, axis, core):
    size = layers[0].shape[axis] // N_DEV
    return jnp.stack([jnp.stack([lax.dynamic_slice_in_dim(g, (2 * q + core) * size, size, axis=axis) for g in layers])
                      for q in range(N_CHIP)])


def _add_pairs(a, b, name):
    cols = a.shape[-1]
    a2, b2 = a.reshape(-1, cols), b.reshape(-1, cols)
    r = a2.shape[0]
    tr = _tile(r, max(2 * SUBLANES, (1 << 20) // cols), 2 * SUBLANES)

    def body(a_ref, b_ref, o_ref):
        o_ref[...] = (a_ref[...].astype(F32) + b_ref[...].astype(F32)).astype(o_ref.dtype)

    blk = pl.BlockSpec((tr, cols), lambda i: (i, 0))
    return _pc(body, name=name, grid=(r // tr,), in_specs=[blk, blk], out_specs=blk,
               out_shape=jax.ShapeDtypeStruct((r, cols), a.dtype), compiler_params=_cp("parallel"))(a2, b2).reshape(a.shape)


def _chip_exchange(parts, name):
    n_p = len(parts)

    def body(*refs):
        srcs = refs[:n_p]
        lands = refs[n_p:2 * n_p]
        send_sems, recv_sems, local_sems = refs[2 * n_p:]
        x, y, c = lax.axis_index("x"), lax.axis_index("y"), lax.axis_index("c")
        my_chip = 2 * x + y
        for p in range(n_p):
            pltpu.make_async_copy(srcs[p].at[my_chip], lands[p].at[my_chip], local_sems.at[p]).start()
            for rel in range(1, N_CHIP):
                px = 1 - x if rel & 2 else x
                py = 1 - y if rel & 1 else y
                pltpu.make_async_remote_copy(
                    src_ref=srcs[p].at[2 * px + py], dst_ref=lands[p].at[my_chip],
                    send_sem=send_sems.at[p, rel - 1], recv_sem=recv_sems.at[p, rel - 1],
                    device_id=(px, py, c), device_id_type=MESH).start()
        for p in range(n_p):
            slab = lands[p].at[0]
            for rel in range(1, N_CHIP):
                pltpu.make_async_remote_copy(
                    src_ref=slab, dst_ref=slab, send_sem=send_sems.at[p, rel - 1], recv_sem=recv_sems.at[p, rel - 1],
                    device_id=(x, y, c), device_id_type=MESH).wait()
            pltpu.make_async_copy(slab, slab, local_sems.at[p]).wait()

    return _pc(
        body, name=name, in_specs=[_ANY] * n_p, out_specs=[_ANY] * n_p,
        out_shape=[jax.ShapeDtypeStruct(z.shape, z.dtype) for z in parts],
        scratch_shapes=[pltpu.SemaphoreType.DMA((n_p, N_CHIP - 1)), pltpu.SemaphoreType.DMA((n_p, N_CHIP - 1)),
                        pltpu.SemaphoreType.DMA((n_p,))],
    )(*parts)


def _gather_rows(a, name):
    r, c = a.shape
    return _allgather([a[None]], [0], name)[0][0].reshape(N_DEV, r, c)


def _conditioning(me):
    def forward(s16, w, b):
        nl, _, cols = w.shape
        part = jnp.concatenate([_matmul(s16, w[i], name="mod_fwd") for i in range(nl)], axis=0)
        full = _gather_rows(part, "mod_gather").reshape(N_DEV, nl, MOD_ROWS, cols)
        return full.transpose(1, 2, 0, 3).reshape(nl, MOD_ROWS, N_DEV * cols) + b[:, None, :]

    @jax.custom_vjp
    def f(s16, w, b):
        return forward(s16, w, b)

    def fwd(s16, w, b):
        return forward(s16, w, b), (s16, w)

    def bwd(res, dm):
        s16, w = res
        nl, dmodel, cols = w.shape
        width = dm.shape[-1]
        mine = lax.dynamic_slice_in_dim(dm, me, 1, axis=1)
        both = jnp.concatenate([mine, dm[:, N_DEV:N_DEV + 1]], axis=1).reshape(nl * 2, width)
        allrows = _gather_rows(both, "dmod_gather").reshape(N_DEV, nl, 2, width)
        total = jnp.concatenate([allrows[:, :, 0].transpose(1, 0, 2), jnp.sum(allrows[:, :, 1], axis=0)[:, None, :],
                                 jnp.zeros((nl, MOD_ROWS - N_DEV - 1, width), F32)], axis=1)
        db = jnp.sum(total, axis=1)
        my_cols = lax.dynamic_slice_in_dim(total, me * cols, cols, axis=2)
        dw = jnp.stack([_matmul(s16, my_cols[i], ta=True, name="mod_dw") for i in range(nl)])
        ds_part = _matmul(my_cols[0], w[0], tb=True, name="mod_ds")
        for i in range(1, nl):
            ds_part = ds_part + _matmul(my_cols[i], w[i], tb=True, name="mod_ds")
        ds = jnp.sum(_gather_rows(ds_part, "dcond_gather"), axis=0)
        return ds, dw, db

    f.defvjp(fwd, bwd)
    return f


def _rope_tables(s, rot_dim):
    t = jnp.arange(s, dtype=jnp.int32)
    rows, cols = t // GRID_W, t % GRID_W
    axis_dim = rot_dim // 2
    inv = jnp.power(ROPE_BASE, -jnp.arange(0, axis_dim, 2, dtype=F32) / axis_dim)
    ang_r = rows.astype(F32)[:, None] * inv
    ang_c = cols.astype(F32)[:, None] * inv
    ang = jnp.concatenate([ang_r, ang_r, ang_c, ang_c], axis=-1)
    return jnp.cos(ang), jnp.sin(ang)


def _rotate_half(z):
    z1, z2 = jnp.split(z, 2, axis=-1)
    return jnp.concatenate([-z2, z1], axis=-1)


def _rope(z, tables):
    if tables is None:
        return z
    cos, sin = tables
    half = z.shape[-1] // 2
    rot = jnp.concatenate([_rotate_half(z[..., :half]), _rotate_half(z[..., half:])], axis=-1)
    return (z * cos[:, None, :] + rot * sin[:, None, :]).astype(z.dtype)


def _head_norm(name, z, g):
    s, h, d = z.shape
    return _norm(name)(z.reshape(s * h, d), g).reshape(s, h, d)


def _mla_q(tag, hx, rope, w):
    s = hx.shape[0]
    cq = _norm(tag + "_cq_norm")(_mm(tag + "_dq")(hx, w["mla_w_dq"]), w["mla_g_dq"])
    q = _mm(tag + "_uq")(cq, w["mla_w_uq"]).reshape(s, -1, MLA_NOPE + MLA_ROPE)
    q_nope = _head_norm(tag + "_qn_norm", q[..., :MLA_NOPE], w["mla_g_q_nope"])
    q_pe = _rope(_head_norm(tag + "_qp_norm", q[..., MLA_NOPE:], w["mla_g_q_pe"]), rope)
    return jnp.concatenate([q_nope, q_pe], axis=-1)


def _mla_kv(tag, hx, rope, w):
    s = hx.shape[0]
    kv_a = _mm(tag + "_dkv")(hx, w["mla_w_dkv"])
    rank = kv_a.shape[-1] - MLA_ROPE
    c_kv = _norm(tag + "_ckv_norm")(kv_a[:, :rank], w["mla_g_dkv"])
    k_pe = _rope(_norm(tag + "_kp_norm")(kv_a[:, rank:], w["mla_g_k_pe"])[:, None, :], rope)
    kv = _mm(tag + "_ukv")(c_kv, w["mla_w_ukv"]).reshape(s, -1, MLA_NOPE + MLA_V)
    heads = kv.shape[1]
    k_nope = _head_norm(tag + "_kn_norm", kv[..., :MLA_NOPE], w["mla_g_k_nope"])
    k = jnp.concatenate([k_nope, jnp.broadcast_to(k_pe, (s, heads, MLA_ROPE))], axis=-1)
    return k, kv[..., MLA_NOPE:]


def _gqa_q(tag, hx, rope, w):
    s = hx.shape[0]
    q = _mm(tag + "_q")(hx, w["gqa_w_q"]).reshape(s, -1, GQA_HEAD_DIM)
    return _rope(_head_norm(tag + "_q_norm", q, w["gqa_g_q"]), rope)


def _gqa_kv(tag, hx, rope, w):
    s = hx.shape[0]
    kv = _mm(tag + "_kv")(hx, w["gqa_w_kv"]).reshape(s, 2, GQA_KV_HEADS, GQA_HEAD_DIM)
    k = _rope(_head_norm(tag + "_k_norm", kv[:, 0], w["gqa_g_k"]), rope)
    return k, kv[:, 1]


def _conv_ffn(tag, hx, w):
    u = _mm(tag + "_up", BF16)(hx, w["ffn_w_up"])
    a = _convgate(tag + "_conv")(u, w["ffn_conv_w"], w["ffn_conv_b"])
    return _mm(tag + "_down")(a, w["ffn_w_down"])


def _forward(leaves, ctx, silu_c_all, me, depth):
    x = leaves["x"]
    s, d = x.shape
    rope_mla = _rope_tables(s, MLA_ROPE)
    rope_gqa = _rope_tables(s, GQA_HEAD_DIM)
    silu_cc = jax.nn.silu(leaves["c_ctx"])
    s16 = jnp.concatenate([silu_c_all, silu_cc[None, :], jnp.zeros((MOD_ROWS - N_DEV - 1, d), F32)], axis=0)
    mods = _conditioning(me)(s16, leaves["w_mod"], leaves["b_mod"])
    for i in range(depth):
        last = i == depth - 1
        w = {k: v[i // 2] for k, v in leaves["mixer"][i % 2].items()}
        w.update({k: v[i] for k, v in leaves["ffn"].items()})
        mod = lax.dynamic_index_in_dim(mods[i], me, axis=0, keepdims=False)
        sh1, sc1, g1, sh2, sc2, g2 = jnp.split(mod, N_MOD)
        csh1, csc1, cg1, csh2, csc2, cg2 = jnp.split(mods[i, N_DEV], N_MOD)
        tag = f"l{i}"
        x, hx = _modulate(tag + "_mix_mod")(x, leaves["norm_mix"][i], sh1, sc1)
        ctx, hc = _modulate(tag + "c_mix_mod")(ctx, leaves["norm_mix"][i], csh1, csc1)
        if i % 2 == 0:
            q_fn, kv_fn, w_o = _mla_q, _mla_kv, w["mla_w_o"]
            rope, scale = rope_mla, 1.0 / float(MLA_NOPE + MLA_ROPE) ** 0.5
        else:
            q_fn, kv_fn, w_o = _gqa_q, _gqa_kv, w["gqa_w_o"]
            rope, scale = rope_gqa, 1.0 / float(GQA_HEAD_DIM) ** 0.5
        k_lat, v_lat = kv_fn(tag, hx, rope, w)
        k_ctx, v_ctx = kv_fn(tag + "c", hc, None, w)
        o = _attention(tag + "_attn", scale)(q_fn(tag, hx, rope, w), jnp.concatenate([k_lat, k_ctx], axis=0),
                                             jnp.concatenate([v_lat, v_ctx], axis=0))
        x = _gres(tag + "_mix_res")(x, g1, _mm(tag + "_o")(o, w_o))
        if not last:
            oc = _attention(tag + "c_attn", scale)(q_fn(tag + "c", hc, None, w), k_ctx, v_ctx)
            ctx = _gres(tag + "c_mix_res")(ctx, cg1, _mm(tag + "c_o")(oc, w_o))
        x, hx = _modulate(tag + "_ffn_mod")(x, leaves["norm_ffn"][i], sh2, sc2)
        x = _gres(tag + "_ffn_res")(x, g2, _conv_ffn(tag, hx, w))
        if not last:
            ctx, hc = _modulate(tag + "c_ffn_mod")(ctx, leaves["norm_ffn"][i], csh2, csc2)
            ctx = _gres(tag + "c_ffn_res")(ctx, cg2, _conv_ffn(tag + "c", hc, w))
    return x


_MLA_BIG = {"mla_w_dq": 0, "mla_w_uq": 1, "mla_w_dkv": 0, "mla_w_ukv": 1, "mla_w_o": 0}
_GQA_BIG = {"gqa_w_q": 0, "gqa_w_kv": 0, "gqa_w_o": 0}
_FFN_BIG = {"ffn_w_up": 1, "ffn_w_down": 0}
_BIG_GROUPS = [("mla", _MLA_BIG), ("gqa", _GQA_BIG), ("ffn_up", {"ffn_w_up": 1}), ("ffn_down", {"ffn_w_down": 0})]
_MLA_GAINS = ["mla_g_dq", "mla_g_q_nope", "mla_g_q_pe", "mla_g_dkv", "mla_g_k_pe", "mla_g_k_nope"]
_GQA_GAINS = ["gqa_g_q", "gqa_g_k"]
_SHARED = ["norm_mix", "norm_ffn"] + _MLA_GAINS + _GQA_GAINS + ["ffn_conv_b"]
_SUMMED = ["c_ctx", "b_mod"]

_NAMES = ["c_ctx", "w_mod", "b_mod", "norm_mix", "norm_ffn", "mla_w_dq", "mla_g_dq", "mla_w_uq", "mla_g_q_nope",
          "mla_g_q_pe", "mla_w_dkv", "mla_g_dkv", "mla_g_k_pe", "mla_w_ukv", "mla_g_k_nope", "mla_w_o", "gqa_w_q",
          "gqa_g_q", "gqa_w_kv", "gqa_g_k", "gqa_w_o", "ffn_w_up", "ffn_conv_w", "ffn_conv_b", "ffn_w_down"]


def _rows128(a):
    return a.reshape(-1, LANES)


def kernel(x, c, ctx, c_ctx, w_mod, b_mod, norm_mix, norm_ffn, mla_w_dq, mla_g_dq, mla_w_uq, mla_g_q_nope, mla_g_q_pe, mla_w_dkv, mla_g_dkv, mla_g_k_pe, mla_w_ukv, mla_g_k_nope, mla_w_o, gqa_w_q, gqa_g_q, gqa_w_kv, gqa_g_k, gqa_w_o, ffn_w_up, ffn_conv_w, ffn_conv_b, ffn_w_down, loss_target, m_c_ctx, m_w_mod, m_b_mod, m_norm_mix, m_norm_ffn, m_mla_w_dq, m_mla_g_dq, m_mla_w_uq, m_mla_g_q_nope, m_mla_g_q_pe, m_mla_w_dkv, m_mla_g_dkv, m_mla_g_k_pe, m_mla_w_ukv, m_mla_g_k_nope, m_mla_w_o, m_gqa_w_q, m_gqa_g_q, m_gqa_w_kv, m_gqa_g_k, m_gqa_w_o, m_ffn_w_up, m_ffn_conv_w, m_ffn_conv_b, m_ffn_w_down, v_c_ctx, v_w_mod, v_b_mod, v_norm_mix, v_norm_ffn, v_mla_w_dq, v_mla_g_dq, v_mla_w_uq, v_mla_g_q_nope, v_mla_g_q_pe, v_mla_w_dkv, v_mla_g_dkv, v_mla_g_k_pe, v_mla_w_ukv, v_mla_g_k_nope, v_mla_w_o, v_gqa_w_q, v_gqa_g_q, v_gqa_w_kv, v_gqa_g_k, v_gqa_w_o, v_ffn_w_up, v_ffn_conv_w, v_ffn_conv_b, v_ffn_w_down):
    args = locals()
    wts = {n: args[n] for n in _NAMES}
    mom = {n: args["m_" + n] for n in _NAMES}
    var = {n: args["v_" + n] for n in _NAMES}
    me = 4 * lax.axis_index("x") + 2 * lax.axis_index("y") + lax.axis_index("c")
    depth = w_mod.shape[0]
    d_model = x.shape[-1]
    d_ff = ffn_conv_b.shape[-1]

    taps = ffn_conv_w.reshape(-1)
    packed = jnp.concatenate([jax.nn.silu(c).reshape(-1), taps])
    packed = jnp.concatenate([packed, jnp.zeros((-packed.size % (SUBLANES * LANES),), F32)])
    got = _gather_rows(_rows128(packed), "cond_gather").reshape(N_DEV, -1)
    silu_c_all = got[:, :d_model]
    conv_w_full = got[:, d_model:d_model + taps.size].reshape(N_DEV, depth, 3, -1)
    conv_w_full = conv_w_full.transpose(1, 2, 0, 3).reshape(depth, 3, d_ff)

    full = {}
    for gname, group in _BIG_GROUPS:
        names = list(group)
        got_w = _allgather([wts[n].astype(BF16) for n in names], [group[n] for n in names], "gather_" + gname)
        full.update(dict(zip(names, got_w)))

    leaves = {
        "x": x[0], "c_ctx": c_ctx, "w_mod": w_mod, "b_mod": b_mod, "norm_mix": norm_mix, "norm_ffn": norm_ffn,
        "mixer": [
            {**{n: full[n] for n in _MLA_BIG}, **{n: wts[n] for n in _MLA_GAINS}},
            {**{n: full[n] for n in _GQA_BIG}, **{n: wts[n] for n in _GQA_GAINS}},
        ],
        "ffn": {"ffn_w_up": full["ffn_w_up"], "ffn_w_down": full["ffn_w_down"], "ffn_conv_w": conv_w_full,
                "ffn_conv_b": ffn_conv_b},
    }
    y, pullback = jax.vjp(lambda lv: _forward(lv, ctx[0], silu_c_all, me, depth), leaves)
    loss_part, dy = _loss_call(y, loss_target[0])
    (gl,) = pullback(dy)
    loss = lax.psum(loss_part, ("x", "y", "c"))

    grads, deltas, new_m, new_v = {}, {}, {}, {}

    def put(n, outs, shape):
        grads[n], deltas[n], new_m[n], new_v[n] = (o.reshape(shape) for o in outs)

    gfull = {**{n: gl["mixer"][0][n] for n in _MLA_BIG}, **{n: gl["mixer"][1][n] for n in _GQA_BIG},
             "ffn_w_up": gl["ffn"]["ffn_w_up"], "ffn_w_down": gl["ffn"]["ffn_w_down"]}
    for gname, group in _BIG_GROUPS:
        names = list(group)
        theirs = _pair_exchange([gfull[n] for n in names], [group[n] for n in names], "pair_" + gname)
        mine = [_own_blocks(gfull[n], group[n], lax.axis_index("c")) for n in names]
        sums = [_add_pairs(a, b, "pairsum_" + n) for n, a, b in zip(names, mine, theirs)]
        lands = _chip_exchange(sums, "exchange_" + gname)
        for n, land in zip(names, lands):
            shape = wts[n].shape
            cols = shape[-1]
            outs = _adamw_call(wts[n].reshape(-1, cols), mom[n].reshape(-1, cols), var[n].reshape(-1, cols),
                               land.reshape(N_CHIP, -1, cols), "adamw_" + n, True)
            put(n, outs, shape)

    cols = w_mod.shape[-1]
    outs = _adamw_call(w_mod.reshape(-1, cols), m_w_mod.reshape(-1, cols), v_w_mod.reshape(-1, cols),
                       gl["w_mod"].reshape(-1, cols), "adamw_w_mod", False)
    put("w_mod", outs, w_mod.shape)

    share = {"norm_mix": gl["norm_mix"], "norm_ffn": gl["norm_ffn"], "ffn_conv_b": gl["ffn"]["ffn_conv_b"]}
    share.update({n: gl["mixer"][0][n] for n in _MLA_GAINS})
    share.update({n: gl["mixer"][1][n] for n in _GQA_GAINS})
    whole = {n: jnp.where(me == 0, gl[n], 0.0) for n in _SUMMED}
    order = _SUMMED + _SHARED
    parts = [whole[n] if n in whole else share[n] for n in order]
    sizes = [p.size for p in parts]
    taps_g = gl["ffn"]["ffn_conv_w"]
    pack_g = jnp.concatenate([p.reshape(-1) for p in parts] + [taps_g.reshape(-1)])
    pad = (-pack_g.size // LANES) % SUBLANES * LANES
    pack_g = jnp.concatenate([pack_g, jnp.zeros((pad,), F32)])
    land = _gather_rows(_rows128(pack_g), "smallgrad_gather")

    def pack(src):
        flat = jnp.concatenate([src[n].reshape(-1) for n in order] + [jnp.zeros((taps_g.size + pad,), F32)])
        return _rows128(flat)

    outs = _adamw_call(pack(wts), pack(mom), pack(var), land, "adamw_small", True)
    at = 0
    for n, size in zip(order, sizes):
        put(n, [o.reshape(-1)[at:at + size] for o in outs], wts[n].shape)
        at += size
    taps_sum = outs[0].reshape(-1)[at:at + taps_g.size].reshape(taps_g.shape)
    my_taps = lax.dynamic_slice_in_dim(taps_sum, me * ffn_conv_w.shape[-1], ffn_conv_w.shape[-1], axis=2)
    outs = _adamw_call(_rows128(ffn_conv_w), _rows128(m_ffn_conv_w), _rows128(v_ffn_conv_w), _rows128(my_taps),
                       "adamw_conv_w", False)
    put("ffn_conv_w", outs, ffn_conv_w.shape)

    return (loss, gl["x"][None], *[grads[n] for n in _NAMES], *[deltas[n] for n in _NAMES],
            *[new_m[n] for n in _NAMES], *[new_v[n] for n in _NAMES])
```
